```python
import jax
import jax.numpy as jnp
from jax import lax
import numpy as np

D_MODEL = 2048
BATCH = 2
SEQ = 8192
DEPTH = 2

GRID_W = 64
CTX_LEN = 256
EPS = 1e-6
NEG = -1e30
CHUNK = 64
D_FF = 4 * D_MODEL
N_BRANCH = 4

RW_WIDTH = D_MODEL // 4
RW_HEAD = 64
RW_HEADS = RW_WIDTH // RW_HEAD
RW_DECAY_RANK = 64
RW_A_RANK = 64
RW_GATE_RANK = 128
RW_DECAY_SCALE = 0.6065306597

LRU_WIDTH = D_MODEL // 4
LRU_BLOCKS = 8
LRU_BLOCK = LRU_WIDTH // LRU_BLOCKS
LRU_CONV = 4
LRU_PAD_L = 2
LRU_C = 8.0

GLA_HEADS = 4
GLA_V = D_MODEL // 4
GLA_QK = GLA_V // 2
GLA_DK = GLA_QK // GLA_HEADS
GLA_DV = GLA_V // GLA_HEADS
GLA_RANK = 16
GLA_TAU = 16.0

ML_HEADS = 4
ML_WIDTH = D_MODEL // 4
ML_DH = ML_WIDTH // ML_HEADS

IN_LAYOUT = (
    ("rw_r", RW_WIDTH), ("rw_k", RW_WIDTH), ("rw_v", RW_WIDTH),
    ("rw_w", 2 * RW_DECAY_RANK), ("rw_a", 2 * RW_A_RANK), ("rw_g", RW_GATE_RANK),
    ("lru_x", LRU_WIDTH), ("lru_g", LRU_WIDTH),
    ("gla_q", GLA_QK), ("gla_k", GLA_QK), ("gla_v", GLA_V), ("gla_r", GLA_V), ("gla_w", 2 * GLA_RANK),
    ("ml_q", ML_WIDTH), ("ml_k", ML_WIDTH), ("ml_v", ML_WIDTH), ("ml_o", ML_WIDTH),
    ("ml_i", 2 * ML_HEADS), ("ml_f", 2 * ML_HEADS),
)
IN_COLS = sum(s for _, s in IN_LAYOUT)

kernel_name = "bidir_hybrid_rwkv7_rglru_gla_mlstm_prefix_trunk"

F32 = jnp.float32


def _rmsnorm(x, w):
    xf = x.astype(F32)
    y = xf * lax.rsqrt(jnp.mean(xf * xf, axis=-1, keepdims=True) + EPS)
    return (y * w.astype(F32)).astype(x.dtype)


def _head_rmsnorm(y, w, n_heads):
    shp = y.shape
    yh = y.reshape(shp[:-1] + (n_heads, shp[-1] // n_heads))
    yh = yh * lax.rsqrt(jnp.mean(yh * yh, axis=-1, keepdims=True) + EPS)
    return yh.reshape(shp) * w.astype(F32)


def _modulate(x, w, shift, scale):
    return _rmsnorm(x, w) * (1 + scale) + shift


def _split_cols(u):
    sizes = [s for _, s in IN_LAYOUT]
    idx = np.cumsum(sizes)[:-1].tolist()
    parts = jnp.split(u, idx, axis=-1)
    return dict(zip([n for n, _ in IN_LAYOUT], parts))


def _bidir(u):
    return jnp.stack([u, jnp.flip(u, axis=1)], axis=0)


def _flip_bwd(u2):
    return jnp.stack([u2[0], jnp.flip(u2[1], axis=1)], axis=0)


def _unbidir_sum(y2):
    return y2[0] + jnp.flip(y2[1], axis=1)


def _to_col_major(u, rows):
    b, l, ch = u.shape
    return jnp.swapaxes(u.reshape(b, rows, GRID_W, ch), 1, 2).reshape(b, l, ch)


def _to_row_major(u, rows):
    b, l, ch = u.shape
    return jnp.swapaxes(u.reshape(b, GRID_W, rows, ch), 1, 2).reshape(b, l, ch)


def _chunks(t):
    z, b, l = t.shape[:3]
    t = t.reshape((z, b, l // CHUNK, CHUNK) + t.shape[3:])
    return jnp.moveaxis(t, 2, 0)


def _unchunks(t):
    t = jnp.moveaxis(t, 0, 2)
    return t.reshape(t.shape[:2] + (-1,) + t.shape[4:])


def _rwkv7_mixer(cols, w_up, w0, a_up, a0, g_up, k_k, k_a, r_k, ln_w, s0):
    r = cols["rw_r"].astype(F32)
    k = cols["rw_k"].astype(F32)
    v = cols["rw_v"].astype(F32)
    b, l, _ = r.shape
    wd = cols["rw_w"].astype(F32).reshape(b, l, 2, RW_DECAY_RANK)
    ad = cols["rw_a"].astype(F32).reshape(b, l, 2, RW_A_RANK)
    decay = jnp.exp(-RW_DECAY_SCALE * jax.nn.sigmoid(
        jnp.einsum("blzr,zrc->zblc", jnp.tanh(wd), w_up) + w0[:, None, None, :]))
    a = jax.nn.sigmoid(jnp.einsum("blzr,zrc->zblc", ad, a_up) + a0[:, None, None, :])
    kk = (k * k_k).reshape(b, l, RW_HEADS, RW_HEAD)
    kk = kk / jnp.maximum(jnp.sqrt(jnp.sum(kk * kk, axis=-1, keepdims=True)), 1e-12)
    kk = kk.reshape(b, l, RW_WIDTH)
    kd = k[None] * (1 + (a - 1) * k_a)
    kk2 = _bidir(kk)

    def heads(t):
        return jnp.moveaxis(t.reshape(2, b, l, RW_HEADS, RW_HEAD), 2, 0)

    xs = (heads(_bidir(r)), heads(_flip_bwd(decay)), heads(_flip_bwd(kd)), heads(_bidir(v)),
          heads(kk2), heads(kk2 * _flip_bwd(a)))

    def step(S, inp):
        rt, wt, kt, vt, kkt, bt = inp
        sa = jnp.einsum("zbhvk,zbhk->zbhv", S, -kkt)
        S = S * wt[..., None, :] + sa[..., :, None] * bt[..., None, :] + vt[..., :, None] * kt[..., None, :]
        return S, jnp.einsum("zbhvk,zbhk->zbhv", S, rt)

    s_fin, y2 = lax.scan(step, s0, xs)
    y2 = jnp.moveaxis(y2, 0, 2).reshape(2, b, l, RW_WIDTH)
    y = _head_rmsnorm(_unbidir_sum(y2), ln_w, RW_HEADS)
    bonus = jnp.sum((r * k * r_k).reshape(b, l, RW_HEADS, RW_HEAD), axis=-1, keepdims=True) \
        * v.reshape(b, l, RW_HEADS, RW_HEAD)
    g = jax.nn.sigmoid(cols["rw_g"].astype(F32)) @ g_up
    return (y + bonus.reshape(b, l, RW_WIDTH)) * g, s_fin


def _rglru_mixer(cols, conv_w, conv_b, w_a, b_a, w_x, b_x, lam, h0, rows):
    xb = cols["lru_x"].astype(F32)
    gb = cols["lru_g"].astype(F32)
    b, l, ch = xb.shape
    if rows is not None:
        xb = _to_col_major(xb, rows)
    xp = jnp.pad(xb, ((0, 0), (LRU_PAD_L, LRU_CONV - 1 - LRU_PAD_L), (0, 0)))
    xc = conv_b.astype(F32)
    for j in range(LRU_CONV):
        xc = xc + xp[:, j:j + l] * conv_w[j]
    xblk = xc.reshape(b, l, LRU_BLOCKS, LRU_BLOCK)
    rg = jax.nn.sigmoid(jnp.einsum("blnc,zncj->zblnj", xblk, w_a).reshape(2, b, l, ch) + b_a[:, None, None, :])
    ig = jax.nn.sigmoid(jnp.einsum("blnc,zncj->zblnj", xblk, w_x).reshape(2, b, l, ch) + b_x[:, None, None, :])
    log_a = -LRU_C * rg * jax.nn.softplus(-lam)[:, None, None, :]
    a = jnp.exp(log_a)
    bterm = jnp.sqrt(-jnp.expm1(2.0 * log_a)) * ig * xc[None]
    a2 = _flip_bwd(a)
    b2 = _flip_bwd(bterm)

    def combine(p, q):
        return (p[0] * q[0], q[0] * p[1] + q[1])

    a_cum, b_cum = lax.associative_scan(combine, (a2, b2), axis=2)
    h = a_cum * h0[:, :, None, :] + b_cum
    h_fin = h[:, :, -1]
    y = _unbidir_sum(h)
    if rows is not None:
        y = _to_row_major(y, rows)
    return y * jax.nn.gelu(gb), h_fin


def _gla_chunked(q, k, v, la, s0):
    tri = jnp.tril(jnp.ones((CHUNK, CHUNK), dtype=bool))

    def step(S, inp):
        qc, kc, vc, lac = inp
        bcum = jnp.cumsum(lac, axis=2)
        b_end = bcum[:, :, -1]
        qe = qc * jnp.exp(bcum)
        att = jnp.einsum("zbthd,zbshd->zbhts", qe, kc * jnp.exp(-bcum))
        att = jnp.where(tri, att, 0.0)
        o = jnp.einsum("zbhts,zbshv->zbthv", att, vc) + jnp.einsum("zbthd,zbhdv->zbthv", qe, S)
        S = jnp.exp(b_end)[..., None] * S + jnp.einsum(
            "zbshd,zbshv->zbhdv", kc * jnp.exp(b_end[:, :, None] - bcum), vc)
        return S, o

    s_fin, o = lax.scan(step, s0, (_chunks(q), _chunks(k), _chunks(v), _chunks(la)))
    return _unchunks(o), s_fin


def _gla_mixer(cols, w_up, w0, ln_w, s0):
    q = cols["gla_q"].astype(F32) * GLA_DK ** -0.5
    k = cols["gla_k"].astype(F32)
    v = cols["gla_v"].astype(F32)
    r = cols["gla_r"].astype(F32)
    b, l, _ = q.shape
    wd = cols["gla_w"].astype(F32).reshape(b, l, 2, GLA_RANK)
    la = jax.nn.log_sigmoid(jnp.einsum("blzr,zrc->zblc", wd, w_up) + w0[:, None, None, :]) / GLA_TAU
    q2 = _bidir(q).reshape(2, b, l, GLA_HEADS, GLA_DK)
    k2 = _bidir(k).reshape(2, b, l, GLA_HEADS, GLA_DK)
    v2 = _bidir(v).reshape(2, b, l, GLA_HEADS, GLA_DV)
    la2 = _flip_bwd(la).reshape(2, b, l, GLA_HEADS, GLA_DK)
    o2, s_fin = _gla_chunked(q2, k2, v2, la2, s0)
    y = _head_rmsnorm(_unbidir_sum(o2).reshape(b, l, GLA_V), ln_w, GLA_HEADS)
    return y * jax.nn.silu(r), s_fin


def _mlstm_chunked(q, k, v, ig, lf, state0):
    tri = jnp.tril(jnp.ones((CHUNK, CHUNK), dtype=bool))

    def step(carry, inp):
        cm, n, m = carry
        qc, kc, vc, ic, fc = inp
        fcum = jnp.moveaxis(jnp.cumsum(fc, axis=2), 2, -1)
        ic = jnp.moveaxis(ic, 2, -1)
        f_end = fcum[..., -1]
        dlog = fcum[..., :, None] - fcum[..., None, :] + ic[..., None, :]
        dlog = jnp.where(tri, dlog, NEG)
        inter = fcum + m[..., None]
        mt = jnp.maximum(inter, jnp.max(dlog, axis=-1))
        w = jnp.exp(dlog - mt[..., None]) * jnp.einsum("zbthd,zbshd->zbhts", qc, kc)
        ei = jnp.exp(inter - mt)
        num = jnp.einsum("zbhts,zbshv->zbhtv", w, vc) + ei[..., None] * jnp.einsum("zbhvd,zbthd->zbhtv", cm, qc)
        den = jnp.sum(w, axis=-1) + ei * jnp.einsum("zbhd,zbthd->zbht", n, qc)
        h = num / jnp.maximum(jnp.abs(den), jnp.exp(-mt))[..., None]
        g = f_end[..., None] - fcum + ic
        m_new = jnp.maximum(f_end + m, jnp.max(g, axis=-1))
        dec = jnp.exp(f_end + m - m_new)
        wg = jnp.exp(g - m_new[..., None])
        cm = dec[..., None, None] * cm + jnp.einsum("zbhs,zbshv,zbshd->zbhvd", wg, vc, kc)
        n = dec[..., None] * n + jnp.einsum("zbhs,zbshd->zbhd", wg, kc)
        return (cm, n, m_new), jnp.moveaxis(h, 3, 2)

    st, h = lax.scan(step, state0, (_chunks(q), _chunks(k), _chunks(v), _chunks(ig), _chunks(lf)))
    return _unchunks(h), st


def _mlstm_mixer(cols, i_b, f_b, ln_w, st0):
    q = cols["ml_q"].astype(F32)
    k = cols["ml_k"].astype(F32) * ML_DH ** -0.5
    v = cols["ml_v"].astype(F32)
    o = cols["ml_o"].astype(F32)
    b, l, _ = q.shape
    ig = jnp.moveaxis(cols["ml_i"].astype(F32).reshape(b, l, 2, ML_HEADS) + i_b, 2, 0)
    lf = jnp.moveaxis(jax.nn.log_sigmoid(cols["ml_f"].astype(F32).reshape(b, l, 2, ML_HEADS) + f_b), 2, 0)
    q2 = _bidir(q).reshape(2, b, l, ML_HEADS, ML_DH)
    k2 = _bidir(k).reshape(2, b, l, ML_HEADS, ML_DH)
    v2 = _bidir(v).reshape(2, b, l, ML_HEADS, ML_DH)
    h2, st = _mlstm_chunked(q2, k2, v2, _flip_bwd(ig), _flip_bwd(lf), st0)
    y = _head_rmsnorm(_unbidir_sum(h2).reshape(b, l, ML_WIDTH), ln_w, ML_HEADS)
    return y * jax.nn.sigmoid(o), st


def _zero_states(b):
    return (jnp.zeros((2, b, RW_HEADS, RW_HEAD, RW_HEAD), F32),
            jnp.zeros((2, b, LRU_WIDTH), F32),
            jnp.zeros((2, b, GLA_HEADS, GLA_DK, GLA_DV), F32),
            (jnp.zeros((2, b, ML_HEADS, ML_DH, ML_DH), F32),
             jnp.zeros((2, b, ML_HEADS, ML_DH), F32),
             jnp.full((2, b, ML_HEADS), NEG, F32)))


def _token_mixers(u, rows, st, rw, lru, gla, ml):
    cols = _split_cols(u)
    y_a, s_a = _rwkv7_mixer(cols, *rw, st[0])
    y_b, s_b = _rglru_mixer(cols, *lru, st[1], rows)
    y_c, s_c = _gla_mixer(cols, *gla, st[2])
    y_d, s_d = _mlstm_mixer(cols, *ml, st[3])
    return (y_a, y_b, y_c, y_d), (s_a, s_b, s_c, s_d)


def _merge(h, ys, br_w, gate_w, gate_b, out_w):
    acc = jax.nn.sigmoid(h @ gate_w[0] + gate_b[0]) * (ys[0].astype(h.dtype) @ br_w[0])
    for i in range(1, N_BRANCH):
        acc = acc + jax.nn.sigmoid(h @ gate_w[i] + gate_b[i]) * (ys[i].astype(h.dtype) @ br_w[i])
    return acc @ out_w


def _sqrelu_mlp(h, w1, w2):
    return jnp.square(jax.nn.relu(h @ w1)) @ w2


def setup_inputs(seed: int = 0) -> dict:
    key = jax.random.key(seed)
    ks = iter(jax.random.split(key, 48))

    def nrm(shape, scale):
        return jax.random.normal(next(ks), shape, F32) * scale

    def gain(shape):
        return 1.0 + nrm(shape, 0.02)

    L = DEPTH
    D = D_MODEL
    u = jax.random.uniform(next(ks), (L, 2, LRU_WIDTH), F32, 0.9, 0.999)
    a_base = u ** (1.0 / LRU_C)
    lru_lambda = jnp.log(a_base) - jnp.log1p(-a_base)
    return {
        "x": nrm((BATCH, SEQ, D), 1.0),
        "c": nrm((BATCH, D), 1.0),
        "ctx": nrm((BATCH, CTX_LEN, D), 1.0),
        "c_ctx": nrm((D,), 1.0),
        "ada_w": nrm((L, D, 6 * D), 0.5 * D ** -0.5),
        "ada_b": nrm((L, 6 * D), 0.02),
        "norm_mix_w": gain((L, D)),
        "w_in": nrm((L, D, IN_COLS), D ** -0.5),
        "rw_w_up": nrm((L, 2, RW_DECAY_RANK, RW_WIDTH), RW_DECAY_RANK ** -0.5),
        "rw_w0": nrm((L, 2, RW_WIDTH), 1.5) - 1.0,
        "rw_a_up": nrm((L, 2, RW_A_RANK, RW_WIDTH), RW_A_RANK ** -0.5),
        "rw_a0": nrm((L, 2, RW_WIDTH), 0.1),
        "rw_g_up": nrm((L, RW_GATE_RANK, RW_WIDTH), RW_GATE_RANK ** -0.5),
        "rw_k_k": 0.85 + nrm((L, RW_WIDTH), 0.02),
        "rw_k_a": gain((L, RW_WIDTH)),
        "rw_r_k": nrm((L, RW_WIDTH), 0.1),
        "rw_ln_w": gain((L, RW_WIDTH)),
        "lru_conv_w": nrm((L, LRU_CONV, LRU_WIDTH), LRU_CONV ** -0.5),
        "lru_conv_b": nrm((L, LRU_WIDTH), 0.02),
        "lru_w_a": nrm((L, 2, LRU_BLOCKS, LRU_BLOCK, LRU_BLOCK), LRU_BLOCK ** -0.5),
        "lru_b_a": nrm((L, 2, LRU_WIDTH), 0.02),
        "lru_w_x": nrm((L, 2, LRU_BLOCKS, LRU_BLOCK, LRU_BLOCK), LRU_BLOCK ** -0.5),
        "lru_b_x": nrm((L, 2, LRU_WIDTH), 0.02),
        "lru_lambda": lru_lambda,
        "gla_w_up": nrm((L, 2, GLA_RANK, GLA_QK), GLA_RANK ** -0.5),
        "gla_w0": nrm((L, 2, GLA_QK), 0.5) + 2.0,
        "gla_ln_w": gain((L, GLA_V)),
        "ml_i_b": nrm((L, 2, ML_HEADS), 0.1),
        "ml_f_b": jnp.linspace(3.0, 6.0, ML_HEADS, dtype=F32)[None, None, :] + nrm((L, 2, ML_HEADS), 0.1),
        "ml_ln_w": gain((L, ML_WIDTH)),
        "br_w": nrm((L, N_BRANCH, RW_WIDTH, D), RW_WIDTH ** -0.5),
        "gate_w": nrm((L, N_BRANCH, D, D), D ** -0.5),
        "gate_b": nrm((L, N_BRANCH, D), 0.02),
        "out_w": nrm((L, D, D), D ** -0.5),
        "norm_ffn_w": gain((L, D)),
        "ffn_w1": nrm((L, D, D_FF), D ** -0.5),
        "ffn_w2": nrm((L, D_FF, D), D_FF ** -0.5),
        "final_norm_w": gain((D,)),
    }


def reference(x, c, ctx, c_ctx, ada_w, ada_b, norm_mix_w, w_in, rw_w_up, rw_w0, rw_a_up, rw_a0, rw_g_up,
              rw_k_k, rw_k_a, rw_r_k, rw_ln_w, lru_conv_w, lru_conv_b, lru_w_a, lru_b_a, lru_w_x, lru_b_x,
              lru_lambda, gla_w_up, gla_w0, gla_ln_w, ml_i_b, ml_f_b, ml_ln_w, br_w, gate_w, gate_b, out_w,
              norm_ffn_w, ffn_w1, ffn_w2, final_norm_w):
    rows = x.shape[1] // GRID_W
    n_ctx = ctx.shape[0]
    for l in range(DEPTH):
        mod_x = (jax.nn.silu(c) @ ada_w[l] + ada_b[l])[:, None, :]
        mod_c = (jax.nn.silu(c_ctx) @ ada_w[l] + ada_b[l])[None, None, :]
        sh1, sc1, g1, sh2, sc2, g2 = jnp.split(mod_x, 6, axis=-1)
        csh1, csc1, cg1, csh2, csc2, cg2 = jnp.split(mod_c, 6, axis=-1)
        rw = (rw_w_up[l], rw_w0[l], rw_a_up[l], rw_a0[l], rw_g_up[l], rw_k_k[l], rw_k_a[l], rw_r_k[l], rw_ln_w[l])
        lru = (lru_conv_w[l], lru_conv_b[l], lru_w_a[l], lru_b_a[l], lru_w_x[l], lru_b_x[l], lru_lambda[l])
        gla = (gla_w_up[l], gla_w0[l], gla_ln_w[l])
        ml = (ml_i_b[l], ml_f_b[l], ml_ln_w[l])
        hc = _modulate(ctx, norm_mix_w[l], csh1, csc1)
        ys_c, st_c = _token_mixers(hc @ w_in[l], None, _zero_states(n_ctx), rw, lru, gla, ml)
        hx = _modulate(x, norm_mix_w[l], sh1, sc1)
        ys_x, _ = _token_mixers(hx @ w_in[l], rows, st_c, rw, lru, gla, ml)
        x = x + g1 * _merge(hx, ys_x, br_w[l], gate_w[l], gate_b[l], out_w[l])
        x = x + g2 * _sqrelu_mlp(_modulate(x, norm_ffn_w[l], sh2, sc2), ffn_w1[l], ffn_w2[l])
        if l < DEPTH - 1:
            ctx = ctx + cg1 * _merge(hc, ys_c, br_w[l], gate_w[l], gate_b[l], out_w[l])
            ctx = ctx + cg2 * _sqrelu_mlp(_modulate(ctx, norm_ffn_w[l], csh2, csc2), ffn_w1[l], ffn_w2[l])
    return _rmsnorm(x, final_norm_w)
```

```python
import functools

import numpy as np
import jax
import jax.numpy as jnp
from jax import lax
from jax.experimental import pallas as pl
from jax.experimental.pallas import tpu as pltpu

F32 = jnp.float32
BF16 = jnp.bfloat16
HI = lax.Precision.HIGHEST

D = 2048
DEPTH = 2
GRID_W = 64
EPS = 1e-6
NEG = -1e30
CH = 64
D_FF = 4 * D
W4 = D // 4

RW_HEAD = 64
RW_HEADS = W4 // RW_HEAD
RW_DECAY_SCALE = 0.6065306597
LRU_BLOCKS = 8
LRU_C = 8.0
GLA_HEADS = 4
GLA_DK = 64
GLA_DV = 128
GLA_RANK = 16
GLA_TAU = 16.0
ML_HEADS = 4
ML_DH = 128

RW_COLS = 3 * W4 + 3 * 128
LRU_COLS = 2 * W4
GLA_COLS = 256 + 256 + 512 + 512 + 128
ML_COLS = 4 * W4 + 128

VMEM_LIMIT = 56 * 1024 * 1024


def _cparams(*sem):
    return pltpu.CompilerParams(dimension_semantics=sem, vmem_limit_bytes=VMEM_LIMIT)


def _dot(a, b, prec=None):
    return jnp.dot(a, b, precision=prec, preferred_element_type=F32)


def _dot_nt(a, b, prec=None):
    return lax.dot_general(a, b, (((1,), (1,)), ((), ())), precision=prec, preferred_element_type=F32)


def _dot_tn(a, b, prec=None):
    return lax.dot_general(a, b, (((0,), (0,)), ((), ())), precision=prec, preferred_element_type=F32)


def _sigmoid(x):
    return 1.0 / (1.0 + jnp.exp(-x))


def _log_sigmoid(x):
    return jnp.minimum(x, 0.0) - jnp.log1p(jnp.exp(-jnp.abs(x)))


def _softplus(x):
    return jnp.maximum(x, 0.0) + jnp.log1p(jnp.exp(-jnp.abs(x)))


def _scan_mask(n, inclusive, reverse):
    r = lax.broadcasted_iota(jnp.int32, (n, n), 0)
    c = lax.broadcasted_iota(jnp.int32, (n, n), 1)
    if reverse:
        return (r <= c) if inclusive else (r < c)
    return (r >= c) if inclusive else (r > c)


def _ada_kernel(s_ref, w_ref, b_ref, o_ref):
    s = s_ref[...]
    s = s * _sigmoid(s)
    o_ref[...] = _dot(s, w_ref[...], HI) + b_ref[...]


def _ada(cc, w, b):
    tn = 1024
    return pl.pallas_call(
        _ada_kernel,
        grid=(6 * D // tn,),
        in_specs=[pl.BlockSpec((8, D), lambda j: (0, 0)),
                  pl.BlockSpec((D, tn), lambda j: (0, j)),
                  pl.BlockSpec((1, tn), lambda j: (0, j))],
        out_specs=pl.BlockSpec((8, tn), lambda j: (0, j)),
        out_shape=jax.ShapeDtypeStruct((8, 6 * D), F32),
        compiler_params=_cparams("parallel"),
        name="ada",
    )(cc, w, b.reshape(1, 6 * D))


def _norm_kernel(x_ref, w_ref, mod_ref, o_ref, *, rows):
    x = x_ref[...]
    y = x * lax.rsqrt(jnp.mean(x * x, axis=-1, keepdims=True) + EPS)
    y = y * w_ref[...]
    if rows is not None:
        shift, scale = rows
        m = mod_ref[...]
        y = y * (1.0 + m[scale:scale + 1]) + m[shift:shift + 1]
    o_ref[...] = y.astype(o_ref.dtype)


def _norm(x, w, mods, rows, out_dtype, tm=256):
    r = x.shape[0]
    g = mods.shape[0]
    per = r // g // tm
    return pl.pallas_call(
        functools.partial(_norm_kernel, rows=rows),
        grid=(r // tm,),
        in_specs=[pl.BlockSpec((tm, D), lambda i: (i, 0)),
                  pl.BlockSpec((1, D), lambda i: (0, 0)),
                  pl.BlockSpec((None, 8, D), lambda i: (i // per, 0, 0))],
        out_specs=pl.BlockSpec((tm, D), lambda i: (i, 0)),
        out_shape=jax.ShapeDtypeStruct((r, D), out_dtype),
        compiler_params=_cparams("parallel"),
        name="norm",
    )(x, w.reshape(1, D), mods)


def _mm_kernel(a_ref, w_ref, o_ref):
    o_ref[...] = _dot(a_ref[...], w_ref[...])


def _mm(a, w, tm=512):
    r, k = a.shape
    n = w.shape[1]
    tm = min(tm, r)
    return pl.pallas_call(
        _mm_kernel,
        grid=(r // tm,),
        in_specs=[pl.BlockSpec((tm, k), lambda i: (i, 0)),
                  pl.BlockSpec((k, n), lambda i: (0, 0))],
        out_specs=pl.BlockSpec((tm, n), lambda i: (i, 0)),
        out_shape=jax.ShapeDtypeStruct((r, n), F32),
        compiler_params=_cparams("parallel"),
        name="in_proj",
    )(a, w)


def _gate_kernel(hx_ref, ys_ref, gw_ref, gb_ref, bw_ref, o_ref):
    hx = hx_ref[...]
    acc = None
    for i in range(4):
        g = _dot(hx, gw_ref[i]) + gb_ref[i]
        p = _dot(ys_ref[:, i * W4:(i + 1) * W4], bw_ref[i])
        t = _sigmoid(g) * p
        acc = t if acc is None else acc + t
    o_ref[...] = acc.astype(o_ref.dtype)


def _gated_sum(hx, ys, gw, gb, bw, tm=512, tn=512):
    r = hx.shape[0]
    tm = min(tm, r)
    return pl.pallas_call(
        _gate_kernel,
        grid=(D // tn, r // tm),
        in_specs=[pl.BlockSpec((tm, D), lambda j, i: (i, 0)),
                  pl.BlockSpec((tm, D), lambda j, i: (i, 0)),
                  pl.BlockSpec((4, D, tn), lambda j, i: (0, 0, j)),
                  pl.BlockSpec((4, 1, tn), lambda j, i: (0, 0, j)),
                  pl.BlockSpec((4, W4, tn), lambda j, i: (0, 0, j))],
        out_specs=pl.BlockSpec((tm, tn), lambda j, i: (i, j)),
        out_shape=jax.ShapeDtypeStruct((r, D), BF16),
        compiler_params=_cparams("parallel", "parallel"),
        name="gated_sum",
    )(hx, ys, gw, gb, bw)


def _proj_res_kernel(a_ref, w_ref, x_ref, mod_ref, o_ref, *, gate_row):
    y = _dot(a_ref[...], w_ref[...])
    m = mod_ref[...]
    o_ref[...] = x_ref[...] + m[gate_row:gate_row + 1] * y


def _proj_residual(a, w, x, mods, gate_row, tm=512):
    r = a.shape[0]
    tm = min(tm, r)
    per = r // mods.shape[0] // tm
    return pl.pallas_call(
        functools.partial(_proj_res_kernel, gate_row=gate_row),
        grid=(r // tm,),
        in_specs=[pl.BlockSpec((tm, D), lambda i: (i, 0)),
                  pl.BlockSpec((D, D), lambda i: (0, 0)),
                  pl.BlockSpec((tm, D), lambda i: (i, 0)),
                  pl.BlockSpec((None, 8, D), lambda i: (i // per, 0, 0))],
        out_specs=pl.BlockSpec((tm, D), lambda i: (i, 0)),
        out_shape=jax.ShapeDtypeStruct((r, D), F32),
        compiler_params=_cparams("parallel"),
        name="out_proj",
    )(a, w, x, mods)


def _ffn_kernel(h_ref, w1_ref, w2_ref, x_ref, mod_ref, o_ref, acc_ref, *, gate_row):
    f = pl.program_id(1)

    @pl.when(f == 0)
    def _():
        acc_ref[...] = jnp.zeros_like(acc_ref)

    a = jnp.maximum(_dot(h_ref[...], w1_ref[...]), 0.0)
    acc_ref[...] += _dot((a * a).astype(BF16), w2_ref[...])

    @pl.when(f == pl.num_programs(1) - 1)
    def _():
        m = mod_ref[...]
        o_ref[...] = x_ref[...] + m[gate_row:gate_row + 1] * acc_ref[...]


def _ffn(hn, w1, w2, x, mods, gate_row, tm=512, tf=512):
    r = hn.shape[0]
    tm = min(tm, r)
    per = r // mods.shape[0] // tm
    return pl.pallas_call(
        functools.partial(_ffn_kernel, gate_row=gate_row),
        grid=(r // tm, D_FF // tf),
        in_specs=[pl.BlockSpec((tm, D), lambda i, f: (i, 0)),
                  pl.BlockSpec((D, tf), lambda i, f: (0, f)),
                  pl.BlockSpec((tf, D), lambda i, f: (f, 0)),
                  pl.BlockSpec((tm, D), lambda i, f: (i, 0)),
                  pl.BlockSpec((None, 8, D), lambda i, f: (i // per, 0, 0))],
        out_specs=pl.BlockSpec((tm, D), lambda i, f: (i, 0)),
        out_shape=jax.ShapeDtypeStruct((r, D), F32),
        scratch_shapes=[pltpu.VMEM((tm, D), F32)],
        compiler_params=_cparams("parallel", "arbitrary"),
        name="ffn",
    )(hn, w1, w2, x, mods)


def _rwkv_chunk(z, r, k, v, wd, ad, w_up, w0, a_up, a0, k_k, k_a, ehead, s_ref):
    rev = z == 1
    m_incl = _scan_mask(CH, True, rev)
    m_strict = _scan_mask(CH, False, rev)
    last = 0 if rev else CH - 1

    wdz = wd[:, 64 * z:64 * z + 64]
    adz = ad[:, 64 * z:64 * z + 64]
    lw = -RW_DECAY_SCALE * _sigmoid(_dot(jnp.tanh(wdz), w_up, HI) + w0)
    a = _sigmoid(_dot(adz, a_up, HI) + a0)
    kk = k * k_k
    kk = kk / jnp.maximum(jnp.sqrt(_dot(kk * kk, ehead, HI)), 1e-12)
    kd = k * (1.0 + (a - 1.0) * k_a)
    bb = kk * a

    lc_incl = _dot(m_incl.astype(F32), lw, HI)
    lc_excl = lc_incl - lw
    p_incl = jnp.exp(lc_incl)
    p_inv = jnp.exp(-lc_incl)
    at = -kk * jnp.exp(lc_excl)
    bt = bb * p_inv
    kt = kd * p_inv
    rt = r * p_incl
    p_end = p_incl[last:last + 1]

    ys = []
    for h in range(RW_HEADS):
        sl = slice(RW_HEAD * h, RW_HEAD * (h + 1))
        s0 = s_ref[z, h]
        vh = v[:, sl]
        g = _dot_nt(jnp.concatenate([at[:, sl], rt[:, sl]], axis=0),
                    jnp.concatenate([bt[:, sl], kt[:, sl]], axis=0), HI)
        ab = jnp.where(m_strict, g[:CH, :CH], 0.0)
        ak = jnp.where(m_strict, g[:CH, CH:], 0.0)
        rb = jnp.where(m_incl, g[CH:, :CH], 0.0)
        rk = jnp.where(m_incl, g[CH:, CH:], 0.0)
        s0t = _dot_nt(jnp.concatenate([at[:, sl], rt[:, sl]], axis=0), s0, HI)
        u = s0t[:CH] + _dot(ak, vh, HI)
        lp = ab
        for j in range(6):
            u = u + _dot(lp, u, HI)
            if j < 5:
                lp = _dot(lp, lp, HI)
        ys.append(s0t[CH:] + _dot(rb, u, HI) + _dot(rk, vh, HI))
        s_new = s0 + _dot_tn(u, bt[:, sl], HI) + _dot_tn(vh, kt[:, sl], HI)
        s_ref[z, h] = s_new * p_end[:, sl]
    return jnp.concatenate(ys, axis=1)


def _rwkv_kernel(rf, kf, vf, wf, af, rb, kb, vb, wb, ab, w_up, w0, a_up, a0, k_k, k_a, ehead, s0_ref,
                 yf_ref, yb_ref, s_ref, *, nchunk):
    @pl.when(pl.program_id(1) == 0)
    def _():
        s_ref[...] = s0_ref[...]

    def body(ci, carry):
        of = pl.multiple_of(ci * CH, CH)
        ob = pl.multiple_of((nchunk - 1 - ci) * CH, CH)
        for z, (rr, kr, vr, wr, ar, yr, off) in enumerate(((rf, kf, vf, wf, af, yf_ref, of),
                                                            (rb, kb, vb, wb, ab, yb_ref, ob))):
            rows = pl.ds(off, CH)
            yr[rows, :] = _rwkv_chunk(z, rr[rows, :], kr[rows, :], vr[rows, :], wr[rows, :], ar[rows, :],
                                      w_up[z], w0[z], a_up[z], a0[z], k_k[...], k_a[...], ehead[...], s_ref)
        return carry

    lax.fori_loop(0, nchunk, body, 0)


def _rwkv(u, n_b, w_up, w0, a_up, a0, k_k, k_a, ehead, s0, tb=256):
    rtot = u.shape[0]
    nblk = rtot // n_b // tb

    def fwd(col):
        return lambda b, j: (b * nblk + j, col)

    def bwd(col):
        return lambda b, j: (b * nblk + nblk - 1 - j, col)

    def seq_specs(imap):
        return [pl.BlockSpec((tb, W4), imap(0)), pl.BlockSpec((tb, W4), imap(1)), pl.BlockSpec((tb, W4), imap(2)),
                pl.BlockSpec((tb, 128), imap(12)), pl.BlockSpec((tb, 128), imap(13))]

    full = lambda shp: pl.BlockSpec(shp, lambda b, j: (0,) * len(shp))
    st_spec = pl.BlockSpec((None, 2, RW_HEADS, RW_HEAD, RW_HEAD), lambda b, j: (b, 0, 0, 0, 0))
    return pl.pallas_call(
        functools.partial(_rwkv_kernel, nchunk=tb // CH),
        grid=(n_b, nblk),
        in_specs=seq_specs(fwd) + seq_specs(bwd) + [
            full((2, 64, W4)), full((2, 1, W4)), full((2, 64, W4)), full((2, 1, W4)),
            full((1, W4)), full((1, W4)), full((W4, W4)), st_spec],
        out_specs=[pl.BlockSpec((tb, W4), fwd(0)), pl.BlockSpec((tb, W4), bwd(0)), st_spec],
        out_shape=[jax.ShapeDtypeStruct((rtot, W4), F32), jax.ShapeDtypeStruct((rtot, W4), F32),
                   jax.ShapeDtypeStruct(s0.shape, F32)],
        compiler_params=_cparams("parallel", "arbitrary"),
        name="rwkv7",
    )(u, u, u, u, u, u, u, u, u, u, w_up, w0, a_up, a0, k_k, k_a, ehead, s0)


def _lru_kernel(pf, cf, nf, pb, cb, nb_, conv_w, conv_b, wa, ba, wx, bx, lam, h0_ref,
                hf_ref, hb_ref, h_ref, ext, a_s, b_s, *, t, nblk):
    j = pl.program_id(1)

    @pl.when(j == 0)
    def _():
        h_ref[...] = h0_ref[...]

    for z, (pr, cr, nr, blk) in enumerate(((pf, cf, nf, j), (pb, cb, nb_, nblk - 1 - j))):
        ext[6:8, :] = jnp.where(blk > 0, pr[t - 2:t, :], 0.0)
        ext[8:8 + t, :] = cr[...]
        ext[8 + t:9 + t, :] = jnp.where(blk < nblk - 1, nr[0:1, :], 0.0)
        xc = conv_b[...] + ext[6:6 + t, :] * conv_w[0:1, :]
        for tap in range(1, 4):
            xc = xc + ext[6 + tap:6 + tap + t, :] * conv_w[tap:tap + 1, :]
        rg = _sigmoid(_dot(xc, wa[z], HI) + ba[z])
        ig = _sigmoid(_dot(xc, wx[z], HI) + bx[z])
        log_a = -LRU_C * rg * _softplus(-lam[z])
        a = jnp.exp(log_a)
        a_s[z] = a
        b_s[z] = jnp.sqrt(-jnp.tanh(log_a) * (a * a + 1.0)) * ig * xc

    def step(i, carry):
        h_f, h_b = carry
        rf = pl.ds(i, 1)
        rb = pl.ds(t - 1 - i, 1)
        h_f = a_s[0, rf, :] * h_f + b_s[0, rf, :]
        h_b = a_s[1, rb, :] * h_b + b_s[1, rb, :]
        hf_ref[rf, :] = h_f
        hb_ref[rb, :] = h_b
        return h_f, h_b

    h_f, h_b = lax.fori_loop(0, t, step, (h_ref[0:1, :], h_ref[1:2, :]), unroll=8)
    h_ref[0:1, :] = h_f
    h_ref[1:2, :] = h_b


def _gla_chunk(z, q, k, v, wd, w_up, w0, s_ref):
    rev = z == 1
    m_incl = _scan_mask(CH, True, rev)
    last = 0 if rev else CH - 1
    la = _log_sigmoid(_dot(wd[:, GLA_RANK * z:GLA_RANK * (z + 1)], w_up, HI) + w0) * (1.0 / GLA_TAU)
    bcum = _dot(m_incl.astype(F32), la, HI)
    b_end = bcum[last:last + 1]
    qe = q * (GLA_DK ** -0.5) * jnp.exp(bcum)
    ke = k * jnp.exp(-bcum)
    kd = k * jnp.exp(b_end - bcum)
    e_end = jnp.exp(b_end)
    outs = []
    for h in range(GLA_HEADS):
        sl = slice(GLA_DK * h, GLA_DK * (h + 1))
        vh = v[:, GLA_DV * h:GLA_DV * (h + 1)]
        st = s_ref[z, h]
        att = jnp.where(m_incl, _dot_nt(qe[:, sl], ke[:, sl], HI), 0.0)
        outs.append(_dot(att, vh, HI) + _dot_nt(qe[:, sl], st, HI))
        s_ref[z, h] = st * e_end[:, sl] + _dot_tn(vh, kd[:, sl], HI)
    return jnp.concatenate(outs, axis=1)


def _gla_kernel(qf, kf, vf, wf, qb, kb, vb, wb, w_up, w0, s0_ref, of_ref, ob_ref, s_ref, *, nchunk):
    @pl.when(pl.program_id(1) == 0)
    def _():
        s_ref[...] = s0_ref[...]

    def body(ci, carry):
        of = pl.multiple_of(ci * CH, CH)
        ob = pl.multiple_of((nchunk - 1 - ci) * CH, CH)
        for z, (qr, kr, vr, wr, orf, off) in enumerate(((qf, kf, vf, wf, of_ref, of), (qb, kb, vb, wb, ob_ref, ob))):
            rows = pl.ds(off, CH)
            orf[rows, :] = _gla_chunk(z, qr[rows, :], kr[rows, :], vr[rows, :], wr[rows, :], w_up[z], w0[z], s_ref)
        return carry

    lax.fori_loop(0, nchunk, body, 0)


def _gla(u, n_b, w_up, w0, s0, tb=256):
    rtot = u.shape[0]
    nblk = rtot // n_b // tb

    def fwd(col):
        return lambda b, j: (b * nblk + j, col)

    def bwd(col):
        return lambda b, j: (b * nblk + nblk - 1 - j, col)

    def seq_specs(imap):
        return [pl.BlockSpec((tb, 256), imap(0)), pl.BlockSpec((tb, 256), imap(1)), pl.BlockSpec((tb, 512), imap(1)),
                pl.BlockSpec((tb, 128), imap(12))]

    full = lambda shp: pl.BlockSpec(shp, lambda b, j: (0,) * len(shp))
    st_spec = pl.BlockSpec((None, 2, GLA_HEADS, GLA_DV, GLA_DK), lambda b, j: (b, 0, 0, 0, 0))
    return pl.pallas_call(
        functools.partial(_gla_kernel, nchunk=tb // CH),
        grid=(n_b, nblk),
        in_specs=seq_specs(fwd) + seq_specs(bwd) + [full((2, GLA_RANK, 256)), full((2, 1, 256)), st_spec],
        out_specs=[pl.BlockSpec((tb, W4), fwd(0)), pl.BlockSpec((tb, W4), bwd(0)), st_spec],
        out_shape=[jax.ShapeDtypeStruct((rtot, W4), F32), jax.ShapeDtypeStruct((rtot, W4), F32),
                   jax.ShapeDtypeStruct(s0.shape, F32)],
        compiler_params=_cparams("parallel", "arbitrary"),
        name="gla",
    )(u, u, u, u, u, u, u, u, w_up, w0, s0)


def _mlstm_chunk(z, q, k, v, gates, bias, cm_ref, n_ref, m_ref):
    rev = z == 1
    m_incl = _scan_mask(CH, True, rev)
    last = 0 if rev else CH - 1
    pre = gates + bias
    lf = _log_sigmoid(pre)
    fcum = _dot(m_incl.astype(F32), lf, HI)
    pre_t = pre.T
    fcum_t = fcum.T
    k = k * (ML_DH ** -0.5)
    outs = []
    for h in range(ML_HEADS):
        ci = ML_HEADS * z + h
        cf = 2 * ML_HEADS + ML_HEADS * z + h
        sl = slice(ML_DH * h, ML_DH * (h + 1))
        qh, kh, vh = q[:, sl], k[:, sl], v[:, sl]
        cm = cm_ref[z, h]
        n = n_ref[z, h]
        m = m_ref[z, h]
        fc = fcum[:, cf:cf + 1]
        ic = pre[:, ci:ci + 1]
        fr = fcum_t[cf:cf + 1, :]
        ir = pre_t[ci:ci + 1, :]
        f_end = fc[last:last + 1]
        dlog = jnp.where(m_incl, fc + (ir - fr), NEG)
        inter = fc + m
        mt = jnp.maximum(inter, jnp.max(dlog, axis=-1, keepdims=True))
        w = jnp.exp(dlog - mt) * _dot_nt(qh, kh, HI)
        ei = jnp.exp(inter - mt)
        num = _dot(w, vh, HI) + ei * _dot_nt(qh, cm, HI)
        den = jnp.sum(w, axis=-1, keepdims=True) + ei * jnp.sum(qh * n, axis=-1, keepdims=True)
        outs.append(num / jnp.maximum(jnp.abs(den), jnp.exp(-mt)))
        g = f_end - fc + ic
        m_new = jnp.maximum(f_end + m, jnp.max(g, axis=0, keepdims=True))
        dec = jnp.exp(f_end + m - m_new)
        wg = jnp.exp(g - m_new)
        cm_ref[z, h] = dec * cm + _dot_tn(vh * wg, kh, HI)
        n_ref[z, h] = dec * n + jnp.sum(kh * wg, axis=0, keepdims=True)
        m_ref[z, h] = m_new
    return jnp.concatenate(outs, axis=1)


def _mlstm_kernel(qf, kf, vf, gf, qb, kb, vb, gb, bias, cm0, n0, m0, of_ref, ob_ref, cm_ref, n_ref, m_ref, *, nchunk):
    @pl.when(pl.program_id(1) == 0)
    def _():
        cm_ref[...] = cm0[...]
        n_ref[...] = n0[...]
        m_ref[...] = m0[...]

    def body(ci, carry):
        of = pl.multiple_of(ci * CH, CH)
        ob = pl.multiple_of((nchunk - 1 - ci) * CH, CH)
        for z, (qr, kr, vr, gr, orf, off) in enumerate(((qf, kf, vf, gf, of_ref, of), (qb, kb, vb, gb, ob_ref, ob))):
            rows = pl.ds(off, CH)
            orf[rows, :] = _mlstm_chunk(z, qr[rows, :], kr[rows, :], vr[rows, :], gr[rows, :], bias[...],
                                        cm_ref, n_ref, m_ref)
        return carry

    lax.fori_loop(0, nchunk, body, 0)


def _mlstm(u, n_b, bias, cm0, n0, m0, tb=256):
    rtot = u.shape[0]
    nblk = rtot // n_b // tb

    def fwd(col):
        return lambda b, j: (b * nblk + j, col)

    def bwd(col):
        return lambda b, j: (b * nblk + nblk - 1 - j, col)

    def seq_specs(imap):
        return [pl.BlockSpec((tb, W4), imap(0)), pl.BlockSpec((tb, W4), imap(1)), pl.BlockSpec((tb, W4), imap(2)),
                pl.BlockSpec((tb, 128), imap(16))]

    full = lambda shp: pl.BlockSpec(shp, lambda b, j: (0,) * len(shp))
    cm_spec = pl.BlockSpec((None, 2, ML_HEADS, ML_DH, ML_DH), lambda b, j: (b, 0, 0, 0, 0))
    n_spec = pl.BlockSpec((None, 2, ML_HEADS, 1, ML_DH), lambda b, j: (b, 0, 0, 0, 0))
    m_spec = pl.BlockSpec((None, 2, ML_HEADS, 1, 1), lambda b, j: (b, 0, 0, 0, 0))
    return pl.pallas_call(
        functools.partial(_mlstm_kernel, nchunk=tb // CH),
        grid=(n_b, nblk),
        in_specs=seq_specs(fwd) + seq_specs(bwd) + [full((1, 128)), cm_spec, n_spec, m_spec],
        out_specs=[pl.BlockSpec((tb, W4), fwd(0)), pl.BlockSpec((tb, W4), bwd(0)), cm_spec, n_spec, m_spec],
        out_shape=[jax.ShapeDtypeStruct((rtot, W4), F32), jax.ShapeDtypeStruct((rtot, W4), F32),
                   jax.ShapeDtypeStruct(cm0.shape, F32), jax.ShapeDtypeStruct(n0.shape, F32),
                   jax.ShapeDtypeStruct(m0.shape, F32)],
        compiler_params=_cparams("parallel", "arbitrary"),
        name="mlstm",
    )(u, u, u, u, u, u, u, u, bias, cm0, n0, m0)


def _branch_kernel(yaf, yab, rr, rk, rv, rg, hbf, hbb, lg, ocf, ocb, gr, odf, odb, mo,
                   rw_ln, rw_rk, g_up, gla_ln, ml_ln, e64, e128, o_ref):
    y = yaf[...] + yab[...]
    y = y * lax.rsqrt(_dot(y * y, e64[...], HI) * (1.0 / RW_HEAD) + EPS) * rw_ln[...]
    bonus = _dot(rr[...] * rk[...] * rw_rk[...], e64[...], HI) * rv[...]
    g = _dot(_sigmoid(rg[...]).astype(BF16), g_up[...])
    o_ref[:, 0:W4] = ((y + bonus) * g).astype(o_ref.dtype)
    o_ref[:, W4:2 * W4] = ((hbf[...] + hbb[...]) * jax.nn.gelu(lg[...])).astype(o_ref.dtype)
    y = ocf[...] + ocb[...]
    y = y * lax.rsqrt(_dot(y * y, e128[...], HI) * (1.0 / GLA_DV) + EPS) * gla_ln[...]
    r = gr[...]
    o_ref[:, 2 * W4:3 * W4] = (y * (r * _sigmoid(r))).astype(o_ref.dtype)
    y = odf[...] + odb[...]
    y = y * lax.rsqrt(_dot(y * y, e128[...], HI) * (1.0 / ML_DH) + EPS) * ml_ln[...]
    o_ref[:, 3 * W4:4 * W4] = (y * _sigmoid(mo[...])).astype(o_ref.dtype)


def _branches(u_rw, u_lru, u_gla, u_ml, ya, hb, oc, od, rw_ln, rw_rk, g_up, gla_ln, ml_ln, e64, e128, tm=256):
    r = u_rw.shape[0]
    row = lambda w, c: pl.BlockSpec((tm, w), lambda i: (i, c))
    full = lambda shp: pl.BlockSpec(shp, lambda i: (0,) * len(shp))
    return pl.pallas_call(
        _branch_kernel,
        grid=(r // tm,),
        in_specs=[row(W4, 0), row(W4, 0), row(W4, 0), row(W4, 1), row(W4, 2), row(128, 14),
                  row(W4, 0), row(W4, 0), row(W4, 1),
                  row(W4, 0), row(W4, 0), row(W4, 2),
                  row(W4, 0), row(W4, 0), row(W4, 3),
                  full((1, W4)), full((1, W4)), full((128, W4)), full((1, W4)), full((1, W4)),
                  full((W4, W4)), full((W4, W4))],
        out_specs=pl.BlockSpec((tm, D), lambda i: (i, 0)),
        out_shape=jax.ShapeDtypeStruct((r, D), BF16),
        compiler_params=_cparams("parallel"),
        name="branches",
    )(ya[0], ya[1], u_rw, u_rw, u_rw, u_rw, hb[0], hb[1], u_lru, oc[0], oc[1], u_gla, od[0], od[1], u_ml,
      rw_ln, rw_rk, g_up, gla_ln, ml_ln, e64, e128)


def _block_ones(width, head):
    i = np.arange(width) // head
    return jnp.asarray((i[:, None] == i[None, :]).astype(np.float32))


def _block_diag(w):
    z, nb, c, _ = w.shape
    eye = jnp.eye(nb, dtype=w.dtype)
    return jnp.einsum("znij,nm->znimj", w, eye).reshape(z, nb * c, nb * c)


def _regroup_w_in(w):
    sizes = [512, 512, 512, 128, 128, 128, 512, 512, 256, 256, 512, 512, 32, 512, 512, 512, 512, 8, 8]
    offs = np.concatenate([[0], np.cumsum(sizes)])
    part = [w[:, offs[i]:offs[i + 1]] for i in range(len(sizes))]
    pad = lambda n: jnp.zeros((w.shape[0], n), w.dtype)
    rw = jnp.concatenate(part[0:6], axis=1)
    lru = jnp.concatenate(part[6:8], axis=1)
    gla = jnp.concatenate(part[8:13] + [pad(96)], axis=1)
    ml = jnp.concatenate(part[13:19] + [pad(112)], axis=1)
    return [t.astype(BF16) for t in (rw, lru, gla, ml)]


def _mixers(hx, n_b, lw, states, latent):
    r = hx.shape[0]
    length = r // n_b
    u_rw, u_lru, u_gla, u_ml = (_mm(hx, w) for w in lw["w_in"])
    s_rw, s_lru, s_gla, s_ml = states

    ya_f, ya_b, s_rw = _rwkv(u_rw, n_b, *lw["rw"], s_rw)
    if latent:
        rows = length // GRID_W
        x3 = u_lru.reshape(n_b, rows, GRID_W * LRU_COLS)
        hb_f, hb_b, s_lru = _lru_cols(x3, lw["lru"], s_lru, rows, stride=LRU_COLS // W4)
        hb_f = hb_f.reshape(r, W4)
        hb_b = hb_b.reshape(r, W4)
    else:
        x3 = u_lru.reshape(n_b, length, LRU_COLS)
        hb_f, hb_b, s_lru = _lru_cols(x3, lw["lru"], s_lru, length, stride=LRU_COLS // W4)
        hb_f = hb_f.reshape(r, W4)
        hb_b = hb_b.reshape(r, W4)
    oc_f, oc_b, s_gla = _gla(u_gla, n_b, *lw["gla"], s_gla)
    od_f, od_b, *s_ml = _mlstm(u_ml, n_b, lw["ml_bias"], *s_ml)
    ys = _branches(u_rw, u_lru, u_gla, u_ml, (ya_f, ya_b), (hb_f, hb_b), (oc_f, oc_b), (od_f, od_b), *lw["branch"])
    return ys, (s_rw, s_lru, s_gla, tuple(s_ml))


def _lru_cols(x3, lru_w, h0, t, stride):
    n_b = x3.shape[0]
    nblk = x3.shape[2] // (stride * W4)
    blk = lambda f: pl.BlockSpec((None, t, W4), f)
    cl = lambda v: jnp.clip(v, 0, nblk - 1)
    specs = [blk(lambda b, j: (b, 0, stride * cl(j - 1))), blk(lambda b, j: (b, 0, stride * j)),
             blk(lambda b, j: (b, 0, stride * cl(j + 1))),
             blk(lambda b, j: (b, 0, stride * cl(nblk - 2 - j))), blk(lambda b, j: (b, 0, stride * (nblk - 1 - j))),
             blk(lambda b, j: (b, 0, stride * cl(nblk - j)))]
    full = lambda shp: pl.BlockSpec(shp, lambda b, j: (0,) * len(shp))
    st_spec = pl.BlockSpec((None, 2, W4), lambda b, j: (b, 0, 0))
    out_sds = jax.ShapeDtypeStruct((n_b, t, nblk * W4), F32)
    return pl.pallas_call(
        functools.partial(_lru_kernel, t=t, nblk=nblk),
        grid=(n_b, nblk),
        in_specs=specs + [full((4, W4)), full((1, W4)), full((2, W4, W4)), full((2, 1, W4)),
                          full((2, W4, W4)), full((2, 1, W4)), full((2, 1, W4)), st_spec],
        out_specs=[blk(lambda b, j: (b, 0, j)), blk(lambda b, j: (b, 0, nblk - 1 - j)), st_spec],
        out_shape=[out_sds, out_sds, jax.ShapeDtypeStruct(h0.shape, F32)],
        scratch_shapes=[pltpu.VMEM((t + 16, W4), F32), pltpu.VMEM((2, t, W4), F32), pltpu.VMEM((2, t, W4), F32)],
        compiler_params=_cparams("parallel", "arbitrary"),
        name="rglru",
    )(x3, x3, x3, x3, x3, x3, *lru_w, h0)


def _zero_states(n_b):
    return (jnp.zeros((n_b, 2, RW_HEADS, RW_HEAD, RW_HEAD), F32),
            jnp.zeros((n_b, 2, W4), F32),
            jnp.zeros((n_b, 2, GLA_HEADS, GLA_DV, GLA_DK), F32),
            (jnp.zeros((n_b, 2, ML_HEADS, ML_DH, ML_DH), F32),
             jnp.zeros((n_b, 2, ML_HEADS, 1, ML_DH), F32),
             jnp.full((n_b, 2, ML_HEADS, 1, 1), NEG, F32)))


def kernel(x, c, ctx, c_ctx, ada_w, ada_b, norm_mix_w, w_in, rw_w_up, rw_w0, rw_a_up, rw_a0, rw_g_up, rw_k_k, rw_k_a, rw_r_k, rw_ln_w, lru_conv_w, lru_conv_b, lru_w_a, lru_b_a, lru_w_x, lru_b_x, lru_lambda, gla_w_up, gla_w0, gla_ln_w, ml_i_b, ml_f_b, ml_ln_w, br_w, gate_w, gate_b, out_w, norm_ffn_w, ffn_w1, ffn_w2, final_norm_w):
    n_b, seq, _ = x.shape
    n_c, ctx_len, _ = ctx.shape
    xs = x.reshape(n_b * seq, D)
    cs = ctx.reshape(n_c * ctx_len, D)
    cc = jnp.zeros((8, D), F32).at[0:n_b].set(c).at[n_b].set(c_ctx)
    e64 = _block_ones(W4, RW_HEAD)
    e128 = _block_ones(W4, GLA_DV)
    row = lambda v: v.reshape(1, -1)

    for l in range(DEPTH):
        mod = _ada(cc, ada_w[l], ada_b[l]).reshape(8, 6, D)
        mod = jnp.pad(mod, ((0, 0), (0, 2), (0, 0)))
        mod_x = mod[0:n_b]
        mod_c = mod[n_b:n_b + 1]
        lw = {
            "w_in": _regroup_w_in(w_in[l]),
            "rw": (rw_w_up[l], rw_w0[l][:, None, :], rw_a_up[l], rw_a0[l][:, None, :],
                   row(rw_k_k[l]), row(rw_k_a[l]), e64),
            "lru": (lru_conv_w[l], row(lru_conv_b[l]), _block_diag(lru_w_a[l]), lru_b_a[l][:, None, :],
                    _block_diag(lru_w_x[l]), lru_b_x[l][:, None, :], lru_lambda[l][:, None, :]),
            "gla": (gla_w_up[l], gla_w0[l][:, None, :]),
            "ml_bias": jnp.zeros((1, 128), F32).at[0, 0:8].set(ml_i_b[l].reshape(-1)).at[0, 8:16].set(ml_f_b[l].reshape(-1)),
            "branch": (row(rw_ln_w[l]), row(rw_r_k[l]), rw_g_up[l].astype(BF16), row(gla_ln_w[l]), row(ml_ln_w[l]),
                       e64, e128),
        }
        gw = gate_w[l].astype(BF16)
        gb = gate_b[l][:, None, :]
        bw = br_w[l].astype(BF16)
        ow = out_w[l].astype(BF16)
        w1 = ffn_w1[l].astype(BF16)
        w2 = ffn_w2[l].astype(BF16)

        hc = _norm(cs, norm_mix_w[l], mod_c, (0, 1), BF16)
        ys_c, st_c = _mixers(hc, n_c, lw, _zero_states(n_c), latent=False)
        hx = _norm(xs, norm_mix_w[l], mod_x, (0, 1), BF16)
        ys_x, _ = _mixers(hx, n_b, lw, st_c, latent=True)
        xs = _proj_residual(_gated_sum(hx, ys_x, gw, gb, bw), ow, xs, mod_x, 2)
        xs = _ffn(_norm(xs, norm_ffn_w[l], mod_x, (3, 4), BF16), w1, w2, xs, mod_x, 5)
        if l < DEPTH - 1:
            cs = _proj_residual(_gated_sum(hc, ys_c, gw, gb, bw), ow, cs, mod_c, 2)
            cs = _ffn(_norm(cs, norm_ffn_w[l], mod_c, (3, 4), BF16), w1, w2, cs, mod_c, 5)
    out = _norm(xs, final_norm_w, mod_x, None, F32)
    return out.reshape(n_b, seq, D)
```

```python
import functools

import numpy as np
import jax
import jax.numpy as jnp
from jax import lax
from jax.experimental import pallas as pl
from jax.experimental.pallas import tpu as pltpu

F32 = jnp.float32
BF16 = jnp.bfloat16
HI = lax.Precision.HIGHEST

D = 2048
DEPTH = 2
GRID_W = 64
EPS = 1e-6
NEG = -1e30
CH = 64
D_FF = 4 * D
W4 = D // 4

RW_HEAD = 64
RW_HEADS = W4 // RW_HEAD
RW_DECAY_SCALE = 0.6065306597
LRU_BLOCKS = 8
LRU_C = 8.0
GLA_HEADS = 4
GLA_DK = 64
GLA_DV = 128
GLA_RANK = 16
GLA_TAU = 16.0
ML_HEADS = 4
ML_DH = 128

RW_COLS = 3 * W4 + 3 * 128
LRU_COLS = 2 * W4
GLA_COLS = 256 + 256 + 512 + 512 + 128
ML_COLS = 4 * W4 + 128

VMEM_LIMIT = 56 * 1024 * 1024


def _cparams(*sem):
    return pltpu.CompilerParams(dimension_semantics=sem, vmem_limit_bytes=VMEM_LIMIT)


def _dot(a, b, prec=None):
    return jnp.dot(a, b, precision=prec, preferred_element_type=F32)


def _dot_nt(a, b, prec=None):
    return lax.dot_general(a, b, (((1,), (1,)), ((), ())), precision=prec, preferred_element_type=F32)


def _dot_tn(a, b, prec=None):
    return lax.dot_general(a, b, (((0,), (0,)), ((), ())), precision=prec, preferred_element_type=F32)


def _b(x):
    return x.astype(BF16)


def _bdot(a, b):
    return jnp.dot(_b(a), _b(b), preferred_element_type=F32)


def _bdot_nt(a, b):
    return _dot_nt(_b(a), _b(b))


def _bdot_tn(a, b):
    return _dot_tn(_b(a), _b(b))


def _split_bf16(x, parts):
    out = []
    for i in range(parts):
        hi = _b(x)
        out.append(hi)
        if i < parts - 1:
            x = x - hi.astype(F32)
    return out


def _cumdot(mask, x):
    m = jnp.where(mask, 1.0, 0.0).astype(BF16)
    parts = _split_bf16(x, 3)
    acc = _dot(m, parts[0])
    for p in parts[1:]:
        acc = acc + _dot(m, p)
    return acc


def _headsum(x, e):
    hi, lo = _split_bf16(x, 2)
    return _dot(hi, e) + _dot(lo, e)


def _stack_heads(x):
    lo = lax.broadcasted_iota(jnp.int32, x.shape, 1) < 64
    return jnp.concatenate([jnp.where(lo, x, 0.0), jnp.where(lo, 0.0, x)], axis=0)


def _pair_scan_mask(inclusive, reverse):
    r = lax.broadcasted_iota(jnp.int32, (2 * CH, 2 * CH), 0) & (CH - 1)
    c = lax.broadcasted_iota(jnp.int32, (2 * CH, 2 * CH), 1) & (CH - 1)
    if reverse:
        return (r <= c) if inclusive else (r < c)
    return (r >= c) if inclusive else (r > c)


def _for_chunks(nchunk, body):
    for ci in range(nchunk):
        body(slice(ci * CH, (ci + 1) * CH), slice((nchunk - 1 - ci) * CH, (nchunk - ci) * CH))


def _sigmoid(x):
    return 1.0 / (1.0 + jnp.exp(-x))


def _log_sigmoid(x):
    return jnp.minimum(x, 0.0) - jnp.log1p(jnp.exp(-jnp.abs(x)))


def _softplus(x):
    return jnp.maximum(x, 0.0) + jnp.log1p(jnp.exp(-jnp.abs(x)))


def _scan_mask(n, inclusive, reverse):
    r = lax.broadcasted_iota(jnp.int32, (n, n), 0)
    c = lax.broadcasted_iota(jnp.int32, (n, n), 1)
    if reverse:
        return (r <= c) if inclusive else (r < c)
    return (r >= c) if inclusive else (r > c)


def _ada_kernel(s_ref, w_ref, b_ref, o_ref):
    s = s_ref[...]
    s = s * _sigmoid(s)
    o_ref[...] = _dot(s, w_ref[...], HI) + b_ref[...]


def _ada(cc, w, b):
    tn = 1024
    return pl.pallas_call(
        _ada_kernel,
        grid=(6 * D // tn,),
        in_specs=[pl.BlockSpec((8, D), lambda j: (0, 0)),
                  pl.BlockSpec((D, tn), lambda j: (0, j)),
                  pl.BlockSpec((1, tn), lambda j: (0, j))],
        out_specs=pl.BlockSpec((8, tn), lambda j: (0, j)),
        out_shape=jax.ShapeDtypeStruct((8, 6 * D), F32),
        compiler_params=_cparams("parallel"),
        name="ada",
    )(cc, w, b.reshape(1, 6 * D))


def _norm_kernel(x_ref, w_ref, mod_ref, o_ref, *, rows):
    x = x_ref[...]
    y = x * lax.rsqrt(jnp.mean(x * x, axis=-1, keepdims=True) + EPS)
    y = y * w_ref[...]
    if rows is not None:
        shift, scale = rows
        m = mod_ref[...]
        y = y * (1.0 + m[scale:scale + 1]) + m[shift:shift + 1]
    o_ref[...] = y.astype(o_ref.dtype)


def _norm(x, w, mods, rows, out_dtype, tm=256):
    r = x.shape[0]
    g = mods.shape[0]
    per = r // g // tm
    return pl.pallas_call(
        functools.partial(_norm_kernel, rows=rows),
        grid=(r // tm,),
        in_specs=[pl.BlockSpec((tm, D), lambda i: (i, 0)),
                  pl.BlockSpec((1, D), lambda i: (0, 0)),
                  pl.BlockSpec((None, 8, D), lambda i: (i // per, 0, 0))],
        out_specs=pl.BlockSpec((tm, D), lambda i: (i, 0)),
        out_shape=jax.ShapeDtypeStruct((r, D), out_dtype),
        compiler_params=_cparams("parallel"),
        name="norm",
    )(x, w.reshape(1, D), mods)


def _mm_kernel(a_ref, w_ref, o_ref):
    o_ref[...] = _dot(a_ref[...], w_ref[...])


def _mm(a, w, tm=512):
    r, k = a.shape
    n = w.shape[1]
    tm = min(tm, r)
    return pl.pallas_call(
        _mm_kernel,
        grid=(r // tm,),
        in_specs=[pl.BlockSpec((tm, k), lambda i: (i, 0)),
                  pl.BlockSpec((k, n), lambda i: (0, 0))],
        out_specs=pl.BlockSpec((tm, n), lambda i: (i, 0)),
        out_shape=jax.ShapeDtypeStruct((r, n), F32),
        compiler_params=_cparams("parallel"),
        name="in_proj",
    )(a, w)


def _gate_kernel(hx_ref, ys_ref, gw_ref, gb_ref, bw_ref, o_ref):
    hx = hx_ref[...]
    acc = None
    for i in range(4):
        g = _dot(hx, gw_ref[i]) + gb_ref[i]
        p = _dot(ys_ref[:, i * W4:(i + 1) * W4], bw_ref[i])
        t = _sigmoid(g) * p
        acc = t if acc is None else acc + t
    o_ref[...] = acc.astype(o_ref.dtype)


def _gated_sum(hx, ys, gw, gb, bw, tm=512, tn=512):
    r = hx.shape[0]
    tm = min(tm, r)
    return pl.pallas_call(
        _gate_kernel,
        grid=(D // tn, r // tm),
        in_specs=[pl.BlockSpec((tm, D), lambda j, i: (i, 0)),
                  pl.BlockSpec((tm, D), lambda j, i: (i, 0)),
                  pl.BlockSpec((4, D, tn), lambda j, i: (0, 0, j)),
                  pl.BlockSpec((4, 1, tn), lambda j, i: (0, 0, j)),
                  pl.BlockSpec((4, W4, tn), lambda j, i: (0, 0, j))],
        out_specs=pl.BlockSpec((tm, tn), lambda j, i: (i, j)),
        out_shape=jax.ShapeDtypeStruct((r, D), BF16),
        compiler_params=_cparams("parallel", "parallel"),
        name="gated_sum",
    )(hx, ys, gw, gb, bw)


def _proj_res_kernel(a_ref, w_ref, x_ref, mod_ref, o_ref, *, gate_row):
    y = _dot(a_ref[...], w_ref[...])
    m = mod_ref[...]
    o_ref[...] = x_ref[...] + m[gate_row:gate_row + 1] * y


def _proj_residual(a, w, x, mods, gate_row, tm=512):
    r = a.shape[0]
    tm = min(tm, r)
    per = r // mods.shape[0] // tm
    return pl.pallas_call(
        functools.partial(_proj_res_kernel, gate_row=gate_row),
        grid=(r // tm,),
        in_specs=[pl.BlockSpec((tm, D), lambda i: (i, 0)),
                  pl.BlockSpec((D, D), lambda i: (0, 0)),
                  pl.BlockSpec((tm, D), lambda i: (i, 0)),
                  pl.BlockSpec((None, 8, D), lambda i: (i // per, 0, 0))],
        out_specs=pl.BlockSpec((tm, D), lambda i: (i, 0)),
        out_shape=jax.ShapeDtypeStruct((r, D), F32),
        compiler_params=_cparams("parallel"),
        name="out_proj",
    )(a, w, x, mods)


def _ffn_kernel(h_ref, w1_ref, w2_ref, x_ref, mod_ref, o_ref, acc_ref, *, gate_row):
    f = pl.program_id(1)

    @pl.when(f == 0)
    def _():
        acc_ref[...] = jnp.zeros_like(acc_ref)

    a = jnp.maximum(_dot(h_ref[...], w1_ref[...]), 0.0)
    acc_ref[...] += _dot((a * a).astype(BF16), w2_ref[...])

    @pl.when(f == pl.num_programs(1) - 1)
    def _():
        m = mod_ref[...]
        o_ref[...] = x_ref[...] + m[gate_row:gate_row + 1] * acc_ref[...]


def _ffn(hn, w1, w2, x, mods, gate_row, tm=512, tf=512):
    r = hn.shape[0]
    tm = min(tm, r)
    per = r // mods.shape[0] // tm
    return pl.pallas_call(
        functools.partial(_ffn_kernel, gate_row=gate_row),
        grid=(r // tm, D_FF // tf),
        in_specs=[pl.BlockSpec((tm, D), lambda i, f: (i, 0)),
                  pl.BlockSpec((D, tf), lambda i, f: (0, f)),
                  pl.BlockSpec((tf, D), lambda i, f: (f, 0)),
                  pl.BlockSpec((tm, D), lambda i, f: (i, 0)),
                  pl.BlockSpec((None, 8, D), lambda i, f: (i // per, 0, 0))],
        out_specs=pl.BlockSpec((tm, D), lambda i, f: (i, 0)),
        out_shape=jax.ShapeDtypeStruct((r, D), F32),
        scratch_shapes=[pltpu.VMEM((tm, D), F32)],
        compiler_params=_cparams("parallel", "arbitrary"),
        name="ffn",
    )(hn, w1, w2, x, mods)


def _rwkv_chunk(z, r, k, v, wd, ad, w_up, w0, a_up, a0, k_k, k_a, s_ref):
    rev = z == 1
    last = 0 if rev else CH - 1
    lo = lax.broadcasted_iota(jnp.int32, (CH, 128), 1) < 64

    wdz = wd[:, 64 * z:64 * z + 64]
    adz = ad[:, 64 * z:64 * z + 64]
    lw = -RW_DECAY_SCALE * _sigmoid(_dot(jnp.tanh(wdz), w_up, HI) + w0)
    a = _sigmoid(_dot(adz, a_up, HI) + a0)
    kk = k * k_k
    kd = k * (1.0 + (a - 1.0) * k_a)

    lc_incl = _cumdot(_scan_mask(CH, True, rev), lw)
    p_incl = jnp.exp(lc_incl)
    p_inv = jnp.exp(-lc_incl)
    p_excl = jnp.exp(lc_incl - lw)
    p_end = p_incl[last:last + 1]

    chains = []
    for p in range(RW_HEADS // 2):
        sl = slice(128 * p, 128 * (p + 1))
        kkp = kk[:, sl]
        sq = kkp * kkp
        nrm = jnp.where(lo, jnp.sum(jnp.where(lo, sq, 0.0), axis=-1, keepdims=True),
                        jnp.sum(jnp.where(lo, 0.0, sq), axis=-1, keepdims=True))
        kkp = kkp / jnp.maximum(jnp.sqrt(nrm), 1e-12)
        at = -kkp * p_excl[:, sl]
        bt = kkp * a[:, sl] * p_inv[:, sl]
        kt = kd[:, sl] * p_inv[:, sl]
        rt = r[:, sl] * p_incl[:, sl]
        chains.append(dict(
            z=z, p=p, p_end=p_end[:, sl],
            ar=_b(jnp.concatenate([_stack_heads(at), _stack_heads(rt)], axis=0)),
            bk=_b(jnp.concatenate([_stack_heads(bt), _stack_heads(kt)], axis=0)),
            vs=_b(_stack_heads(v[:, sl]))))
    return chains


def _rwkv_solve(chains, s_ref):
    n2 = 2 * CH
    masks = {z: (_pair_scan_mask(True, z == 1), _pair_scan_mask(False, z == 1)) for z in (0, 1)}
    s0 = [s_ref[c["z"], c["p"]] for c in chains]
    g = [_dot_nt(c["ar"], c["bk"]) for c in chains]
    s0t = [_dot_nt(c["ar"], _b(s)) for c, s in zip(chains, s0)]
    lp = [_b(jnp.where(masks[c["z"]][1], gi[:n2, :n2], 0.0)) for c, gi in zip(chains, g)]
    u = [st[:n2] + _dot(_b(jnp.where(masks[c["z"]][1], gi[:n2, n2:], 0.0)), c["vs"])
         for c, gi, st in zip(chains, g, s0t)]
    for j in range(6):
        u = [ui + _dot(li, _b(ui)) for ui, li in zip(u, lp)]
        if j < 5:
            lp = [_b(_dot(li, li)) for li in lp]
    ys = {0: [], 1: []}
    for c, gi, st, ui, s in zip(chains, g, s0t, u, s0):
        m_incl = masks[c["z"]][0]
        uv = jnp.concatenate([_b(ui), c["vs"]], axis=0)
        rbk = _b(jnp.concatenate([jnp.where(m_incl, gi[n2:, :n2], 0.0), jnp.where(m_incl, gi[n2:, n2:], 0.0)], axis=1))
        y = st[n2:] + _dot(rbk, uv)
        ys[c["z"]].append(y[:CH] + y[CH:])
        s_ref[c["z"], c["p"]] = (s + _dot_tn(uv, c["bk"])) * c["p_end"]
    return [jnp.concatenate(ys[z], axis=1) for z in (0, 1)]


def _rwkv_kernel(rf, kf, vf, wf, af, rb, kb, vb, wb, ab, w_up, w0, a_up, a0, k_k, k_a, s0_ref,
                 yf_ref, yb_ref, s_ref, *, nchunk):
    @pl.when(pl.program_id(1) == 0)
    def _():
        s_ref[...] = s0_ref[...]

    def body(rows_f, rows_b):
        chains = []
        for z, (rr, kr, vr, wr, ar, rows) in enumerate(((rf, kf, vf, wf, af, rows_f), (rb, kb, vb, wb, ab, rows_b))):
            chains += _rwkv_chunk(z, rr[rows, :], kr[rows, :], vr[rows, :], wr[rows, :], ar[rows, :],
                                  w_up[z], w0[z], a_up[z], a0[z], k_k[...], k_a[...], s_ref)
        y_f, y_b = _rwkv_solve(chains, s_ref)
        yf_ref[rows_f, :] = y_f
        yb_ref[rows_b, :] = y_b

    _for_chunks(nchunk, body)


def _rwkv(u, n_b, w_up, w0, a_up, a0, k_k, k_a, s0, tb=256):
    rtot = u.shape[0]
    nblk = rtot // n_b // tb

    def fwd(col):
        return lambda b, j: (b * nblk + j, col)

    def bwd(col):
        return lambda b, j: (b * nblk + nblk - 1 - j, col)

    def seq_specs(imap):
        return [pl.BlockSpec((tb, W4), imap(0)), pl.BlockSpec((tb, W4), imap(1)), pl.BlockSpec((tb, W4), imap(2)),
                pl.BlockSpec((tb, 128), imap(12)), pl.BlockSpec((tb, 128), imap(13))]

    full = lambda shp: pl.BlockSpec(shp, lambda b, j: (0,) * len(shp))
    st_spec = pl.BlockSpec((None, 2, RW_HEADS // 2, 128, 128), lambda b, j: (b, 0, 0, 0, 0))
    return pl.pallas_call(
        functools.partial(_rwkv_kernel, nchunk=tb // CH),
        grid=(n_b, nblk),
        in_specs=seq_specs(fwd) + seq_specs(bwd) + [
            full((2, 64, W4)), full((2, 1, W4)), full((2, 64, W4)), full((2, 1, W4)),
            full((1, W4)), full((1, W4)), st_spec],
        out_specs=[pl.BlockSpec((tb, W4), fwd(0)), pl.BlockSpec((tb, W4), bwd(0)), st_spec],
        out_shape=[jax.ShapeDtypeStruct((rtot, W4), F32), jax.ShapeDtypeStruct((rtot, W4), F32),
                   jax.ShapeDtypeStruct(s0.shape, F32)],
        compiler_params=_cparams("parallel", "arbitrary"),
        name="rwkv7",
    )(u, u, u, u, u, u, u, u, u, u, w_up, w0, a_up, a0, k_k, k_a, s0)


def _lru_kernel(pf, cf, nf, pb, cb, nb_, conv_w, conv_b, wa, ba, wx, bx, lam, h0_ref,
                hf_ref, hb_ref, h_ref, ext, a_s, b_s, *, t, nblk):
    j = pl.program_id(1)

    @pl.when(j == 0)
    def _():
        h_ref[...] = h0_ref[...]

    for z, (pr, cr, nr, blk) in enumerate(((pf, cf, nf, j), (pb, cb, nb_, nblk - 1 - j))):
        ext[6:8, :] = jnp.where(blk > 0, pr[t - 2:t, :], 0.0)
        ext[8:8 + t, :] = cr[...]
        ext[8 + t:9 + t, :] = jnp.where(blk < nblk - 1, nr[0:1, :], 0.0)
        xc = conv_b[...] + ext[6:6 + t, :] * conv_w[0:1, :]
        for tap in range(1, 4):
            xc = xc + ext[6 + tap:6 + tap + t, :] * conv_w[tap:tap + 1, :]
        xcb = _b(xc)
        rg = _sigmoid(_dot(xcb, wa[z]) + ba[z])
        ig = _sigmoid(_dot(xcb, wx[z]) + bx[z])
        log_a = -LRU_C * rg * _softplus(-lam[z])
        a = jnp.exp(log_a)
        a_s[z] = a
        b_s[z] = jnp.sqrt(-jnp.tanh(log_a) * (a * a + 1.0)) * ig * xc

    def step(i, carry):
        h_f, h_b = carry
        rf = pl.ds(i, 1)
        rb = pl.ds(t - 1 - i, 1)
        h_f = a_s[0, rf, :] * h_f + b_s[0, rf, :]
        h_b = a_s[1, rb, :] * h_b + b_s[1, rb, :]
        hf_ref[rf, :] = h_f
        hb_ref[rb, :] = h_b
        return h_f, h_b

    h_f, h_b = lax.fori_loop(0, t, step, (h_ref[0:1, :], h_ref[1:2, :]), unroll=8)
    h_ref[0:1, :] = h_f
    h_ref[1:2, :] = h_b


def _gla_chunk(z, q, k, v, wd, w_up, w0, s_ref):
    rev = z == 1
    last = 0 if rev else CH - 1
    pm_incl = _pair_scan_mask(True, rev)
    la = _log_sigmoid(_dot(wd[:, GLA_RANK * z:GLA_RANK * (z + 1)], w_up, HI) + w0) * (1.0 / GLA_TAU)
    bcum = _cumdot(_scan_mask(CH, True, rev), la)
    b_end = bcum[last:last + 1]
    qe = q * (GLA_DK ** -0.5) * jnp.exp(bcum)
    ke = k * jnp.exp(-bcum)
    kd = k * jnp.exp(b_end - bcum)
    e_end = jnp.exp(b_end)
    chains = []
    for p in range(GLA_HEADS // 2):
        sl = slice(128 * p, 128 * (p + 1))
        chains.append(dict(
            z=z, p=p, mask=pm_incl, e_end=e_end[:, sl],
            qs=_b(_stack_heads(qe[:, sl])),
            ks=_b(_stack_heads(ke[:, sl])),
            kds=_b(_stack_heads(kd[:, sl])),
            vs=_b(jnp.concatenate([v[:, 2 * GLA_DV * p:2 * GLA_DV * p + GLA_DV],
                                   v[:, 2 * GLA_DV * p + GLA_DV:2 * GLA_DV * (p + 1)]], axis=0))))
    return chains


def _gla_solve(chains, s_ref):
    st = [s_ref[c["z"], c["p"]] for c in chains]
    att = [_b(jnp.where(c["mask"], _dot_nt(c["qs"], c["ks"]), 0.0)) for c in chains]
    inter = [_dot_nt(c["qs"], _b(s)) for c, s in zip(chains, st)]
    outs = {0: [], 1: []}
    for c, a, i, s in zip(chains, att, inter, st):
        o = _dot(a, c["vs"]) + i
        outs[c["z"]] += [o[:CH], o[CH:]]
        s_ref[c["z"], c["p"]] = s * c["e_end"] + _dot_tn(c["vs"], c["kds"])
    return [jnp.concatenate(outs[z], axis=1) for z in (0, 1)]


def _gla_kernel(qf, kf, vf, wf, qb, kb, vb, wb, w_up, w0, s0_ref, of_ref, ob_ref, s_ref, *, nchunk):
    @pl.when(pl.program_id(1) == 0)
    def _():
        s_ref[...] = s0_ref[...]

    def body(rows_f, rows_b):
        chains = []
        for z, (qr, kr, vr, wr, rows) in enumerate(((qf, kf, vf, wf, rows_f), (qb, kb, vb, wb, rows_b))):
            chains += _gla_chunk(z, qr[rows, :], kr[rows, :], vr[rows, :], wr[rows, :], w_up[z], w0[z], s_ref)
        o_f, o_b = _gla_solve(chains, s_ref)
        of_ref[rows_f, :] = o_f
        ob_ref[rows_b, :] = o_b

    _for_chunks(nchunk, body)


def _gla(u, n_b, w_up, w0, s0, tb=256):
    rtot = u.shape[0]
    nblk = rtot // n_b // tb

    def fwd(col):
        return lambda b, j: (b * nblk + j, col)

    def bwd(col):
        return lambda b, j: (b * nblk + nblk - 1 - j, col)

    def seq_specs(imap):
        return [pl.BlockSpec((tb, 256), imap(0)), pl.BlockSpec((tb, 256), imap(1)), pl.BlockSpec((tb, 512), imap(1)),
                pl.BlockSpec((tb, 128), imap(12))]

    full = lambda shp: pl.BlockSpec(shp, lambda b, j: (0,) * len(shp))
    st_spec = pl.BlockSpec((None, 2, GLA_HEADS // 2, GLA_DV, 2 * GLA_DK), lambda b, j: (b, 0, 0, 0, 0))
    return pl.pallas_call(
        functools.partial(_gla_kernel, nchunk=tb // CH),
        grid=(n_b, nblk),
        in_specs=seq_specs(fwd) + seq_specs(bwd) + [full((2, GLA_RANK, 256)), full((2, 1, 256)), st_spec],
        out_specs=[pl.BlockSpec((tb, W4), fwd(0)), pl.BlockSpec((tb, W4), bwd(0)), st_spec],
        out_shape=[jax.ShapeDtypeStruct((rtot, W4), F32), jax.ShapeDtypeStruct((rtot, W4), F32),
                   jax.ShapeDtypeStruct(s0.shape, F32)],
        compiler_params=_cparams("parallel", "arbitrary"),
        name="gla",
    )(u, u, u, u, u, u, u, u, w_up, w0, s0)


def _mlstm_chunk(z, q, k, v, gates, bias):
    rev = z == 1
    m_incl = _scan_mask(CH, True, rev)
    last = 0 if rev else CH - 1
    pre = gates + bias
    lf = _log_sigmoid(pre)
    fcum = _cumdot(m_incl, lf)
    pre_t = pre.T
    fcum_t = fcum.T
    k = k * (ML_DH ** -0.5)
    chains = []
    for h in range(ML_HEADS):
        ci = ML_HEADS * z + h
        cf = 2 * ML_HEADS + ML_HEADS * z + h
        sl = slice(ML_DH * h, ML_DH * (h + 1))
        fc = fcum[:, cf:cf + 1]
        ic = pre[:, ci:ci + 1]
        fr = fcum_t[cf:cf + 1, :]
        ir = pre_t[ci:ci + 1, :]
        chains.append(dict(z=z, h=h, q=q[:, sl], k=k[:, sl], v=v[:, sl], qb=_b(q[:, sl]), kb=_b(k[:, sl]),
                           fc=fc, ic=ic, f_end=fc[last:last + 1],
                           dlog=jnp.where(m_incl, fc + (ir - fr), NEG)))
    return chains


def _mlstm_solve(chains, cm_ref, n_ref, m_ref):
    cm = [cm_ref[c["z"], c["h"]] for c in chains]
    n = [n_ref[c["z"], c["h"]] for c in chains]
    m = [m_ref[c["z"], c["h"]] for c in chains]
    qk = [_dot_nt(c["qb"], c["kb"]) for c in chains]
    qc = [_dot_nt(c["qb"], _b(s)) for c, s in zip(chains, cm)]
    inter = [c["fc"] + mi for c, mi in zip(chains, m)]
    mt = [jnp.maximum(i, jnp.max(c["dlog"], axis=-1, keepdims=True)) for c, i in zip(chains, inter)]
    w = [jnp.exp(c["dlog"] - t) * s for c, t, s in zip(chains, mt, qk)]
    ei = [jnp.exp(i - t) for i, t in zip(inter, mt)]
    wv = [_dot(_b(wi), _b(c["v"])) for c, wi in zip(chains, w)]
    outs = {0: [], 1: []}
    for c, wi, e, t, a, b, cmi, ni, mi in zip(chains, w, ei, mt, wv, qc, cm, n, m):
        num = a + e * b
        den = jnp.sum(wi, axis=-1, keepdims=True) + e * jnp.sum(c["q"] * ni, axis=-1, keepdims=True)
        outs[c["z"]].append(num / jnp.maximum(jnp.abs(den), jnp.exp(-t)))
        g = c["f_end"] - c["fc"] + c["ic"]
        m_new = jnp.maximum(c["f_end"] + mi, jnp.max(g, axis=0, keepdims=True))
        dec = jnp.exp(c["f_end"] + mi - m_new)
        wg = jnp.exp(g - m_new)
        cm_ref[c["z"], c["h"]] = dec * cmi + _dot_tn(_b(c["v"] * wg), c["kb"])
        n_ref[c["z"], c["h"]] = dec * ni + jnp.sum(c["k"] * wg, axis=0, keepdims=True)
        m_ref[c["z"], c["h"]] = m_new
    return [jnp.concatenate(outs[z], axis=1) for z in (0, 1)]


def _mlstm_kernel(qf, kf, vf, gf, qb, kb, vb, gb, bias, cm0, n0, m0, of_ref, ob_ref, cm_ref, n_ref, m_ref, *, nchunk):
    @pl.when(pl.program_id(1) == 0)
    def _():
        cm_ref[...] = cm0[...]
        n_ref[...] = n0[...]
        m_ref[...] = m0[...]

    def body(rows_f, rows_b):
        chains = []
        for z, (qr, kr, vr, gr, rows) in enumerate(((qf, kf, vf, gf, rows_f), (qb, kb, vb, gb, rows_b))):
            chains += _mlstm_chunk(z, qr[rows, :], kr[rows, :], vr[rows, :], gr[rows, :], bias[...])
        o_f, o_b = _mlstm_solve(chains, cm_ref, n_ref, m_ref)
        of_ref[rows_f, :] = o_f
        ob_ref[rows_b, :] = o_b

    _for_chunks(nchunk, body)


def _mlstm(u, n_b, bias, cm0, n0, m0, tb=256):
    rtot = u.shape[0]
    nblk = rtot // n_b // tb

    def fwd(col):
        return lambda b, j: (b * nblk + j, col)

    def bwd(col):
        return lambda b, j: (b * nblk + nblk - 1 - j, col)

    def seq_specs(imap):
        return [pl.BlockSpec((tb, W4), imap(0)), pl.BlockSpec((tb, W4), imap(1)), pl.BlockSpec((tb, W4), imap(2)),
                pl.BlockSpec((tb, 128), imap(16))]

    full = lambda shp: pl.BlockSpec(shp, lambda b, j: (0,) * len(shp))
    cm_spec = pl.BlockSpec((None, 2, ML_HEADS, ML_DH, ML_DH), lambda b, j: (b, 0, 0, 0, 0))
    n_spec = pl.BlockSpec((None, 2, ML_HEADS, 1, ML_DH), lambda b, j: (b, 0, 0, 0, 0))
    m_spec = pl.BlockSpec((None, 2, ML_HEADS, 1, 1), lambda b, j: (b, 0, 0, 0, 0))
    return pl.pallas_call(
        functools.partial(_mlstm_kernel, nchunk=tb // CH),
        grid=(n_b, nblk),
        in_specs=seq_specs(fwd) + seq_specs(bwd) + [full((1, 128)), cm_spec, n_spec, m_spec],
        out_specs=[pl.BlockSpec((tb, W4), fwd(0)), pl.BlockSpec((tb, W4), bwd(0)), cm_spec, n_spec, m_spec],
        out_shape=[jax.ShapeDtypeStruct((rtot, W4), F32), jax.ShapeDtypeStruct((rtot, W4), F32),
                   jax.ShapeDtypeStruct(cm0.shape, F32), jax.ShapeDtypeStruct(n0.shape, F32),
                   jax.ShapeDtypeStruct(m0.shape, F32)],
        compiler_params=_cparams("parallel", "arbitrary"),
        name="mlstm",
    )(u, u, u, u, u, u, u, u, bias, cm0, n0, m0)


def _branch_kernel(yaf, yab, rr, rk, rv, rg, hbf, hbb, lg, ocf, ocb, gr, odf, odb, mo,
                   rw_ln, rw_rk, g_up, gla_ln, ml_ln, e64, e128, o_ref):
    y = yaf[...] + yab[...]
    y = y * lax.rsqrt(_headsum(y * y, e64[...]) * (1.0 / RW_HEAD) + EPS) * rw_ln[...]
    bonus = _headsum(rr[...] * rk[...] * rw_rk[...], e64[...]) * rv[...]
    g = _dot(_sigmoid(rg[...]).astype(BF16), g_up[...])
    o_ref[:, 0:W4] = ((y + bonus) * g).astype(o_ref.dtype)
    o_ref[:, W4:2 * W4] = ((hbf[...] + hbb[...]) * jax.nn.gelu(lg[...])).astype(o_ref.dtype)
    y = ocf[...] + ocb[...]
    y = y * lax.rsqrt(_headsum(y * y, e128[...]) * (1.0 / GLA_DV) + EPS) * gla_ln[...]
    r = gr[...]
    o_ref[:, 2 * W4:3 * W4] = (y * (r * _sigmoid(r))).astype(o_ref.dtype)
    y = odf[...] + odb[...]
    y = y * lax.rsqrt(_headsum(y * y, e128[...]) * (1.0 / ML_DH) + EPS) * ml_ln[...]
    o_ref[:, 3 * W4:4 * W4] = (y * _sigmoid(mo[...])).astype(o_ref.dtype)


def _branches(u_rw, u_lru, u_gla, u_ml, ya, hb, oc, od, rw_ln, rw_rk, g_up, gla_ln, ml_ln, e64, e128, tm=256):
    r = u_rw.shape[0]
    row = lambda w, c: pl.BlockSpec((tm, w), lambda i: (i, c))
    full = lambda shp: pl.BlockSpec(shp, lambda i: (0,) * len(shp))
    return pl.pallas_call(
        _branch_kernel,
        grid=(r // tm,),
        in_specs=[row(W4, 0), row(W4, 0), row(W4, 0), row(W4, 1), row(W4, 2), row(128, 14),
                  row(W4, 0), row(W4, 0), row(W4, 1),
                  row(W4, 0), row(W4, 0), row(W4, 2),
                  row(W4, 0), row(W4, 0), row(W4, 3),
                  full((1, W4)), full((1, W4)), full((128, W4)), full((1, W4)), full((1, W4)),
                  full((W4, W4)), full((W4, W4))],
        out_specs=pl.BlockSpec((tm, D), lambda i: (i, 0)),
        out_shape=jax.ShapeDtypeStruct((r, D), BF16),
        compiler_params=_cparams("parallel"),
        name="branches",
    )(ya[0], ya[1], u_rw, u_rw, u_rw, u_rw, hb[0], hb[1], u_lru, oc[0], oc[1], u_gla, od[0], od[1], u_ml,
      rw_ln, rw_rk, g_up, gla_ln, ml_ln, e64, e128)


def _block_ones(width, head):
    i = np.arange(width) // head
    return jnp.asarray((i[:, None] == i[None, :]).astype(np.float32)).astype(BF16)


def _block_diag(w):
    z, nb, c, _ = w.shape
    eye = jnp.eye(nb, dtype=w.dtype)
    return jnp.einsum("znij,nm->znimj", w, eye).reshape(z, nb * c, nb * c)


def _regroup_w_in(w):
    sizes = [512, 512, 512, 128, 128, 128, 512, 512, 256, 256, 512, 512, 32, 512, 512, 512, 512, 8, 8]
    offs = np.concatenate([[0], np.cumsum(sizes)])
    part = [w[:, offs[i]:offs[i + 1]] for i in range(len(sizes))]
    pad = lambda n: jnp.zeros((w.shape[0], n), w.dtype)
    rw = jnp.concatenate(part[0:6], axis=1)
    lru = jnp.concatenate(part[6:8], axis=1)
    gla = jnp.concatenate(part[8:13] + [pad(96)], axis=1)
    ml = jnp.concatenate(part[13:19] + [pad(112)], axis=1)
    return [t.astype(BF16) for t in (rw, lru, gla, ml)]


def _mixers(hx, n_b, lw, states, latent):
    r = hx.shape[0]
    length = r // n_b
    u_rw, u_lru, u_gla, u_ml = (_mm(hx, w) for w in lw["w_in"])
    s_rw, s_lru, s_gla, s_ml = states

    ya_f, ya_b, s_rw = _rwkv(u_rw, n_b, *lw["rw"], s_rw)
    if latent:
        rows = length // GRID_W
        x3 = u_lru.reshape(n_b, rows, GRID_W * LRU_COLS)
        hb_f, hb_b, s_lru = _lru_cols(x3, lw["lru"], s_lru, rows, stride=LRU_COLS // W4)
        hb_f = hb_f.reshape(r, W4)
        hb_b = hb_b.reshape(r, W4)
    else:
        x3 = u_lru.reshape(n_b, length, LRU_COLS)
        hb_f, hb_b, s_lru = _lru_cols(x3, lw["lru"], s_lru, length, stride=LRU_COLS // W4)
        hb_f = hb_f.reshape(r, W4)
        hb_b = hb_b.reshape(r, W4)
    oc_f, oc_b, s_gla = _gla(u_gla, n_b, *lw["gla"], s_gla)
    od_f, od_b, *s_ml = _mlstm(u_ml, n_b, lw["ml_bias"], *s_ml)
    ys = _branches(u_rw, u_lru, u_gla, u_ml, (ya_f, ya_b), (hb_f, hb_b), (oc_f, oc_b), (od_f, od_b), *lw["branch"])
    return ys, (s_rw, s_lru, s_gla, tuple(s_ml))


def _lru_cols(x3, lru_w, h0, t, stride):
    n_b = x3.shape[0]
    nblk = x3.shape[2] // (stride * W4)
    blk = lambda f: pl.BlockSpec((None, t, W4), f)
    cl = lambda v: jnp.clip(v, 0, nblk - 1)
    specs = [blk(lambda b, j: (b, 0, stride * cl(j - 1))), blk(lambda b, j: (b, 0, stride * j)),
             blk(lambda b, j: (b, 0, stride * cl(j + 1))),
             blk(lambda b, j: (b, 0, stride * cl(nblk - 2 - j))), blk(lambda b, j: (b, 0, stride * (nblk - 1 - j))),
             blk(lambda b, j: (b, 0, stride * cl(nblk - j)))]
    full = lambda shp: pl.BlockSpec(shp, lambda b, j: (0,) * len(shp))
    st_spec = pl.BlockSpec((None, 2, W4), lambda b, j: (b, 0, 0))
    out_sds = jax.ShapeDtypeStruct((n_b, t, nblk * W4), F32)
    return pl.pallas_call(
        functools.partial(_lru_kernel, t=t, nblk=nblk),
        grid=(n_b, nblk),
        in_specs=specs + [full((4, W4)), full((1, W4)), full((2, W4, W4)), full((2, 1, W4)),
                          full((2, W4, W4)), full((2, 1, W4)), full((2, 1, W4)), st_spec],
        out_specs=[blk(lambda b, j: (b, 0, j)), blk(lambda b, j: (b, 0, nblk - 1 - j)), st_spec],
        out_shape=[out_sds, out_sds, jax.ShapeDtypeStruct(h0.shape, F32)],
        scratch_shapes=[pltpu.VMEM((t + 16, W4), F32), pltpu.VMEM((2, t, W4), F32), pltpu.VMEM((2, t, W4), F32)],
        compiler_params=_cparams("parallel", "arbitrary"),
        name="rglru",
    )(x3, x3, x3, x3, x3, x3, *lru_w, h0)


def _zero_states(n_b):
    return (jnp.zeros((n_b, 2, RW_HEADS // 2, 128, 128), F32),
            jnp.zeros((n_b, 2, W4), F32),
            jnp.zeros((n_b, 2, GLA_HEADS // 2, GLA_DV, 2 * GLA_DK), F32),
            (jnp.zeros((n_b, 2, ML_HEADS, ML_DH, ML_DH), F32),
             jnp.zeros((n_b, 2, ML_HEADS, 1, ML_DH), F32),
             jnp.full((n_b, 2, ML_HEADS, 1, 1), NEG, F32)))


def kernel(x, c, ctx, c_ctx, ada_w, ada_b, norm_mix_w, w_in, rw_w_up, rw_w0, rw_a_up, rw_a0, rw_g_up, rw_k_k, rw_k_a, rw_r_k, rw_ln_w, lru_conv_w, lru_conv_b, lru_w_a, lru_b_a, lru_w_x, lru_b_x, lru_lambda, gla_w_up, gla_w0, gla_ln_w, ml_i_b, ml_f_b, ml_ln_w, br_w, gate_w, gate_b, out_w, norm_ffn_w, ffn_w1, ffn_w2, final_norm_w):
    n_b, seq, _ = x.shape
    n_c, ctx_len, _ = ctx.shape
    xs = x.reshape(n_b * seq, D)
    cs = ctx.reshape(n_c * ctx_len, D)
    cc = jnp.zeros((8, D), F32).at[0:n_b].set(c).at[n_b].set(c_ctx)
    e64 = _block_ones(W4, RW_HEAD)
    e128 = _block_ones(W4, GLA_DV)
    row = lambda v: v.reshape(1, -1)

    for l in range(DEPTH):
        mod = _ada(cc, ada_w[l], ada_b[l]).reshape(8, 6, D)
        mod = jnp.pad(mod, ((0, 0), (0, 2), (0, 0)))
        mod_x = mod[0:n_b]
        mod_c = mod[n_b:n_b + 1]
        lw = {
            "w_in": _regroup_w_in(w_in[l]),
            "rw": (rw_w_up[l], rw_w0[l][:, None, :], rw_a_up[l], rw_a0[l][:, None, :],
                   row(rw_k_k[l]), row(rw_k_a[l])),
            "lru": (lru_conv_w[l], row(lru_conv_b[l]), _block_diag(lru_w_a[l]).astype(BF16), lru_b_a[l][:, None, :],
                    _block_diag(lru_w_x[l]).astype(BF16), lru_b_x[l][:, None, :], lru_lambda[l][:, None, :]),
            "gla": (gla_w_up[l], gla_w0[l][:, None, :]),
            "ml_bias": jnp.zeros((1, 128), F32).at[0, 0:8].set(ml_i_b[l].reshape(-1)).at[0, 8:16].set(ml_f_b[l].reshape(-1)),
            "branch": (row(rw_ln_w[l]), row(rw_r_k[l]), rw_g_up[l].astype(BF16), row(gla_ln_w[l]), row(ml_ln_w[l]),
                       e64, e128),
        }
        gw = gate_w[l].astype(BF16)
        gb = gate_b[l][:, None, :]
        bw = br_w[l].astype(BF16)
        ow = out_w[l].astype(BF16)
        w1 = ffn_w1[l].astype(BF16)
        w2 = ffn_w2[l].astype(BF16)

        hc = _norm(cs, norm_mix_w[l], mod_c, (0, 1), BF16)
        ys_c, st_c = _mixers(hc, n_c, lw, _zero_states(n_c), latent=False)
        hx = _norm(xs, norm_mix_w[l], mod_x, (0, 1), BF16)
        ys_x, _ = _mixers(hx, n_b, lw, st_c, latent=True)
        xs = _proj_residual(_gated_sum(hx, ys_x, gw, gb, bw), ow, xs, mod_x, 2)
        xs = _ffn(_norm(xs, norm_ffn_w[l], mod_x, (3, 4), BF16), w1, w2, xs, mod_x, 5)
        if l < DEPTH - 1:
            cs = _proj_residual(_gated_sum(hc, ys_c, gw, gb, bw), ow, cs, mod_c, 2)
            cs = _ffn(_norm(cs, norm_ffn_w[l], mod_c, (3, 4), BF16), w1, w2, cs, mod_c, 5)
    out = _norm(xs, final_norm_w, mod_x, None, F32)
    return out.reshape(n_b, seq, D)
```

```python
import functools

import numpy as np
import jax
import jax.numpy as jnp
from jax import lax
from jax.experimental import pallas as pl
from jax.experimental.pallas import tpu as pltpu

F32 = jnp.float32
BF16 = jnp.bfloat16
HI = lax.Precision.HIGHEST

D = 2048
DEPTH = 2
GRID_W = 64
EPS = 1e-6
NEG = -1e30
CH = 64
D_FF = 4 * D
W4 = D // 4

RW_HEAD = 64
RW_HEADS = W4 // RW_HEAD
RW_DECAY_SCALE = 0.6065306597
LRU_BLOCKS = 8
LRU_C = 8.0
GLA_HEADS = 4
GLA_DK = 64
GLA_DV = 128
GLA_RANK = 16
GLA_TAU = 16.0
ML_HEADS = 4
ML_DH = 128

RW_COLS = 3 * W4 + 3 * 128
LRU_COLS = 2 * W4
GLA_COLS = 256 + 256 + 512 + 512 + 128
ML_COLS = 4 * W4 + 128

VMEM_LIMIT = 56 * 1024 * 1024


def _cparams(*sem):
    return pltpu.CompilerParams(dimension_semantics=sem, vmem_limit_bytes=VMEM_LIMIT)


def _dot(a, b, prec=None):
    return jnp.dot(a, b, precision=prec, preferred_element_type=F32)


def _dot_nt(a, b, prec=None):
    return lax.dot_general(a, b, (((1,), (1,)), ((), ())), precision=prec, preferred_element_type=F32)


def _dot_tn(a, b, prec=None):
    return lax.dot_general(a, b, (((0,), (0,)), ((), ())), precision=prec, preferred_element_type=F32)


def _b(x):
    return x.astype(BF16)


def _bdot(a, b):
    return jnp.dot(_b(a), _b(b), preferred_element_type=F32)


def _bdot_nt(a, b):
    return _dot_nt(_b(a), _b(b))


def _bdot_tn(a, b):
    return _dot_tn(_b(a), _b(b))


def _split_bf16(x, parts):
    out = []
    for i in range(parts):
        hi = _b(x)
        out.append(hi)
        if i < parts - 1:
            x = x - hi.astype(F32)
    return out


def _cumdot(mask, x):
    m = jnp.where(mask, 1.0, 0.0).astype(BF16)
    parts = _split_bf16(x, 3)
    acc = _dot(m, parts[0])
    for p in parts[1:]:
        acc = acc + _dot(m, p)
    return acc


def _headsum(x, e):
    hi, lo = _split_bf16(x, 2)
    return _dot(hi, e) + _dot(lo, e)


def _stack_heads(x):
    lo = lax.broadcasted_iota(jnp.int32, x.shape, 1) < 64
    return jnp.concatenate([jnp.where(lo, x, 0.0), jnp.where(lo, 0.0, x)], axis=0)


def _pair_scan_mask(inclusive, reverse):
    r = lax.broadcasted_iota(jnp.int32, (2 * CH, 2 * CH), 0) & (CH - 1)
    c = lax.broadcasted_iota(jnp.int32, (2 * CH, 2 * CH), 1) & (CH - 1)
    if reverse:
        return (r <= c) if inclusive else (r < c)
    return (r >= c) if inclusive else (r > c)


def _for_chunks(nchunk, body):
    for ci in range(nchunk):
        body(slice(ci * CH, (ci + 1) * CH), slice((nchunk - 1 - ci) * CH, (nchunk - ci) * CH))


def _sigmoid(x):
    return 1.0 / (1.0 + jnp.exp(-x))


def _log_sigmoid(x):
    return jnp.minimum(x, 0.0) - jnp.log1p(jnp.exp(-jnp.abs(x)))


def _softplus(x):
    return jnp.maximum(x, 0.0) + jnp.log1p(jnp.exp(-jnp.abs(x)))


def _scan_mask(n, inclusive, reverse):
    r = lax.broadcasted_iota(jnp.int32, (n, n), 0)
    c = lax.broadcasted_iota(jnp.int32, (n, n), 1)
    if reverse:
        return (r <= c) if inclusive else (r < c)
    return (r >= c) if inclusive else (r > c)


def _ada_kernel(s_ref, w_ref, b_ref, o_ref):
    s = s_ref[...]
    s = s * _sigmoid(s)
    o_ref[...] = _dot(s, w_ref[...], HI) + b_ref[...]


def _ada(cc, w, b):
    tn = 1024
    n_l = w.shape[0]
    return pl.pallas_call(
        _ada_kernel,
        grid=(n_l, 6 * D // tn),
        in_specs=[pl.BlockSpec((8, D), lambda l, j: (0, 0)),
                  pl.BlockSpec((None, D, tn), lambda l, j: (l, 0, j)),
                  pl.BlockSpec((None, 1, tn), lambda l, j: (l, 0, j))],
        out_specs=pl.BlockSpec((None, 8, tn), lambda l, j: (l, 0, j)),
        out_shape=jax.ShapeDtypeStruct((n_l, 8, 6 * D), F32),
        compiler_params=_cparams("parallel", "parallel"),
        name="ada",
    )(cc, w, b.reshape(n_l, 1, 6 * D))


def _rms_mod(x, w, m, rows):
    y = x * lax.rsqrt(jnp.mean(x * x, axis=-1, keepdims=True) + EPS) * w
    if rows is not None:
        shift, scale = rows
        y = y * (1.0 + m[scale:scale + 1]) + m[shift:shift + 1]
    return y


def _norm_kernel(x_ref, w_ref, mod_ref, o_ref, *, rows):
    o_ref[...] = _rms_mod(x_ref[...], w_ref[...], mod_ref[...], rows).astype(o_ref.dtype)


def _norm(x, w, mods, rows, out_dtype, tm=256):
    r = x.shape[0]
    g = mods.shape[0]
    per = r // g // tm
    return pl.pallas_call(
        functools.partial(_norm_kernel, rows=rows),
        grid=(r // tm,),
        in_specs=[pl.BlockSpec((tm, D), lambda i: (i, 0)),
                  pl.BlockSpec((1, D), lambda i: (0, 0)),
                  pl.BlockSpec((None, 8, D), lambda i: (i // per, 0, 0))],
        out_specs=pl.BlockSpec((tm, D), lambda i: (i, 0)),
        out_shape=jax.ShapeDtypeStruct((r, D), out_dtype),
        compiler_params=_cparams("parallel"),
        name="norm",
    )(x, w.reshape(1, D), mods)


def _mm_kernel(a_ref, w_ref, o_ref):
    o_ref[...] = _dot(a_ref[...], w_ref[...])


def _mm(a, w, tm=512):
    r, k = a.shape
    n = w.shape[1]
    tm = min(tm, r)
    return pl.pallas_call(
        _mm_kernel,
        grid=(r // tm,),
        in_specs=[pl.BlockSpec((tm, k), lambda i: (i, 0)),
                  pl.BlockSpec((k, n), lambda i: (0, 0))],
        out_specs=pl.BlockSpec((tm, n), lambda i: (i, 0)),
        out_shape=jax.ShapeDtypeStruct((r, n), F32),
        compiler_params=_cparams("parallel"),
        name="in_proj",
    )(a, w)


def _gate_kernel(hx_ref, ys_ref, gw_ref, gb_ref, bw_ref, o_ref):
    hx = hx_ref[...]
    acc = None
    for i in range(4):
        g = _dot(hx, gw_ref[i]) + gb_ref[i]
        p = _dot(ys_ref[:, i * W4:(i + 1) * W4], bw_ref[i])
        t = _sigmoid(g) * p
        acc = t if acc is None else acc + t
    o_ref[...] = acc.astype(o_ref.dtype)


def _gated_sum(hx, ys, gw, gb, bw, tm=512, tn=512):
    r = hx.shape[0]
    tm = min(tm, r)
    return pl.pallas_call(
        _gate_kernel,
        grid=(D // tn, r // tm),
        in_specs=[pl.BlockSpec((tm, D), lambda j, i: (i, 0)),
                  pl.BlockSpec((tm, D), lambda j, i: (i, 0)),
                  pl.BlockSpec((4, D, tn), lambda j, i: (0, 0, j)),
                  pl.BlockSpec((4, 1, tn), lambda j, i: (0, 0, j)),
                  pl.BlockSpec((4, W4, tn), lambda j, i: (0, 0, j))],
        out_specs=pl.BlockSpec((tm, tn), lambda j, i: (i, j)),
        out_shape=jax.ShapeDtypeStruct((r, D), BF16),
        compiler_params=_cparams("parallel", "parallel"),
        name="gated_sum",
    )(hx, ys, gw, gb, bw)


def _proj_res_kernel(a_ref, w_ref, x_ref, mod_ref, nw_ref, o_ref, h_ref):
    m = mod_ref[...]
    x = x_ref[...] + m[2:3] * _dot(a_ref[...], w_ref[...])
    o_ref[...] = x
    h_ref[...] = _rms_mod(x, nw_ref[...], m, (3, 4)).astype(h_ref.dtype)


def _proj_residual(a, w, x, mods, norm_w, tm=512):
    r = a.shape[0]
    tm = min(tm, r)
    per = r // mods.shape[0] // tm
    row = pl.BlockSpec((tm, D), lambda i: (i, 0))
    return pl.pallas_call(
        _proj_res_kernel,
        grid=(r // tm,),
        in_specs=[row, pl.BlockSpec((D, D), lambda i: (0, 0)), row,
                  pl.BlockSpec((None, 8, D), lambda i: (i // per, 0, 0)),
                  pl.BlockSpec((1, D), lambda i: (0, 0))],
        out_specs=[row, row],
        out_shape=[jax.ShapeDtypeStruct((r, D), F32), jax.ShapeDtypeStruct((r, D), BF16)],
        compiler_params=_cparams("parallel"),
        name="out_proj",
    )(a, w, x, mods, norm_w.reshape(1, D))


def _ffn_kernel(h_ref, w1_ref, w2_ref, x_ref, mod_ref, nw_ref, nmod_ref, *refs, final):
    o_refs, acc_ref = refs[:-1], refs[-1]
    f = pl.program_id(1)

    @pl.when(f == 0)
    def _():
        acc_ref[...] = jnp.zeros_like(acc_ref)

    a = jnp.maximum(_dot(h_ref[...], w1_ref[...]), 0.0)
    acc_ref[...] += _dot((a * a).astype(BF16), w2_ref[...])

    @pl.when(f == pl.num_programs(1) - 1)
    def _():
        x = x_ref[...] + mod_ref[...][5:6] * acc_ref[...]
        if final:
            o_refs[0][...] = _rms_mod(x, nw_ref[...], None, None)
        else:
            o_refs[0][...] = x
            o_refs[1][...] = _rms_mod(x, nw_ref[...], nmod_ref[...], (0, 1)).astype(BF16)


def _ffn(hn, w1, w2, x, mods, next_norm_w, next_mods, final, tm=512, tf=1024):
    r = hn.shape[0]
    tm = min(tm, r)
    per = r // mods.shape[0] // tm
    row = pl.BlockSpec((tm, D), lambda i, f: (i, 0))
    mod_spec = pl.BlockSpec((None, 8, D), lambda i, f: (i // per, 0, 0))
    out_specs = [row] if final else [row, row]
    out_shape = [jax.ShapeDtypeStruct((r, D), F32)] + ([] if final else [jax.ShapeDtypeStruct((r, D), BF16)])
    return pl.pallas_call(
        functools.partial(_ffn_kernel, final=final),
        grid=(r // tm, D_FF // tf),
        in_specs=[row, pl.BlockSpec((D, tf), lambda i, f: (0, f)), pl.BlockSpec((tf, D), lambda i, f: (f, 0)),
                  row, mod_spec, pl.BlockSpec((1, D), lambda i, f: (0, 0)), mod_spec],
        out_specs=out_specs,
        out_shape=out_shape,
        scratch_shapes=[pltpu.VMEM((tm, D), F32)],
        compiler_params=_cparams("parallel", "arbitrary"),
        name="ffn",
    )(hn, w1, w2, x, mods, next_norm_w.reshape(1, D), next_mods)


def _rwkv_chunk(b, z, r, k, v, wd, ad, w_up, w0, a_up, a0, k_k, k_a):
    rev = z == 1
    last = 0 if rev else CH - 1
    lo = lax.broadcasted_iota(jnp.int32, (CH, 128), 1) < 64

    wdz = wd[:, 64 * z:64 * z + 64]
    adz = ad[:, 64 * z:64 * z + 64]
    lw = -RW_DECAY_SCALE * _sigmoid(_dot(jnp.tanh(wdz), w_up, HI) + w0)
    a = _sigmoid(_dot(adz, a_up, HI) + a0)
    kk = k * k_k
    kd = k * (1.0 + (a - 1.0) * k_a)

    lc_incl = _cumdot(_scan_mask(CH, True, rev), lw)
    p_incl = jnp.exp(lc_incl)
    p_inv = jnp.exp(-lc_incl)
    p_excl = jnp.exp(lc_incl - lw)
    p_end = p_incl[last:last + 1]

    chains = []
    for p in range(RW_HEADS // 2):
        sl = slice(128 * p, 128 * (p + 1))
        kkp = kk[:, sl]
        sq = kkp * kkp
        nrm = jnp.where(lo, jnp.sum(jnp.where(lo, sq, 0.0), axis=-1, keepdims=True),
                        jnp.sum(jnp.where(lo, 0.0, sq), axis=-1, keepdims=True))
        kkp = kkp / jnp.maximum(jnp.sqrt(nrm), 1e-12)
        at = -kkp * p_excl[:, sl]
        bt = kkp * a[:, sl] * p_inv[:, sl]
        kt = kd[:, sl] * p_inv[:, sl]
        rt = r[:, sl] * p_incl[:, sl]
        chains.append(dict(
            z=z, key=(b, z), idx=(b, z, p), p_end=p_end[:, sl],
            ar=_b(jnp.concatenate([_stack_heads(at), _stack_heads(rt)], axis=0)),
            bk=_b(jnp.concatenate([_stack_heads(bt), _stack_heads(kt)], axis=0)),
            vs=_b(_stack_heads(v[:, sl]))))
    return chains


def _rwkv_solve(chains, s_ref):
    n2 = 2 * CH
    masks = {z: (_pair_scan_mask(True, z == 1), _pair_scan_mask(False, z == 1)) for z in (0, 1)}
    s0 = [s_ref[c["idx"]] for c in chains]
    g = [_dot_nt(c["ar"], c["bk"]) for c in chains]
    s0t = [_dot_nt(c["ar"], _b(s)) for c, s in zip(chains, s0)]
    lp = [_b(jnp.where(masks[c["z"]][1], gi[:n2, :n2], 0.0)) for c, gi in zip(chains, g)]
    u = [st[:n2] + _dot(_b(jnp.where(masks[c["z"]][1], gi[:n2, n2:], 0.0)), c["vs"])
         for c, gi, st in zip(chains, g, s0t)]
    for j in range(6):
        u = [ui + _dot(li, _b(ui)) for ui, li in zip(u, lp)]
        if j < 5:
            lp = [_b(_dot(li, li)) for li in lp]
    ys = {}
    for c, gi, st, ui, s in zip(chains, g, s0t, u, s0):
        m_incl = masks[c["z"]][0]
        uv = jnp.concatenate([_b(ui), c["vs"]], axis=0)
        rbk = _b(jnp.concatenate([jnp.where(m_incl, gi[n2:, :n2], 0.0), jnp.where(m_incl, gi[n2:, n2:], 0.0)], axis=1))
        y = st[n2:] + _dot(rbk, uv)
        ys.setdefault(c["key"], []).append(y[:CH] + y[CH:])
        s_ref[c["idx"]] = (s + _dot_tn(uv, c["bk"])) * c["p_end"]
    return {key: jnp.concatenate(parts, axis=1) for key, parts in ys.items()}


def _seq_block_specs(n_b, tb, nblk, cols):
    fwd = [pl.BlockSpec((n_b, tb, w), functools.partial(lambda j, c: (0, j, c), c=c)) for w, c in cols]
    bwd = [pl.BlockSpec((n_b, tb, w), functools.partial(lambda j, c: (0, nblk - 1 - j, c), c=c)) for w, c in cols]
    return fwd, bwd


def _rwkv_kernel(*refs, n_b, nchunk):
    fwd, bwd = refs[0:5], refs[5:10]
    w_up, w0, a_up, a0, k_k, k_a, s0_ref, yf_ref, yb_ref, s_ref = refs[10:]

    @pl.when(pl.program_id(0) == 0)
    def _():
        s_ref[...] = s0_ref[...]

    def body(rows_f, rows_b):
        chains = []
        for b in range(n_b):
            for z, (seq, rows) in enumerate(((fwd, rows_f), (bwd, rows_b))):
                rr, kr, vr, wr, ar = (ref[b, rows, :] for ref in seq)
                chains += _rwkv_chunk(b, z, rr, kr, vr, wr, ar, w_up[z], w0[z], a_up[z], a0[z], k_k[...], k_a[...])
        ys = _rwkv_solve(chains, s_ref)
        for b in range(n_b):
            yf_ref[b, rows_f, :] = ys[(b, 0)]
            yb_ref[b, rows_b, :] = ys[(b, 1)]

    _for_chunks(nchunk, body)


def _rwkv(u, w_up, w0, a_up, a0, k_k, k_a, s0, tb=256):
    n_b, length, _ = u.shape
    nblk = length // tb
    fwd, bwd = _seq_block_specs(n_b, tb, nblk, [(W4, 0), (W4, 1), (W4, 2), (128, 12), (128, 13)])
    full = lambda shp: pl.BlockSpec(shp, lambda j: (0,) * len(shp))
    out_sds = jax.ShapeDtypeStruct((n_b, length, W4), F32)
    return pl.pallas_call(
        functools.partial(_rwkv_kernel, n_b=n_b, nchunk=tb // CH),
        grid=(nblk,),
        in_specs=fwd + bwd + [full((2, 64, W4)), full((2, 1, W4)), full((2, 64, W4)), full((2, 1, W4)),
                              full((1, W4)), full((1, W4)), full(s0.shape)],
        out_specs=[fwd[0], bwd[0], full(s0.shape)],
        out_shape=[out_sds, out_sds, jax.ShapeDtypeStruct(s0.shape, F32)],
        compiler_params=_cparams("arbitrary"),
        name="rwkv7",
    )(*([u] * 10), w_up, w0, a_up, a0, k_k, k_a, s0)


def _lru_kernel(xf, xb, tail, head, conv_w, conv_b, wa, ba, wx, bx, lam, h0_ref,
                hf_ref, hb_ref, h_ref, ext, a_s, b_s, *, n_b, t, wpb, nscan, wmajor):
    j = pl.program_id(0)
    nstep = nscan // wpb

    def rows(ref, b, w, sl):
        return ref[b, w, sl, :] if wmajor else ref[b, sl, w, :]

    def put(ref, b, w, sl, val):
        if wmajor:
            ref[b, w, sl, :] = val
        else:
            ref[b, sl, w, :] = val

    @pl.when(j == 0)
    def _():
        h_ref[...] = h0_ref[...]

    for wi in range(wpb):
        local = (wi, wpb - 1 - wi)
        glob = (j * wpb + wi, (nstep - 1 - j) * wpb + wpb - 1 - wi)
        for z, xr in enumerate((xf, xb)):
            w, g = local[z], glob[z]
            for b in range(n_b):
                if w > 0:
                    prev2 = rows(xr, b, w - 1, slice(t - 2, t))
                else:
                    prev2 = jnp.where(g > 0, tail[b, jnp.maximum(g - 1, 0)], 0.0)
                if w < wpb - 1:
                    next1 = rows(xr, b, w + 1, slice(0, 1))
                else:
                    next1 = jnp.where(g < nscan - 1, head[b, pl.ds(jnp.minimum(g + 1, nscan - 1), 1), :], 0.0)
                ext[b, z, 6:8, :] = prev2
                ext[b, z, 8:8 + t, :] = rows(xr, b, w, slice(None))
                ext[b, z, 8 + t:9 + t, :] = next1
                xc = conv_b[...] + ext[b, z, 6:6 + t, :] * conv_w[0:1, :]
                for tap in range(1, 4):
                    xc = xc + ext[b, z, 6 + tap:6 + tap + t, :] * conv_w[tap:tap + 1, :]
                xcb = _b(xc)
                rg = _sigmoid(_dot(xcb, wa[z]) + ba[z])
                ig = _sigmoid(_dot(xcb, wx[z]) + bx[z])
                log_a = -LRU_C * rg * _softplus(-lam[z])
                a = jnp.exp(log_a)
                a_s[b, z] = a
                b_s[b, z] = jnp.sqrt(-jnp.tanh(log_a) * (a * a + 1.0)) * ig * xc

        def step(i, carry):
            rf = pl.ds(i, 1)
            rb = pl.ds(t - 1 - i, 1)
            out = []
            for b in range(n_b):
                h_f = a_s[b, 0, rf, :] * carry[2 * b] + b_s[b, 0, rf, :]
                h_b = a_s[b, 1, rb, :] * carry[2 * b + 1] + b_s[b, 1, rb, :]
                put(hf_ref, b, local[0], rf, h_f)
                put(hb_ref, b, local[1], rb, h_b)
                out += [h_f, h_b]
            return tuple(out)

        init = tuple(h_ref[b, z:z + 1, :] for b in range(n_b) for z in (0, 1))
        fin = lax.fori_loop(0, t, step, init, unroll=8)
        for b in range(n_b):
            h_ref[b, 0:1, :] = fin[2 * b]
            h_ref[b, 1:2, :] = fin[2 * b + 1]


def _lru(x4, tail, head, lru_w, h0, wpb, wmajor):
    n_b = x4.shape[0]
    nscan, t = (x4.shape[1], x4.shape[2]) if wmajor else (x4.shape[2], x4.shape[1])
    nstep = nscan // wpb
    shp = (n_b, wpb, t, W4) if wmajor else (n_b, t, wpb, W4)
    at = (lambda j: (0, j, 0, 0)) if wmajor else (lambda j: (0, 0, j, 0))
    at_rev = (lambda j: (0, nstep - 1 - j, 0, 0)) if wmajor else (lambda j: (0, 0, nstep - 1 - j, 0))
    full = lambda s: pl.BlockSpec(s, lambda j: (0,) * len(s))
    out_sds = jax.ShapeDtypeStruct(x4.shape[:3] + (W4,), F32)
    return pl.pallas_call(
        functools.partial(_lru_kernel, n_b=n_b, t=t, wpb=wpb, nscan=nscan, wmajor=wmajor),
        grid=(nstep,),
        in_specs=[pl.BlockSpec(shp, at), pl.BlockSpec(shp, at_rev), full(tail.shape), full(head.shape),
                  full((4, W4)), full((1, W4)), full((2, W4, W4)), full((2, 1, W4)),
                  full((2, W4, W4)), full((2, 1, W4)), full((2, 1, W4)), full(h0.shape)],
        out_specs=[pl.BlockSpec(shp, at), pl.BlockSpec(shp, at_rev), full(h0.shape)],
        out_shape=[out_sds, out_sds, jax.ShapeDtypeStruct(h0.shape, F32)],
        scratch_shapes=[pltpu.VMEM((n_b, 2, t + 16, W4), F32), pltpu.VMEM((n_b, 2, t, W4), F32),
                        pltpu.VMEM((n_b, 2, t, W4), F32)],
        compiler_params=_cparams("arbitrary"),
        name="rglru",
    )(x4, x4, tail, head, *lru_w, h0)


def _gla_chunk(b, z, q, k, v, wd, w_up, w0):
    rev = z == 1
    last = 0 if rev else CH - 1
    pm_incl = _pair_scan_mask(True, rev)
    la = _log_sigmoid(_dot(wd[:, GLA_RANK * z:GLA_RANK * (z + 1)], w_up, HI) + w0) * (1.0 / GLA_TAU)
    bcum = _cumdot(_scan_mask(CH, True, rev), la)
    b_end = bcum[last:last + 1]
    qe = q * (GLA_DK ** -0.5) * jnp.exp(bcum)
    ke = k * jnp.exp(-bcum)
    kd = k * jnp.exp(b_end - bcum)
    e_end = jnp.exp(b_end)
    chains = []
    for p in range(GLA_HEADS // 2):
        sl = slice(128 * p, 128 * (p + 1))
        chains.append(dict(
            key=(b, z), idx=(b, z, p), mask=pm_incl, e_end=e_end[:, sl],
            qs=_b(_stack_heads(qe[:, sl])),
            ks=_b(_stack_heads(ke[:, sl])),
            kds=_b(_stack_heads(kd[:, sl])),
            vs=_b(jnp.concatenate([v[:, 2 * GLA_DV * p:2 * GLA_DV * p + GLA_DV],
                                   v[:, 2 * GLA_DV * p + GLA_DV:2 * GLA_DV * (p + 1)]], axis=0))))
    return chains


def _gla_solve(chains, s_ref):
    st = [s_ref[c["idx"]] for c in chains]
    att = [_b(jnp.where(c["mask"], _dot_nt(c["qs"], c["ks"]), 0.0)) for c in chains]
    inter = [_dot_nt(c["qs"], _b(s)) for c, s in zip(chains, st)]
    outs = {}
    for c, a, i, s in zip(chains, att, inter, st):
        o = _dot(a, c["vs"]) + i
        outs.setdefault(c["key"], []).extend([o[:CH], o[CH:]])
        s_ref[c["idx"]] = s * c["e_end"] + _dot_tn(c["vs"], c["kds"])
    return {key: jnp.concatenate(parts, axis=1) for key, parts in outs.items()}


def _gla_kernel(*refs, n_b, nchunk):
    fwd, bwd = refs[0:4], refs[4:8]
    w_up, w0, s0_ref, of_ref, ob_ref, s_ref = refs[8:]

    @pl.when(pl.program_id(0) == 0)
    def _():
        s_ref[...] = s0_ref[...]

    def body(rows_f, rows_b):
        chains = []
        for b in range(n_b):
            for z, (seq, rows) in enumerate(((fwd, rows_f), (bwd, rows_b))):
                q, k, v, wd = (ref[b, rows, :] for ref in seq)
                chains += _gla_chunk(b, z, q, k, v, wd, w_up[z], w0[z])
        outs = _gla_solve(chains, s_ref)
        for b in range(n_b):
            of_ref[b, rows_f, :] = outs[(b, 0)]
            ob_ref[b, rows_b, :] = outs[(b, 1)]

    _for_chunks(nchunk, body)


def _gla(u, w_up, w0, s0, tb=256):
    n_b, length, _ = u.shape
    nblk = length // tb
    fwd, bwd = _seq_block_specs(n_b, tb, nblk, [(256, 0), (256, 1), (512, 1), (128, 12)])
    full = lambda shp: pl.BlockSpec(shp, lambda j: (0,) * len(shp))
    out_sds = jax.ShapeDtypeStruct((n_b, length, W4), F32)
    out_f, out_b = _seq_block_specs(n_b, tb, nblk, [(W4, 0)])
    return pl.pallas_call(
        functools.partial(_gla_kernel, n_b=n_b, nchunk=tb // CH),
        grid=(nblk,),
        in_specs=fwd + bwd + [full((2, GLA_RANK, 256)), full((2, 1, 256)), full(s0.shape)],
        out_specs=[out_f[0], out_b[0], full(s0.shape)],
        out_shape=[out_sds, out_sds, jax.ShapeDtypeStruct(s0.shape, F32)],
        compiler_params=_cparams("arbitrary"),
        name="gla",
    )(*([u] * 8), w_up, w0, s0)


def _mlstm_chunk(b, z, q, k, v, gates, bias):
    rev = z == 1
    m_incl = _scan_mask(CH, True, rev)
    last = 0 if rev else CH - 1
    pre = gates + bias
    lf = _log_sigmoid(pre)
    fcum = _cumdot(m_incl, lf)
    pre_t = pre.T
    fcum_t = fcum.T
    k = k * (ML_DH ** -0.5)
    chains = []
    for h in range(ML_HEADS):
        ci = ML_HEADS * z + h
        cf = 2 * ML_HEADS + ML_HEADS * z + h
        sl = slice(ML_DH * h, ML_DH * (h + 1))
        fc = fcum[:, cf:cf + 1]
        ic = pre[:, ci:ci + 1]
        fr = fcum_t[cf:cf + 1, :]
        ir = pre_t[ci:ci + 1, :]
        chains.append(dict(key=(b, z), idx=(b, z, h), q=q[:, sl], k=k[:, sl], v=v[:, sl],
                           qb=_b(q[:, sl]), kb=_b(k[:, sl]),
                           fc=fc, ic=ic, f_end=fc[last:last + 1],
                           dlog=jnp.where(m_incl, fc + (ir - fr), NEG)))
    return chains


def _mlstm_solve(chains, cm_ref, n_ref, m_ref):
    cm = [cm_ref[c["idx"]] for c in chains]
    n = [n_ref[c["idx"]] for c in chains]
    m = [m_ref[c["idx"]] for c in chains]
    qk = [_dot_nt(c["qb"], c["kb"]) for c in chains]
    qc = [_dot_nt(c["qb"], _b(s)) for c, s in zip(chains, cm)]
    inter = [c["fc"] + mi for c, mi in zip(chains, m)]
    mt = [jnp.maximum(i, jnp.max(c["dlog"], axis=-1, keepdims=True)) for c, i in zip(chains, inter)]
    w = [jnp.exp(c["dlog"] - t) * s for c, t, s in zip(chains, mt, qk)]
    ei = [jnp.exp(i - t) for i, t in zip(inter, mt)]
    wv = [_dot(_b(wi), _b(c["v"])) for c, wi in zip(chains, w)]
    outs = {}
    for c, wi, e, t, a, qci, cmi, ni, mi in zip(chains, w, ei, mt, wv, qc, cm, n, m):
        num = a + e * qci
        den = jnp.sum(wi, axis=-1, keepdims=True) + e * jnp.sum(c["q"] * ni, axis=-1, keepdims=True)
        outs.setdefault(c["key"], []).append(num / jnp.maximum(jnp.abs(den), jnp.exp(-t)))
        g = c["f_end"] - c["fc"] + c["ic"]
        m_new = jnp.maximum(c["f_end"] + mi, jnp.max(g, axis=0, keepdims=True))
        dec = jnp.exp(c["f_end"] + mi - m_new)
        wg = jnp.exp(g - m_new)
        cm_ref[c["idx"]] = dec * cmi + _dot_tn(_b(c["v"] * wg), c["kb"])
        n_ref[c["idx"]] = dec * ni + jnp.sum(c["k"] * wg, axis=0, keepdims=True)
        m_ref[c["idx"]] = m_new
    return {key: jnp.concatenate(parts, axis=1) for key, parts in outs.items()}


def _mlstm_kernel(*refs, n_b, nchunk):
    fwd, bwd = refs[0:4], refs[4:8]
    bias, cm0, n0, m0, of_ref, ob_ref, cm_ref, n_ref, m_ref = refs[8:]

    @pl.when(pl.program_id(0) == 0)
    def _():
        cm_ref[...] = cm0[...]
        n_ref[...] = n0[...]
        m_ref[...] = m0[...]

    def body(rows_f, rows_b):
        chains = []
        for b in range(n_b):
            for z, (seq, rows) in enumerate(((fwd, rows_f), (bwd, rows_b))):
                q, k, v, gates = (ref[b, rows, :] for ref in seq)
                chains += _mlstm_chunk(b, z, q, k, v, gates, bias[...])
        outs = _mlstm_solve(chains, cm_ref, n_ref, m_ref)
        for b in range(n_b):
            of_ref[b, rows_f, :] = outs[(b, 0)]
            ob_ref[b, rows_b, :] = outs[(b, 1)]

    _for_chunks(nchunk, body)


def _mlstm(u, bias, cm0, n0, m0, tb=256):
    n_b, length, _ = u.shape
    nblk = length // tb
    fwd, bwd = _seq_block_specs(n_b, tb, nblk, [(W4, 0), (W4, 1), (W4, 2), (128, 16)])
    full = lambda shp: pl.BlockSpec(shp, lambda j: (0,) * len(shp))
    out_sds = jax.ShapeDtypeStruct((n_b, length, W4), F32)
    states = [full(cm0.shape), full(n0.shape), full(m0.shape)]
    return pl.pallas_call(
        functools.partial(_mlstm_kernel, n_b=n_b, nchunk=tb // CH),
        grid=(nblk,),
        in_specs=fwd + bwd + [full((1, 128))] + states,
        out_specs=[fwd[0], bwd[0]] + states,
        out_shape=[out_sds, out_sds, jax.ShapeDtypeStruct(cm0.shape, F32), jax.ShapeDtypeStruct(n0.shape, F32),
                   jax.ShapeDtypeStruct(m0.shape, F32)],
        compiler_params=_cparams("arbitrary"),
        name="mlstm",
    )(*([u] * 8), bias, cm0, n0, m0)


def _branch_kernel(yaf, yab, rr, rk, rv, rg, hbf, hbb, lg, ocf, ocb, gr, odf, odb, mo,
                   rw_ln, rw_rk, g_up, gla_ln, ml_ln, e64, e128, o_ref):
    y = yaf[...] + yab[...]
    y = y * lax.rsqrt(_headsum(y * y, e64[...]) * (1.0 / RW_HEAD) + EPS) * rw_ln[...]
    bonus = _headsum(rr[...] * rk[...] * rw_rk[...], e64[...]) * rv[...]
    g = _dot(_sigmoid(rg[...]).astype(BF16), g_up[...])
    o_ref[:, 0:W4] = ((y + bonus) * g).astype(o_ref.dtype)
    o_ref[:, W4:2 * W4] = ((hbf[...] + hbb[...]) * jax.nn.gelu(lg[...])).astype(o_ref.dtype)
    y = ocf[...] + ocb[...]
    y = y * lax.rsqrt(_headsum(y * y, e128[...]) * (1.0 / GLA_DV) + EPS) * gla_ln[...]
    r = gr[...]
    o_ref[:, 2 * W4:3 * W4] = (y * (r * _sigmoid(r))).astype(o_ref.dtype)
    y = odf[...] + odb[...]
    y = y * lax.rsqrt(_headsum(y * y, e128[...]) * (1.0 / ML_DH) + EPS) * ml_ln[...]
    o_ref[:, 3 * W4:4 * W4] = (y * _sigmoid(mo[...])).astype(o_ref.dtype)


def _branches(u_rw, u_lru, u_gla, u_ml, ya, hb, oc, od, rw_ln, rw_rk, g_up, gla_ln, ml_ln, e64, e128, tm=256):
    r = u_rw.shape[0]
    row = lambda w, c: pl.BlockSpec((tm, w), lambda i: (i, c))
    full = lambda shp: pl.BlockSpec(shp, lambda i: (0,) * len(shp))
    return pl.pallas_call(
        _branch_kernel,
        grid=(r // tm,),
        in_specs=[row(W4, 0), row(W4, 0), row(W4, 0), row(W4, 1), row(W4, 2), row(128, 14),
                  row(W4, 0), row(W4, 0), row(W4, 1),
                  row(W4, 0), row(W4, 0), row(W4, 2),
                  row(W4, 0), row(W4, 0), row(W4, 3),
                  full((1, W4)), full((1, W4)), full((128, W4)), full((1, W4)), full((1, W4)),
                  full((W4, W4)), full((W4, W4))],
        out_specs=pl.BlockSpec((tm, D), lambda i: (i, 0)),
        out_shape=jax.ShapeDtypeStruct((r, D), BF16),
        compiler_params=_cparams("parallel"),
        name="branches",
    )(ya[0], ya[1], u_rw, u_rw, u_rw, u_rw, hb[0], hb[1], u_lru, oc[0], oc[1], u_gla, od[0], od[1], u_ml,
      rw_ln, rw_rk, g_up, gla_ln, ml_ln, e64, e128)


def _block_ones(width, head):
    i = np.arange(width) // head
    return jnp.asarray((i[:, None] == i[None, :]).astype(np.float32)).astype(BF16)


def _block_diag(w):
    z, nb, c, _ = w.shape
    eye = jnp.eye(nb, dtype=w.dtype)
    return jnp.einsum("znij,nm->znimj", w, eye).reshape(z, nb * c, nb * c)


def _regroup_w_in(w):
    sizes = [512, 512, 512, 128, 128, 128, 512, 512, 256, 256, 512, 512, 32, 512, 512, 512, 512, 8, 8]
    offs = np.concatenate([[0], np.cumsum(sizes)])
    part = [w[:, offs[i]:offs[i + 1]] for i in range(len(sizes))]
    pad = lambda n: jnp.zeros((w.shape[0], n), w.dtype)
    rw = jnp.concatenate(part[0:6], axis=1)
    lru = jnp.concatenate(part[6:8], axis=1)
    gla = jnp.concatenate(part[8:13] + [pad(96)], axis=1)
    ml = jnp.concatenate(part[13:19] + [pad(112)], axis=1)
    return [t.astype(BF16) for t in (rw, lru, gla, ml)]


def _mixers(hx, n_b, lw, states, latent):
    r = hx.shape[0]
    length = r // n_b
    u_rw, u_lru, u_gla, u_ml = (_mm(hx, w) for w in lw["w_in"])
    seq = lambda u: u.reshape(n_b, length, u.shape[-1])
    flat = lambda u: u.reshape(r, u.shape[-1])
    s_rw, s_lru, s_gla, s_ml = states

    ya_f, ya_b, s_rw = _rwkv(seq(u_rw), *lw["rw"], s_rw)
    if latent:
        rows = length // GRID_W
        x4 = u_lru.reshape(n_b, rows, GRID_W, LRU_COLS)
        tail = jnp.swapaxes(x4[:, rows - 2:rows, :, :W4], 1, 2)
        head = x4[:, 0, :, :W4]
        hb_f, hb_b, s_lru = _lru(x4, tail, head, lw["lru"], s_lru, wpb=8, wmajor=False)
    else:
        x4 = u_lru.reshape(n_b, 1, length, LRU_COLS)
        hb_f, hb_b, s_lru = _lru(x4, jnp.zeros((n_b, 1, 2, W4), F32), jnp.zeros((n_b, 1, W4), F32), lw["lru"], s_lru,
                                 wpb=1, wmajor=True)
    oc_f, oc_b, s_gla = _gla(seq(u_gla), *lw["gla"], s_gla)
    od_f, od_b, *s_ml = _mlstm(seq(u_ml), lw["ml_bias"], *s_ml)
    ys = _branches(u_rw, u_lru, u_gla, u_ml, (flat(ya_f), flat(ya_b)), (flat(hb_f), flat(hb_b)),
                   (flat(oc_f), flat(oc_b)), (flat(od_f), flat(od_b)), *lw["branch"])
    return ys, (s_rw, s_lru, s_gla, tuple(s_ml))


def _zero_states(n_b):
    return (jnp.zeros((n_b, 2, RW_HEADS // 2, 128, 128), F32),
            jnp.zeros((n_b, 2, W4), F32),
            jnp.zeros((n_b, 2, GLA_HEADS // 2, GLA_DV, 2 * GLA_DK), F32),
            (jnp.zeros((n_b, 2, ML_HEADS, ML_DH, ML_DH), F32),
             jnp.zeros((n_b, 2, ML_HEADS, 1, ML_DH), F32),
             jnp.full((n_b, 2, ML_HEADS, 1, 1), NEG, F32)))


def kernel(x, c, ctx, c_ctx, ada_w, ada_b, norm_mix_w, w_in, rw_w_up, rw_w0, rw_a_up, rw_a0, rw_g_up, rw_k_k, rw_k_a, rw_r_k, rw_ln_w, lru_conv_w, lru_conv_b, lru_w_a, lru_b_a, lru_w_x, lru_b_x, lru_lambda, gla_w_up, gla_w0, gla_ln_w, ml_i_b, ml_f_b, ml_ln_w, br_w, gate_w, gate_b, out_w, norm_ffn_w, ffn_w1, ffn_w2, final_norm_w):
    n_b, seq, _ = x.shape
    n_c, ctx_len, _ = ctx.shape
    xs = x.reshape(n_b * seq, D)
    cs = ctx.reshape(n_c * ctx_len, D)
    cc = jnp.zeros((8, D), F32).at[0:n_b].set(c).at[n_b].set(c_ctx)
    e64 = _block_ones(W4, RW_HEAD)
    e128 = _block_ones(W4, GLA_DV)
    row = lambda v: v.reshape(1, -1)

    mods = jnp.pad(_ada(cc, ada_w, ada_b).reshape(DEPTH, 8, 6, D), ((0, 0), (0, 0), (0, 2), (0, 0)))
    mod_x = [mods[l, 0:n_b] for l in range(DEPTH)]
    mod_c = [mods[l, n_b:n_b + 1] for l in range(DEPTH)]
    hx = _norm(xs, norm_mix_w[0], mod_x[0], (0, 1), BF16)
    hc = _norm(cs, norm_mix_w[0], mod_c[0], (0, 1), BF16)

    for l in range(DEPTH):
        last = l == DEPTH - 1
        lw = {
            "w_in": _regroup_w_in(w_in[l]),
            "rw": (rw_w_up[l], rw_w0[l][:, None, :], rw_a_up[l], rw_a0[l][:, None, :],
                   row(rw_k_k[l]), row(rw_k_a[l])),
            "lru": (lru_conv_w[l], row(lru_conv_b[l]), _block_diag(lru_w_a[l]).astype(BF16), lru_b_a[l][:, None, :],
                    _block_diag(lru_w_x[l]).astype(BF16), lru_b_x[l][:, None, :], lru_lambda[l][:, None, :]),
            "gla": (gla_w_up[l], gla_w0[l][:, None, :]),
            "ml_bias": jnp.zeros((1, 128), F32).at[0, 0:8].set(ml_i_b[l].reshape(-1)).at[0, 8:16].set(ml_f_b[l].reshape(-1)),
            "branch": (row(rw_ln_w[l]), row(rw_r_k[l]), rw_g_up[l].astype(BF16), row(gla_ln_w[l]), row(ml_ln_w[l]),
                       e64, e128),
        }
        gw = gate_w[l].astype(BF16)
        gb = gate_b[l][:, None, :]
        bw = br_w[l].astype(BF16)
        ow = out_w[l].astype(BF16)
        w1 = ffn_w1[l].astype(BF16)
        w2 = ffn_w2[l].astype(BF16)

        ys_c, st_c = _mixers(hc, n_c, lw, _zero_states(n_c), latent=False)
        ys_x, _ = _mixers(hx, n_b, lw, st_c, latent=True)
        xs, hn = _proj_residual(_gated_sum(hx, ys_x, gw, gb, bw), ow, xs, mod_x[l], norm_ffn_w[l])
        if last:
            out, = _ffn(hn, w1, w2, xs, mod_x[l], final_norm_w, mod_x[l], final=True)
        else:
            xs, hx = _ffn(hn, w1, w2, xs, mod_x[l], norm_mix_w[l + 1], mod_x[l + 1], final=False)
            cs, hn = _proj_residual(_gated_sum(hc, ys_c, gw, gb, bw), ow, cs, mod_c[l], norm_ffn_w[l])
            cs, hc = _ffn(hn, w1, w2, cs, mod_c[l], norm_mix_w[l + 1], mod_c[l + 1], final=False)
    return out.reshape(n_b, seq, D)
```

```python
import functools

import numpy as np
import jax
import jax.numpy as jnp
from jax import lax
from jax.experimental import pallas as pl
from jax.experimental.pallas import tpu as pltpu

F32 = jnp.float32
BF16 = jnp.bfloat16
HI = lax.Precision.HIGHEST

D = 2048
DEPTH = 2
GRID_W = 64
EPS = 1e-6
NEG = -1e30
CH = 64
D_FF = 4 * D
W4 = D // 4

RW_HEAD = 64
RW_HEADS = W4 // RW_HEAD
RW_DECAY_SCALE = 0.6065306597
LRU_BLOCKS = 8
LRU_C = 8.0
GLA_HEADS = 4
GLA_DK = 64
GLA_DV = 128
GLA_RANK = 16
GLA_TAU = 16.0
ML_HEADS = 4
ML_DH = 128

RW_COLS = 3 * W4 + 3 * 128
LRU_COLS = 2 * W4
GLA_COLS = 256 + 256 + 512 + 512 + 128
ML_COLS = 4 * W4 + 128

VMEM_LIMIT = 56 * 1024 * 1024


def _cparams(*sem):
    return pltpu.CompilerParams(dimension_semantics=sem, vmem_limit_bytes=VMEM_LIMIT)


def _dot(a, b, prec=None):
    return jnp.dot(a, b, precision=prec, preferred_element_type=F32)


def _dot_nt(a, b, prec=None):
    return lax.dot_general(a, b, (((1,), (1,)), ((), ())), precision=prec, preferred_element_type=F32)


def _dot_tn(a, b, prec=None):
    return lax.dot_general(a, b, (((0,), (0,)), ((), ())), precision=prec, preferred_element_type=F32)


def _b(x):
    return x.astype(BF16)


def _bdot(a, b):
    return jnp.dot(_b(a), _b(b), preferred_element_type=F32)


def _bdot_nt(a, b):
    return _dot_nt(_b(a), _b(b))


def _bdot_tn(a, b):
    return _dot_tn(_b(a), _b(b))


def _split_bf16(x, parts):
    out = []
    for i in range(parts):
        hi = _b(x)
        out.append(hi)
        if i < parts - 1:
            x = x - hi.astype(F32)
    return out


def _cumdot(mask, x):
    m = jnp.where(mask, 1.0, 0.0).astype(BF16)
    parts = _split_bf16(x, 3)
    acc = _dot(m, parts[0])
    for p in parts[1:]:
        acc = acc + _dot(m, p)
    return acc


def _headsum(x, e):
    hi, lo = _split_bf16(x, 2)
    return _dot(hi, e) + _dot(lo, e)


def _stack_heads(x):
    lo = lax.broadcasted_iota(jnp.int32, x.shape, 1) < 64
    return jnp.concatenate([jnp.where(lo, x, 0.0), jnp.where(lo, 0.0, x)], axis=0)


def _pair_scan_mask(inclusive, reverse):
    r = lax.broadcasted_iota(jnp.int32, (2 * CH, 2 * CH), 0) & (CH - 1)
    c = lax.broadcasted_iota(jnp.int32, (2 * CH, 2 * CH), 1) & (CH - 1)
    if reverse:
        return (r <= c) if inclusive else (r < c)
    return (r >= c) if inclusive else (r > c)


def _for_chunks(nchunk, body):
    for ci in range(nchunk):
        body(slice(ci * CH, (ci + 1) * CH), slice((nchunk - 1 - ci) * CH, (nchunk - ci) * CH))


def _sigmoid(x):
    return 1.0 / (1.0 + jnp.exp(-x))


def _log_sigmoid(x):
    return jnp.minimum(x, 0.0) - jnp.log1p(jnp.exp(-jnp.abs(x)))


def _softplus(x):
    return jnp.maximum(x, 0.0) + jnp.log1p(jnp.exp(-jnp.abs(x)))


def _scan_mask(n, inclusive, reverse):
    r = lax.broadcasted_iota(jnp.int32, (n, n), 0)
    c = lax.broadcasted_iota(jnp.int32, (n, n), 1)
    if reverse:
        return (r <= c) if inclusive else (r < c)
    return (r >= c) if inclusive else (r > c)


def _ada_kernel(s_ref, w_ref, b_ref, o_ref):
    s = s_ref[...]
    s = s * _sigmoid(s)
    o_ref[...] = _dot(s, w_ref[...], HI) + b_ref[...]


def _ada(cc, w, b):
    tn = 1024
    n_l = w.shape[0]
    return pl.pallas_call(
        _ada_kernel,
        grid=(n_l, 6 * D // tn),
        in_specs=[pl.BlockSpec((8, D), lambda l, j: (0, 0)),
                  pl.BlockSpec((None, D, tn), lambda l, j: (l, 0, j)),
                  pl.BlockSpec((None, 1, tn), lambda l, j: (l, 0, j))],
        out_specs=pl.BlockSpec((None, 8, tn), lambda l, j: (l, 0, j)),
        out_shape=jax.ShapeDtypeStruct((n_l, 8, 6 * D), F32),
        compiler_params=_cparams("parallel", "parallel"),
        name="ada",
    )(cc, w, b.reshape(n_l, 1, 6 * D))


def _rms_mod(x, w, m, rows):
    y = x * lax.rsqrt(jnp.mean(x * x, axis=-1, keepdims=True) + EPS) * w
    if rows is not None:
        shift, scale = rows
        y = y * (1.0 + m[scale:scale + 1]) + m[shift:shift + 1]
    return y


def _norm_kernel(x_ref, w_ref, mod_ref, o_ref, *, rows):
    o_ref[...] = _rms_mod(x_ref[...], w_ref[...], mod_ref[...], rows).astype(o_ref.dtype)


def _norm(x, w, mods, rows, out_dtype, tm=256):
    r = x.shape[0]
    g = mods.shape[0]
    per = r // g // tm
    return pl.pallas_call(
        functools.partial(_norm_kernel, rows=rows),
        grid=(r // tm,),
        in_specs=[pl.BlockSpec((tm, D), lambda i: (i, 0)),
                  pl.BlockSpec((1, D), lambda i: (0, 0)),
                  pl.BlockSpec((None, 8, D), lambda i: (i // per, 0, 0))],
        out_specs=pl.BlockSpec((tm, D), lambda i: (i, 0)),
        out_shape=jax.ShapeDtypeStruct((r, D), out_dtype),
        compiler_params=_cparams("parallel"),
        name="norm",
    )(x, w.reshape(1, D), mods)


def _mm_kernel(a_ref, w_ref, o_ref):
    o_ref[...] = _dot(a_ref[...], w_ref[...])


def _mm(a, w, tm=1024):
    r, k = a.shape
    n = w.shape[1]
    tm = min(tm, r)
    return pl.pallas_call(
        _mm_kernel,
        grid=(r // tm,),
        in_specs=[pl.BlockSpec((tm, k), lambda i: (i, 0)),
                  pl.BlockSpec((k, n), lambda i: (0, 0), pipeline_mode=pl.Buffered(1))],
        out_specs=pl.BlockSpec((tm, n), lambda i: (i, 0)),
        out_shape=jax.ShapeDtypeStruct((r, n), F32),
        compiler_params=_cparams("parallel"),
        name="in_proj",
    )(a, w)


def _gate_kernel(hx_ref, ys_ref, gw_ref, gb_ref, bw_ref, o_ref):
    hx = hx_ref[...]
    acc = None
    for i in range(4):
        g = _dot(hx, gw_ref[i]) + gb_ref[i]
        p = _dot(ys_ref[:, i * W4:(i + 1) * W4], bw_ref[i])
        t = _sigmoid(g) * p
        acc = t if acc is None else acc + t
    o_ref[...] = acc.astype(o_ref.dtype)


def _gated_sum(hx, ys, gw, gb, bw, tm=512, tn=512):
    r = hx.shape[0]
    tm = min(tm, r)
    return pl.pallas_call(
        _gate_kernel,
        grid=(D // tn, r // tm),
        in_specs=[pl.BlockSpec((tm, D), lambda j, i: (i, 0)),
                  pl.BlockSpec((tm, D), lambda j, i: (i, 0)),
                  pl.BlockSpec((4, D, tn), lambda j, i: (0, 0, j)),
                  pl.BlockSpec((4, 1, tn), lambda j, i: (0, 0, j)),
                  pl.BlockSpec((4, W4, tn), lambda j, i: (0, 0, j))],
        out_specs=pl.BlockSpec((tm, tn), lambda j, i: (i, j)),
        out_shape=jax.ShapeDtypeStruct((r, D), BF16),
        compiler_params=_cparams("parallel", "parallel"),
        name="gated_sum",
    )(hx, ys, gw, gb, bw)


def _proj_res_kernel(a_ref, w_ref, x_ref, mod_ref, nw_ref, o_ref, h_ref):
    m = mod_ref[...]
    x = x_ref[...] + m[2:3] * _dot(a_ref[...], w_ref[...])
    o_ref[...] = x
    h_ref[...] = _rms_mod(x, nw_ref[...], m, (3, 4)).astype(h_ref.dtype)


def _proj_residual(a, w, x, mods, norm_w, tm=512):
    r = a.shape[0]
    tm = min(tm, r)
    per = r // mods.shape[0] // tm
    row = pl.BlockSpec((tm, D), lambda i: (i, 0))
    return pl.pallas_call(
        _proj_res_kernel,
        grid=(r // tm,),
        in_specs=[row, pl.BlockSpec((D, D), lambda i: (0, 0)), row,
                  pl.BlockSpec((None, 8, D), lambda i: (i // per, 0, 0)),
                  pl.BlockSpec((1, D), lambda i: (0, 0))],
        out_specs=[row, row],
        out_shape=[jax.ShapeDtypeStruct((r, D), F32), jax.ShapeDtypeStruct((r, D), BF16)],
        compiler_params=_cparams("parallel"),
        name="out_proj",
    )(a, w, x, mods, norm_w.reshape(1, D))


def _ffn_kernel(h_ref, w1_ref, w2_ref, x_ref, mod_ref, nw_ref, nmod_ref, *refs, final):
    o_refs, acc_ref = refs[:-1], refs[-1]
    f = pl.program_id(1)

    @pl.when(f == 0)
    def _():
        acc_ref[...] = jnp.zeros_like(acc_ref)

    a = jnp.maximum(_dot(h_ref[...], w1_ref[...]), 0.0)
    acc_ref[...] += _dot((a * a).astype(BF16), w2_ref[...])

    @pl.when(f == pl.num_programs(1) - 1)
    def _():
        x = x_ref[...] + mod_ref[...][5:6] * acc_ref[...]
        if final:
            o_refs[0][...] = _rms_mod(x, nw_ref[...], None, None)
        else:
            o_refs[0][...] = x
            o_refs[1][...] = _rms_mod(x, nw_ref[...], nmod_ref[...], (0, 1)).astype(BF16)


def _ffn(hn, w1, w2, x, mods, next_norm_w, next_mods, final, tm=512, tf=1024):
    r = hn.shape[0]
    tm = min(tm, r)
    per = r // mods.shape[0] // tm
    row = pl.BlockSpec((tm, D), lambda i, f: (i, 0))
    mod_spec = pl.BlockSpec((None, 8, D), lambda i, f: (i // per, 0, 0))
    out_specs = [row] if final else [row, row]
    out_shape = [jax.ShapeDtypeStruct((r, D), F32)] + ([] if final else [jax.ShapeDtypeStruct((r, D), BF16)])
    return pl.pallas_call(
        functools.partial(_ffn_kernel, final=final),
        grid=(r // tm, D_FF // tf),
        in_specs=[row, pl.BlockSpec((D, tf), lambda i, f: (0, f)), pl.BlockSpec((tf, D), lambda i, f: (f, 0)),
                  row, mod_spec, pl.BlockSpec((1, D), lambda i, f: (0, 0)), mod_spec],
        out_specs=out_specs,
        out_shape=out_shape,
        scratch_shapes=[pltpu.VMEM((tm, D), F32)],
        compiler_params=_cparams("parallel", "arbitrary"),
        name="ffn",
    )(hn, w1, w2, x, mods, next_norm_w.reshape(1, D), next_mods)


def _rwkv_chunk(b, z, r, k, v, wd, ad, w_up, w0, a_up, a0, k_k, k_a):
    rev = z == 1
    last = 0 if rev else CH - 1
    lo = lax.broadcasted_iota(jnp.int32, (CH, 128), 1) < 64

    wdz = wd[:, 64 * z:64 * z + 64]
    adz = ad[:, 64 * z:64 * z + 64]
    lw = -RW_DECAY_SCALE * _sigmoid(_dot(jnp.tanh(wdz), w_up, HI) + w0)
    a = _sigmoid(_dot(adz, a_up, HI) + a0)
    kk = k * k_k
    kd = k * (1.0 + (a - 1.0) * k_a)

    lc_incl = _cumdot(_scan_mask(CH, True, rev), lw)
    p_incl = jnp.exp(lc_incl)
    p_inv = jnp.exp(-lc_incl)
    p_excl = jnp.exp(lc_incl - lw)
    p_end = p_incl[last:last + 1]

    chains = []
    for p in range(RW_HEADS // 2):
        sl = slice(128 * p, 128 * (p + 1))
        kkp = kk[:, sl]
        sq = kkp * kkp
        nrm = jnp.where(lo, jnp.sum(jnp.where(lo, sq, 0.0), axis=-1, keepdims=True),
                        jnp.sum(jnp.where(lo, 0.0, sq), axis=-1, keepdims=True))
        kkp = kkp / jnp.maximum(jnp.sqrt(nrm), 1e-12)
        at = -kkp * p_excl[:, sl]
        bt = kkp * a[:, sl] * p_inv[:, sl]
        kt = kd[:, sl] * p_inv[:, sl]
        rt = r[:, sl] * p_incl[:, sl]
        chains.append(dict(
            z=z, key=(b, z), idx=(b, z, p), p_end=p_end[:, sl],
            ar=_b(jnp.concatenate([at, rt], axis=0)),
            bk=_b(jnp.concatenate([_stack_heads(bt), _stack_heads(kt)], axis=0)),
            vs=_b(_stack_heads(v[:, sl]))))
    return chains


def _row_scan_mask(inclusive, reverse):
    r = lax.broadcasted_iota(jnp.int32, (CH, 2 * CH), 0)
    c = lax.broadcasted_iota(jnp.int32, (CH, 2 * CH), 1) & (CH - 1)
    if reverse:
        return (r <= c) if inclusive else (r < c)
    return (r >= c) if inclusive else (r > c)


def _rwkv_solve(chains, s_ref):
    n2 = 2 * CH
    masks = {z: (_row_scan_mask(True, z == 1), _row_scan_mask(False, z == 1)) for z in (0, 1)}
    stack = lambda x: _b(_stack_heads(x))
    s0 = [s_ref[c["idx"]] for c in chains]
    g = [_dot_nt(c["ar"], c["bk"]) for c in chains]
    s0t = [_dot_nt(c["ar"], _b(s)) for c, s in zip(chains, s0)]
    yield
    lp = [jnp.where(masks[c["z"]][1], gi[:CH, :n2], 0.0) for c, gi in zip(chains, g)]
    u = [st[:CH] + _dot(_b(jnp.where(masks[c["z"]][1], gi[:CH, n2:], 0.0)), c["vs"])
         for c, gi, st in zip(chains, g, s0t)]
    for j in range(6):
        lb = [_b(li) for li in lp]
        u = [ui + _dot(li, stack(ui)) for ui, li in zip(u, lb)]
        if j < 5:
            lp = [_dot(li, stack(lf)) for li, lf in zip(lb, lp)]
        yield
    ys = {}
    for c, gi, st, ui, s in zip(chains, g, s0t, u, s0):
        m_incl = masks[c["z"]][0]
        uv = jnp.concatenate([stack(ui), c["vs"]], axis=0)
        rbk = _b(jnp.concatenate([jnp.where(m_incl, gi[CH:, :n2], 0.0), jnp.where(m_incl, gi[CH:, n2:], 0.0)], axis=1))
        ys.setdefault(c["key"], []).append(st[CH:] + _dot(rbk, uv))
        s_ref[c["idx"]] = (s + _dot_tn(uv, c["bk"])) * c["p_end"]
    return {key: jnp.concatenate(parts, axis=1) for key, parts in ys.items()}


def _seq_block_specs(n_b, tb, nblk, cols):
    fwd = [pl.BlockSpec((n_b, tb, w), functools.partial(lambda j, c: (0, j, c), c=c)) for w, c in cols]
    bwd = [pl.BlockSpec((n_b, tb, w), functools.partial(lambda j, c: (0, nblk - 1 - j, c), c=c)) for w, c in cols]
    return fwd, bwd


def _run_stages(gens):
    out = [None] * len(gens)
    live = list(range(len(gens)))
    while live:
        for i in list(live):
            try:
                next(gens[i])
            except StopIteration as done:
                out[i] = done.value
                live.remove(i)
    return out


def _lru_kernel(xf, xb, tail, head, conv_w, conv_b, wa, ba, wx, bx, lam, h0_ref,
                hf_ref, hb_ref, h_ref, ext, a_s, b_s, *, n_b, t, wpb, nscan, wmajor):
    j = pl.program_id(0)
    nstep = nscan // wpb

    def rows(ref, b, w, sl):
        return ref[b, w, sl, :] if wmajor else ref[b, sl, w, :]

    def put(ref, b, w, sl, val):
        if wmajor:
            ref[b, w, sl, :] = val
        else:
            ref[b, sl, w, :] = val

    @pl.when(j == 0)
    def _():
        h_ref[...] = h0_ref[...]

    for wi in range(wpb):
        local = (wi, wpb - 1 - wi)
        glob = (j * wpb + wi, (nstep - 1 - j) * wpb + wpb - 1 - wi)
        for z, xr in enumerate((xf, xb)):
            w, g = local[z], glob[z]
            for b in range(n_b):
                if w > 0:
                    prev2 = rows(xr, b, w - 1, slice(t - 2, t))
                else:
                    prev2 = jnp.where(g > 0, tail[b, jnp.maximum(g - 1, 0)], 0.0)
                if w < wpb - 1:
                    next1 = rows(xr, b, w + 1, slice(0, 1))
                else:
                    next1 = jnp.where(g < nscan - 1, head[b, pl.ds(jnp.minimum(g + 1, nscan - 1), 1), :], 0.0)
                ext[b, z, 6:8, :] = prev2
                ext[b, z, 8:8 + t, :] = rows(xr, b, w, slice(None))
                ext[b, z, 8 + t:9 + t, :] = next1
                xc = conv_b[...] + ext[b, z, 6:6 + t, :] * conv_w[0:1, :]
                for tap in range(1, 4):
                    xc = xc + ext[b, z, 6 + tap:6 + tap + t, :] * conv_w[tap:tap + 1, :]
                xcb = _b(xc)
                rg = _sigmoid(_dot(xcb, wa[z]) + ba[z])
                ig = _sigmoid(_dot(xcb, wx[z]) + bx[z])
                log_a = -LRU_C * rg * _softplus(-lam[z])
                a = jnp.exp(log_a)
                a_s[b, z] = a
                b_s[b, z] = jnp.sqrt(-jnp.tanh(log_a) * (a * a + 1.0)) * ig * xc

        def step(i, carry):
            rf = pl.ds(i, 1)
            rb = pl.ds(t - 1 - i, 1)
            out = []
            for b in range(n_b):
                h_f = a_s[b, 0, rf, :] * carry[2 * b] + b_s[b, 0, rf, :]
                h_b = a_s[b, 1, rb, :] * carry[2 * b + 1] + b_s[b, 1, rb, :]
                put(hf_ref, b, local[0], rf, h_f)
                put(hb_ref, b, local[1], rb, h_b)
                out += [h_f, h_b]
            return tuple(out)

        init = tuple(h_ref[b, z:z + 1, :] for b in range(n_b) for z in (0, 1))
        fin = lax.fori_loop(0, t, step, init, unroll=8)
        for b in range(n_b):
            h_ref[b, 0:1, :] = fin[2 * b]
            h_ref[b, 1:2, :] = fin[2 * b + 1]


def _lru(x4, tail, head, lru_w, h0, wpb, wmajor):
    n_b = x4.shape[0]
    nscan, t = (x4.shape[1], x4.shape[2]) if wmajor else (x4.shape[2], x4.shape[1])
    nstep = nscan // wpb
    shp = (n_b, wpb, t, W4) if wmajor else (n_b, t, wpb, W4)
    at = (lambda j: (0, j, 0, 0)) if wmajor else (lambda j: (0, 0, j, 0))
    at_rev = (lambda j: (0, nstep - 1 - j, 0, 0)) if wmajor else (lambda j: (0, 0, nstep - 1 - j, 0))
    full = lambda s: pl.BlockSpec(s, lambda j: (0,) * len(s))
    out_sds = jax.ShapeDtypeStruct(x4.shape[:3] + (W4,), F32)
    return pl.pallas_call(
        functools.partial(_lru_kernel, n_b=n_b, t=t, wpb=wpb, nscan=nscan, wmajor=wmajor),
        grid=(nstep,),
        in_specs=[pl.BlockSpec(shp, at), pl.BlockSpec(shp, at_rev), full(tail.shape), full(head.shape),
                  full((4, W4)), full((1, W4)), full((2, W4, W4)), full((2, 1, W4)),
                  full((2, W4, W4)), full((2, 1, W4)), full((2, 1, W4)), full(h0.shape)],
        out_specs=[pl.BlockSpec(shp, at), pl.BlockSpec(shp, at_rev), full(h0.shape)],
        out_shape=[out_sds, out_sds, jax.ShapeDtypeStruct(h0.shape, F32)],
        scratch_shapes=[pltpu.VMEM((n_b, 2, t + 16, W4), F32), pltpu.VMEM((n_b, 2, t, W4), F32),
                        pltpu.VMEM((n_b, 2, t, W4), F32)],
        compiler_params=_cparams("arbitrary"),
        name="rglru",
    )(x4, x4, tail, head, *lru_w, h0)


def _gla_chunk(b, z, q, k, v, wd, w_up, w0):
    rev = z == 1
    last = 0 if rev else CH - 1
    pm_incl = _pair_scan_mask(True, rev)
    la = _log_sigmoid(_dot(wd[:, GLA_RANK * z:GLA_RANK * (z + 1)], w_up, HI) + w0) * (1.0 / GLA_TAU)
    bcum = _cumdot(_scan_mask(CH, True, rev), la)
    b_end = bcum[last:last + 1]
    qe = q * (GLA_DK ** -0.5) * jnp.exp(bcum)
    ke = k * jnp.exp(-bcum)
    kd = k * jnp.exp(b_end - bcum)
    e_end = jnp.exp(b_end)
    chains = []
    for p in range(GLA_HEADS // 2):
        sl = slice(128 * p, 128 * (p + 1))
        chains.append(dict(
            key=(b, z), idx=(b, z, p), mask=pm_incl, e_end=e_end[:, sl],
            qs=_b(_stack_heads(qe[:, sl])),
            ks=_b(_stack_heads(ke[:, sl])),
            kds=_b(_stack_heads(kd[:, sl])),
            vs=_b(jnp.concatenate([v[:, 2 * GLA_DV * p:2 * GLA_DV * p + GLA_DV],
                                   v[:, 2 * GLA_DV * p + GLA_DV:2 * GLA_DV * (p + 1)]], axis=0))))
    return chains


def _gla_solve(chains, s_ref):
    st = [s_ref[c["idx"]] for c in chains]
    att = [_b(jnp.where(c["mask"], _dot_nt(c["qs"], c["ks"]), 0.0)) for c in chains]
    inter = [_dot_nt(c["qs"], _b(s)) for c, s in zip(chains, st)]
    yield
    outs = {}
    for c, a, i, s in zip(chains, att, inter, st):
        o = _dot(a, c["vs"]) + i
        outs.setdefault(c["key"], []).extend([o[:CH], o[CH:]])
        s_ref[c["idx"]] = s * c["e_end"] + _dot_tn(c["vs"], c["kds"])
    return {key: jnp.concatenate(parts, axis=1) for key, parts in outs.items()}


def _mlstm_chunk(b, z, q, k, v, gates, bias):
    rev = z == 1
    m_incl = _scan_mask(CH, True, rev)
    last = 0 if rev else CH - 1
    pre = gates + bias
    lf = _log_sigmoid(pre)
    fcum = _cumdot(m_incl, lf)
    pre_t = pre.T
    fcum_t = fcum.T
    k = k * (ML_DH ** -0.5)
    chains = []
    for h in range(ML_HEADS):
        ci = ML_HEADS * z + h
        cf = 2 * ML_HEADS + ML_HEADS * z + h
        sl = slice(ML_DH * h, ML_DH * (h + 1))
        fc = fcum[:, cf:cf + 1]
        ic = pre[:, ci:ci + 1]
        fr = fcum_t[cf:cf + 1, :]
        ir = pre_t[ci:ci + 1, :]
        chains.append(dict(key=(b, z), idx=(b, z, h), q=q[:, sl], k=k[:, sl], v=v[:, sl],
                           qb=_b(q[:, sl]), kb=_b(k[:, sl]),
                           fc=fc, ic=ic, f_end=fc[last:last + 1],
                           dlog=jnp.where(m_incl, fc + (ir - fr), NEG)))
    return chains


def _mlstm_solve(chains, cm_ref, n_ref, m_ref):
    cm = [cm_ref[c["idx"]] for c in chains]
    n = [n_ref[c["idx"]] for c in chains]
    m = [m_ref[c["idx"]] for c in chains]
    qk = [_dot_nt(c["qb"], c["kb"]) for c in chains]
    qc = [_dot_nt(c["qb"], _b(s)) for c, s in zip(chains, cm)]
    yield
    inter = [c["fc"] + mi for c, mi in zip(chains, m)]
    mt = [jnp.maximum(i, jnp.max(c["dlog"], axis=-1, keepdims=True)) for c, i in zip(chains, inter)]
    w = [jnp.exp(c["dlog"] - t) * s for c, t, s in zip(chains, mt, qk)]
    ei = [jnp.exp(i - t) for i, t in zip(inter, mt)]
    yield
    wv = [_dot(_b(wi), _b(c["v"])) for c, wi in zip(chains, w)]
    yield
    outs = {}
    for c, wi, e, t, a, qci, cmi, ni, mi in zip(chains, w, ei, mt, wv, qc, cm, n, m):
        num = a + e * qci
        den = jnp.sum(wi, axis=-1, keepdims=True) + e * jnp.sum(c["q"] * ni, axis=-1, keepdims=True)
        outs.setdefault(c["key"], []).append(num / jnp.maximum(jnp.abs(den), jnp.exp(-t)))
        g = c["f_end"] - c["fc"] + c["ic"]
        m_new = jnp.maximum(c["f_end"] + mi, jnp.max(g, axis=0, keepdims=True))
        dec = jnp.exp(c["f_end"] + mi - m_new)
        wg = jnp.exp(g - m_new)
        cm_ref[c["idx"]] = dec * cmi + _dot_tn(_b(c["v"] * wg), c["kb"])
        n_ref[c["idx"]] = dec * ni + jnp.sum(c["k"] * wg, axis=0, keepdims=True)
        m_ref[c["idx"]] = m_new
    return {key: jnp.concatenate(parts, axis=1) for key, parts in outs.items()}


_RW_SEQ = [(W4, 0), (W4, 1), (W4, 2), (128, 12), (128, 13)]
_GLA_SEQ = [(256, 0), (256, 1), (512, 1), (128, 12)]
_ML_SEQ = [(W4, 0), (W4, 1), (W4, 2), (128, 16)]


def _mix_kernel(*refs, n_b, nchunk):
    it = iter(refs)
    take = lambda n: [next(it) for _ in range(n)]
    rw_seq = (take(len(_RW_SEQ)), take(len(_RW_SEQ)))
    gla_seq = (take(len(_GLA_SEQ)), take(len(_GLA_SEQ)))
    ml_seq = (take(len(_ML_SEQ)), take(len(_ML_SEQ)))
    rw_par, gla_par, (ml_bias,) = take(6), take(2), take(1)
    init = take(5)
    outs = (take(2), take(2), take(2))
    state = take(5)
    rw_s, gla_s, cm_ref, n_ref, m_ref = state

    @pl.when(pl.program_id(0) == 0)
    def _():
        for dst, src in zip(state, init):
            dst[...] = src[...]

    def body(rows_f, rows_b):
        rw, gla, ml = [], [], []
        for b in range(n_b):
            for z, rows in enumerate((rows_f, rows_b)):
                w_up, w0, a_up, a0, k_k, k_a = rw_par
                rw += _rwkv_chunk(b, z, *(ref[b, rows, :] for ref in rw_seq[z]),
                                  w_up[z], w0[z], a_up[z], a0[z], k_k[...], k_a[...])
                gla += _gla_chunk(b, z, *(ref[b, rows, :] for ref in gla_seq[z]), gla_par[0][z], gla_par[1][z])
                ml += _mlstm_chunk(b, z, *(ref[b, rows, :] for ref in ml_seq[z]), ml_bias[...])
        results = _run_stages([_rwkv_solve(rw, rw_s), _mlstm_solve(ml, cm_ref, n_ref, m_ref), _gla_solve(gla, gla_s)])
        for (o_f, o_b), res in zip((outs[0], outs[2], outs[1]), results):
            for b in range(n_b):
                o_f[b, rows_f, :] = res[(b, 0)]
                o_b[b, rows_b, :] = res[(b, 1)]

    _for_chunks(nchunk, body)


def _mix(u_rw, u_gla, u_ml, rw_par, gla_par, ml_bias, states, tb=128):
    n_b, length, _ = u_rw.shape
    nblk = length // tb
    full = lambda a: pl.BlockSpec(a.shape, lambda j: (0,) * a.ndim)
    seq_specs, seq_args = [], []
    for u, cols in ((u_rw, _RW_SEQ), (u_gla, _GLA_SEQ), (u_ml, _ML_SEQ)):
        fwd, bwd = _seq_block_specs(n_b, tb, nblk, cols)
        seq_specs += fwd + bwd
        seq_args += [u] * (2 * len(cols))
    params = list(rw_par) + list(gla_par) + [ml_bias]
    out_f, out_b = _seq_block_specs(n_b, tb, nblk, [(W4, 0)])
    out_sds = jax.ShapeDtypeStruct((n_b, length, W4), F32)
    res = pl.pallas_call(
        functools.partial(_mix_kernel, n_b=n_b, nchunk=tb // CH),
        grid=(nblk,),
        in_specs=seq_specs + [full(a) for a in params + list(states)],
        out_specs=[out_f[0], out_b[0]] * 3 + [full(a) for a in states],
        out_shape=[out_sds] * 6 + [jax.ShapeDtypeStruct(a.shape, F32) for a in states],
        compiler_params=_cparams("arbitrary"),
        name="mix3",
    )(*seq_args, *params, *states)
    return res[0:2], res[2:4], res[4:6], res[6:]


def _branch_kernel(yaf, yab, rr, rk, rv, rg, hbf, hbb, lg, ocf, ocb, gr, odf, odb, mo,
                   rw_ln, rw_rk, g_up, gla_ln, ml_ln, e64, e128, o_ref):
    y = yaf[...] + yab[...]
    y = y * lax.rsqrt(_headsum(y * y, e64[...]) * (1.0 / RW_HEAD) + EPS) * rw_ln[...]
    bonus = _headsum(rr[...] * rk[...] * rw_rk[...], e64[...]) * rv[...]
    g = _dot(_sigmoid(rg[...]).astype(BF16), g_up[...])
    o_ref[:, 0:W4] = ((y + bonus) * g).astype(o_ref.dtype)
    o_ref[:, W4:2 * W4] = ((hbf[...] + hbb[...]) * jax.nn.gelu(lg[...])).astype(o_ref.dtype)
    y = ocf[...] + ocb[...]
    y = y * lax.rsqrt(_headsum(y * y, e128[...]) * (1.0 / GLA_DV) + EPS) * gla_ln[...]
    r = gr[...]
    o_ref[:, 2 * W4:3 * W4] = (y * (r * _sigmoid(r))).astype(o_ref.dtype)
    y = odf[...] + odb[...]
    y = y * lax.rsqrt(_headsum(y * y, e128[...]) * (1.0 / ML_DH) + EPS) * ml_ln[...]
    o_ref[:, 3 * W4:4 * W4] = (y * _sigmoid(mo[...])).astype(o_ref.dtype)


def _branches(u_rw, u_lru, u_gla, u_ml, ya, hb, oc, od, rw_ln, rw_rk, g_up, gla_ln, ml_ln, e64, e128, tm=256):
    r = u_rw.shape[0]
    row = lambda w, c: pl.BlockSpec((tm, w), lambda i: (i, c))
    full = lambda shp: pl.BlockSpec(shp, lambda i: (0,) * len(shp))
    return pl.pallas_call(
        _branch_kernel,
        grid=(r // tm,),
        in_specs=[row(W4, 0), row(W4, 0), row(W4, 0), row(W4, 1), row(W4, 2), row(128, 14),
                  row(W4, 0), row(W4, 0), row(W4, 1),
                  row(W4, 0), row(W4, 0), row(W4, 2),
                  row(W4, 0), row(W4, 0), row(W4, 3),
                  full((1, W4)), full((1, W4)), full((128, W4)), full((1, W4)), full((1, W4)),
                  full((W4, W4)), full((W4, W4))],
        out_specs=pl.BlockSpec((tm, D), lambda i: (i, 0)),
        out_shape=jax.ShapeDtypeStruct((r, D), BF16),
        compiler_params=_cparams("parallel"),
        name="branches",
    )(ya[0], ya[1], u_rw, u_rw, u_rw, u_rw, hb[0], hb[1], u_lru, oc[0], oc[1], u_gla, od[0], od[1], u_ml,
      rw_ln, rw_rk, g_up, gla_ln, ml_ln, e64, e128)


def _block_ones(width, head):
    i = np.arange(width) // head
    return jnp.asarray((i[:, None] == i[None, :]).astype(np.float32)).astype(BF16)


def _block_diag(w):
    z, nb, c, _ = w.shape
    eye = jnp.eye(nb, dtype=w.dtype)
    return jnp.einsum("znij,nm->znimj", w, eye).reshape(z, nb * c, nb * c)


def _regroup_w_in(w):
    sizes = [512, 512, 512, 128, 128, 128, 512, 512, 256, 256, 512, 512, 32, 512, 512, 512, 512, 8, 8]
    offs = np.concatenate([[0], np.cumsum(sizes)])
    part = [w[:, offs[i]:offs[i + 1]] for i in range(len(sizes))]
    pad = lambda n: jnp.zeros((w.shape[0], n), w.dtype)
    rw = jnp.concatenate(part[0:6], axis=1)
    lru = jnp.concatenate(part[6:8], axis=1)
    gla = jnp.concatenate(part[8:13] + [pad(96)], axis=1)
    ml = jnp.concatenate(part[13:19] + [pad(112)], axis=1)
    return [t.astype(BF16) for t in (rw, lru, gla, ml)]


def _mixers(hx, n_b, lw, states, latent):
    r = hx.shape[0]
    length = r // n_b
    u_rw, u_lru, u_gla, u_ml = (_mm(hx, w) for w in lw["w_in"])
    seq = lambda u: u.reshape(n_b, length, u.shape[-1])
    flat = lambda u: u.reshape(r, u.shape[-1])
    s_rw, s_lru, s_gla, s_ml = states

    (ya_f, ya_b), (oc_f, oc_b), (od_f, od_b), (s_rw, s_gla, *s_ml) = _mix(
        seq(u_rw), seq(u_gla), seq(u_ml), lw["rw"], lw["gla"], lw["ml_bias"], [s_rw, s_gla, *s_ml])
    if latent:
        rows = length // GRID_W
        x4 = u_lru.reshape(n_b, rows, GRID_W, LRU_COLS)
        tail = jnp.swapaxes(x4[:, rows - 2:rows, :, :W4], 1, 2)
        head = x4[:, 0, :, :W4]
        hb_f, hb_b, s_lru = _lru(x4, tail, head, lw["lru"], s_lru, wpb=8, wmajor=False)
    else:
        x4 = u_lru.reshape(n_b, 1, length, LRU_COLS)
        hb_f, hb_b, s_lru = _lru(x4, jnp.zeros((n_b, 1, 2, W4), F32), jnp.zeros((n_b, 1, W4), F32), lw["lru"], s_lru,
                                 wpb=1, wmajor=True)
    ys = _branches(u_rw, u_lru, u_gla, u_ml, (flat(ya_f), flat(ya_b)), (flat(hb_f), flat(hb_b)),
                   (flat(oc_f), flat(oc_b)), (flat(od_f), flat(od_b)), *lw["branch"])
    return ys, (s_rw, s_lru, s_gla, tuple(s_ml))


def _zero_states(n_b):
    return (jnp.zeros((n_b, 2, RW_HEADS // 2, 128, 128), F32),
            jnp.zeros((n_b, 2, W4), F32),
            jnp.zeros((n_b, 2, GLA_HEADS // 2, GLA_DV, 2 * GLA_DK), F32),
            (jnp.zeros((n_b, 2, ML_HEADS, ML_DH, ML_DH), F32),
             jnp.zeros((n_b, 2, ML_HEADS, 1, ML_DH), F32),
             jnp.full((n_b, 2, ML_HEADS, 1, 1), NEG, F32)))


def kernel(x, c, ctx, c_ctx, ada_w, ada_b, norm_mix_w, w_in, rw_w_up, rw_w0, rw_a_up, rw_a0, rw_g_up, rw_k_k, rw_k_a, rw_r_k, rw_ln_w, lru_conv_w, lru_conv_b, lru_w_a, lru_b_a, lru_w_x, lru_b_x, lru_lambda, gla_w_up, gla_w0, gla_ln_w, ml_i_b, ml_f_b, ml_ln_w, br_w, gate_w, gate_b, out_w, norm_ffn_w, ffn_w1, ffn_w2, final_norm_w):
    n_b, seq, _ = x.shape
    n_c, ctx_len, _ = ctx.shape
    xs = x.reshape(n_b * seq, D)
    cs = ctx.reshape(n_c * ctx_len, D)
    cc = jnp.zeros((8, D), F32).at[0:n_b].set(c).at[n_b].set(c_ctx)
    e64 = _block_ones(W4, RW_HEAD)
    e128 = _block_ones(W4, GLA_DV)
    row = lambda v: v.reshape(1, -1)

    mods = jnp.pad(_ada(cc, ada_w, ada_b).reshape(DEPTH, 8, 6, D), ((0, 0), (0, 0), (0, 2), (0, 0)))
    mod_x = [mods[l, 0:n_b] for l in range(DEPTH)]
    mod_c = [mods[l, n_b:n_b + 1] for l in range(DEPTH)]
    hx = _norm(xs, norm_mix_w[0], mod_x[0], (0, 1), BF16)
    hc = _norm(cs, norm_mix_w[0], mod_c[0], (0, 1), BF16)

    for l in range(DEPTH):
        last = l == DEPTH - 1
        lw = {
            "w_in": _regroup_w_in(w_in[l]),
            "rw": (rw_w_up[l], rw_w0[l][:, None, :], rw_a_up[l], rw_a0[l][:, None, :],
                   row(rw_k_k[l]), row(rw_k_a[l])),
            "lru": (lru_conv_w[l], row(lru_conv_b[l]), _block_diag(lru_w_a[l]).astype(BF16), lru_b_a[l][:, None, :],
                    _block_diag(lru_w_x[l]).astype(BF16), lru_b_x[l][:, None, :], lru_lambda[l][:, None, :]),
            "gla": (gla_w_up[l], gla_w0[l][:, None, :]),
            "ml_bias": jnp.zeros((1, 128), F32).at[0, 0:8].set(ml_i_b[l].reshape(-1)).at[0, 8:16].set(ml_f_b[l].reshape(-1)),
            "branch": (row(rw_ln_w[l]), row(rw_r_k[l]), rw_g_up[l].astype(BF16), row(gla_ln_w[l]), row(ml_ln_w[l]),
                       e64, e128),
        }
        gw = gate_w[l].astype(BF16)
        gb = gate_b[l][:, None, :]
        bw = br_w[l].astype(BF16)
        ow = out_w[l].astype(BF16)
        w1 = ffn_w1[l].astype(BF16)
        w2 = ffn_w2[l].astype(BF16)

        ys_c, st_c = _mixers(hc, n_c, lw, _zero_states(n_c), latent=False)
        ys_x, _ = _mixers(hx, n_b, lw, st_c, latent=True)
        xs, hn = _proj_residual(_gated_sum(hx, ys_x, gw, gb, bw), ow, xs, mod_x[l], norm_ffn_w[l])
        if last:
            out, = _ffn(hn, w1, w2, xs, mod_x[l], final_norm_w, mod_x[l], final=True)
        else:
            xs, hx = _ffn(hn, w1, w2, xs, mod_x[l], norm_mix_w[l + 1], mod_x[l + 1], final=False)
            cs, hn = _proj_residual(_gated_sum(hc, ys_c, gw, gb, bw), ow, cs, mod_c[l], norm_ffn_w[l])
            cs, hc = _ffn(hn, w1, w2, cs, mod_c[l], norm_mix_w[l + 1], mod_c[l + 1], final=False)
    return out.reshape(n_b, seq, D)
```

```python
import functools

import numpy as np
import jax
import jax.numpy as jnp
from jax import lax
from jax.experimental import pallas as pl
from jax.experimental.pallas import tpu as pltpu

F32 = jnp.float32
BF16 = jnp.bfloat16
HI = lax.Precision.HIGHEST

D = 2048
DEPTH = 2
GRID_W = 64
EPS = 1e-6
NEG = -1e30
CH = 64
D_FF = 4 * D
W4 = D // 4

RW_HEAD = 64
RW_HEADS = W4 // RW_HEAD
RW_DECAY_SCALE = 0.6065306597
LRU_BLOCKS = 8
LRU_C = 8.0
GLA_HEADS = 4
GLA_DK = 64
GLA_DV = 128
GLA_RANK = 16
GLA_TAU = 16.0
ML_HEADS = 4
ML_DH = 128

RW_COLS = 3 * W4 + 3 * 128
LRU_COLS = 2 * W4
GLA_COLS = 256 + 256 + 512 + 512 + 128
ML_COLS = 4 * W4 + 128

VMEM_LIMIT = 56 * 1024 * 1024


def _cparams(*sem):
    return pltpu.CompilerParams(dimension_semantics=sem, vmem_limit_bytes=VMEM_LIMIT)


def _dot(a, b, prec=None):
    return jnp.dot(a, b, precision=prec, preferred_element_type=F32)


def _dot_nt(a, b, prec=None):
    return lax.dot_general(a, b, (((1,), (1,)), ((), ())), precision=prec, preferred_element_type=F32)


def _dot_tn(a, b, prec=None):
    return lax.dot_general(a, b, (((0,), (0,)), ((), ())), precision=prec, preferred_element_type=F32)


def _b(x):
    return x.astype(BF16)


def _bdot(a, b):
    return jnp.dot(_b(a), _b(b), preferred_element_type=F32)


def _bdot_nt(a, b):
    return _dot_nt(_b(a), _b(b))


def _bdot_tn(a, b):
    return _dot_tn(_b(a), _b(b))


def _split_bf16(x, parts):
    out = []
    for i in range(parts):
        hi = _b(x)
        out.append(hi)
        if i < parts - 1:
            x = x - hi.astype(F32)
    return out


def _cumdot(mask, x):
    m = jnp.where(mask, 1.0, 0.0).astype(BF16)
    parts = _split_bf16(x, 3)
    acc = _dot(m, parts[0])
    for p in parts[1:]:
        acc = acc + _dot(m, p)
    return acc


def _headsum(x, e):
    hi, lo = _split_bf16(x, 2)
    return _dot(hi, e) + _dot(lo, e)


def _stack_heads(x):
    lo = lax.broadcasted_iota(jnp.int32, x.shape, 1) < 64
    return jnp.concatenate([jnp.where(lo, x, 0.0), jnp.where(lo, 0.0, x)], axis=0)


def _pair_scan_mask(inclusive, reverse):
    r = lax.broadcasted_iota(jnp.int32, (2 * CH, 2 * CH), 0) & (CH - 1)
    c = lax.broadcasted_iota(jnp.int32, (2 * CH, 2 * CH), 1) & (CH - 1)
    if reverse:
        return (r <= c) if inclusive else (r < c)
    return (r >= c) if inclusive else (r > c)


def _for_chunks(nchunk, body):
    for ci in range(nchunk):
        body(slice(ci * CH, (ci + 1) * CH), slice((nchunk - 1 - ci) * CH, (nchunk - ci) * CH))


def _sigmoid(x):
    return 1.0 / (1.0 + jnp.exp(-x))


def _log_sigmoid(x):
    return jnp.minimum(x, 0.0) - jnp.log1p(jnp.exp(-jnp.abs(x)))


def _softplus(x):
    return jnp.maximum(x, 0.0) + jnp.log1p(jnp.exp(-jnp.abs(x)))


def _scan_mask(n, inclusive, reverse):
    r = lax.broadcasted_iota(jnp.int32, (n, n), 0)
    c = lax.broadcasted_iota(jnp.int32, (n, n), 1)
    if reverse:
        return (r <= c) if inclusive else (r < c)
    return (r >= c) if inclusive else (r > c)


def _ada_kernel(s_ref, w_ref, b_ref, o_ref):
    s = s_ref[...]
    s = s * _sigmoid(s)
    o_ref[...] = _dot(s, w_ref[...], HI) + b_ref[...]


def _ada(cc, w, b):
    tn = 1024
    n_l = w.shape[0]
    return pl.pallas_call(
        _ada_kernel,
        grid=(n_l, 6 * D // tn),
        in_specs=[pl.BlockSpec((8, D), lambda l, j: (0, 0)),
                  pl.BlockSpec((None, D, tn), lambda l, j: (l, 0, j)),
                  pl.BlockSpec((None, 1, tn), lambda l, j: (l, 0, j))],
        out_specs=pl.BlockSpec((None, 8, tn), lambda l, j: (l, 0, j)),
        out_shape=jax.ShapeDtypeStruct((n_l, 8, 6 * D), F32),
        compiler_params=_cparams("parallel", "parallel"),
        name="ada",
    )(cc, w, b.reshape(n_l, 1, 6 * D))


def _rms_mod(x, w, m, rows):
    y = x * lax.rsqrt(jnp.mean(x * x, axis=-1, keepdims=True) + EPS) * w
    if rows is not None:
        shift, scale = rows
        y = y * (1.0 + m[scale:scale + 1]) + m[shift:shift + 1]
    return y


def _norm_kernel(x_ref, w_ref, mod_ref, o_ref, *, rows):
    o_ref[...] = _rms_mod(x_ref[...], w_ref[...], mod_ref[...], rows).astype(o_ref.dtype)


def _norm(x, w, mods, rows, out_dtype, tm=256):
    r = x.shape[0]
    g = mods.shape[0]
    per = r // g // tm
    return pl.pallas_call(
        functools.partial(_norm_kernel, rows=rows),
        grid=(r // tm,),
        in_specs=[pl.BlockSpec((tm, D), lambda i: (i, 0)),
                  pl.BlockSpec((1, D), lambda i: (0, 0)),
                  pl.BlockSpec((None, 8, D), lambda i: (i // per, 0, 0))],
        out_specs=pl.BlockSpec((tm, D), lambda i: (i, 0)),
        out_shape=jax.ShapeDtypeStruct((r, D), out_dtype),
        compiler_params=_cparams("parallel"),
        name="norm",
    )(x, w.reshape(1, D), mods)


def _mm_kernel(a_ref, w_ref, *o_refs, n_main):
    y = _dot(a_ref[...], w_ref[...])
    if n_main:
        o_refs[0][...] = y[:, :n_main].astype(BF16)
    if n_main < y.shape[1]:
        o_refs[-1][...] = y[:, n_main:]


def _mm(a, w, n_main, tm=512):
    r, k = a.shape
    n = w.shape[1]
    tm = min(tm, r)
    widths = [(n_main, BF16)] * bool(n_main) + [(n - n_main, F32)] * bool(n - n_main)
    return pl.pallas_call(
        functools.partial(_mm_kernel, n_main=n_main),
        grid=(r // tm,),
        in_specs=[pl.BlockSpec((tm, k), lambda i: (i, 0)),
                  pl.BlockSpec((k, n), lambda i: (0, 0), pipeline_mode=pl.Buffered(1))],
        out_specs=[pl.BlockSpec((tm, wd), lambda i: (i, 0)) for wd, _ in widths],
        out_shape=[jax.ShapeDtypeStruct((r, wd), dt) for wd, dt in widths],
        compiler_params=_cparams("parallel"),
        name="in_proj",
    )(a, w)


def _gate_kernel(hx_ref, ys_ref, gw_ref, gb_ref, bw_ref, o_ref):
    hx = hx_ref[...]
    acc = None
    for i in range(4):
        g = _dot(hx, gw_ref[i]) + gb_ref[i]
        p = _dot(ys_ref[:, i * W4:(i + 1) * W4], bw_ref[i])
        t = _sigmoid(g) * p
        acc = t if acc is None else acc + t
    o_ref[...] = acc.astype(o_ref.dtype)


def _gated_sum(hx, ys, gw, gb, bw, tm=512, tn=512):
    r = hx.shape[0]
    tm = min(tm, r)
    return pl.pallas_call(
        _gate_kernel,
        grid=(D // tn, r // tm),
        in_specs=[pl.BlockSpec((tm, D), lambda j, i: (i, 0)),
                  pl.BlockSpec((tm, D), lambda j, i: (i, 0)),
                  pl.BlockSpec((4, D, tn), lambda j, i: (0, 0, j)),
                  pl.BlockSpec((4, 1, tn), lambda j, i: (0, 0, j)),
                  pl.BlockSpec((4, W4, tn), lambda j, i: (0, 0, j))],
        out_specs=pl.BlockSpec((tm, tn), lambda j, i: (i, j)),
        out_shape=jax.ShapeDtypeStruct((r, D), BF16),
        compiler_params=_cparams("parallel", "parallel"),
        name="gated_sum",
    )(hx, ys, gw, gb, bw)


def _proj_res_kernel(a_ref, w_ref, x_ref, mod_ref, nw_ref, o_ref, h_ref):
    m = mod_ref[...]
    x = x_ref[...] + m[2:3] * _dot(a_ref[...], w_ref[...])
    o_ref[...] = x
    h_ref[...] = _rms_mod(x, nw_ref[...], m, (3, 4)).astype(h_ref.dtype)


def _proj_residual(a, w, x, mods, norm_w, tm=512):
    r = a.shape[0]
    tm = min(tm, r)
    per = r // mods.shape[0] // tm
    row = pl.BlockSpec((tm, D), lambda i: (i, 0))
    return pl.pallas_call(
        _proj_res_kernel,
        grid=(r // tm,),
        in_specs=[row, pl.BlockSpec((D, D), lambda i: (0, 0)), row,
                  pl.BlockSpec((None, 8, D), lambda i: (i // per, 0, 0)),
                  pl.BlockSpec((1, D), lambda i: (0, 0))],
        out_specs=[row, row],
        out_shape=[jax.ShapeDtypeStruct((r, D), F32), jax.ShapeDtypeStruct((r, D), BF16)],
        compiler_params=_cparams("parallel"),
        name="out_proj",
    )(a, w, x, mods, norm_w.reshape(1, D))


def _ffn_kernel(h_ref, w1_ref, w2_ref, x_ref, mod_ref, nw_ref, nmod_ref, *refs, final):
    o_refs, acc_ref = refs[:-1], refs[-1]
    f = pl.program_id(1)

    @pl.when(f == 0)
    def _():
        acc_ref[...] = jnp.zeros_like(acc_ref)

    a = jnp.maximum(_dot(h_ref[...], w1_ref[...]), 0.0)
    acc_ref[...] += _dot((a * a).astype(BF16), w2_ref[...])

    @pl.when(f == pl.num_programs(1) - 1)
    def _():
        x = x_ref[...] + mod_ref[...][5:6] * acc_ref[...]
        if final:
            o_refs[0][...] = _rms_mod(x, nw_ref[...], None, None)
        else:
            o_refs[0][...] = x
            o_refs[1][...] = _rms_mod(x, nw_ref[...], nmod_ref[...], (0, 1)).astype(BF16)


def _ffn(hn, w1, w2, x, mods, next_norm_w, next_mods, final, tm=512, tf=1024):
    r = hn.shape[0]
    tm = min(tm, r)
    per = r // mods.shape[0] // tm
    row = pl.BlockSpec((tm, D), lambda i, f: (i, 0))
    mod_spec = pl.BlockSpec((None, 8, D), lambda i, f: (i // per, 0, 0))
    out_specs = [row] if final else [row, row]
    out_shape = [jax.ShapeDtypeStruct((r, D), F32)] + ([] if final else [jax.ShapeDtypeStruct((r, D), BF16)])
    return pl.pallas_call(
        functools.partial(_ffn_kernel, final=final),
        grid=(r // tm, D_FF // tf),
        in_specs=[row, pl.BlockSpec((D, tf), lambda i, f: (0, f)), pl.BlockSpec((tf, D), lambda i, f: (f, 0)),
                  row, mod_spec, pl.BlockSpec((1, D), lambda i, f: (0, 0)), mod_spec],
        out_specs=out_specs,
        out_shape=out_shape,
        scratch_shapes=[pltpu.VMEM((tm, D), F32)],
        compiler_params=_cparams("parallel", "arbitrary"),
        name="ffn",
    )(hn, w1, w2, x, mods, next_norm_w.reshape(1, D), next_mods)


def _rwkv_chunk(b, z, r, k, v, wd, ad, w_up, w0, a_up, a0, k_k, k_a):
    rev = z == 1
    last = 0 if rev else CH - 1
    lo = lax.broadcasted_iota(jnp.int32, (CH, 128), 1) < 64

    wdz = wd[:, 64 * z:64 * z + 64]
    adz = ad[:, 64 * z:64 * z + 64]
    lw = -RW_DECAY_SCALE * _sigmoid(_dot(jnp.tanh(wdz), w_up, HI) + w0)
    a = _sigmoid(_dot(adz, a_up, HI) + a0)
    kk = k * k_k
    kd = k * (1.0 + (a - 1.0) * k_a)

    lc_incl = _cumdot(_scan_mask(CH, True, rev), lw)
    p_incl = jnp.exp(lc_incl)
    p_inv = jnp.exp(-lc_incl)
    p_excl = jnp.exp(lc_incl - lw)
    p_end = p_incl[last:last + 1]

    chains = []
    for p in range(RW_HEADS // 2):
        sl = slice(128 * p, 128 * (p + 1))
        kkp = kk[:, sl]
        sq = kkp * kkp
        nrm = jnp.where(lo, jnp.sum(jnp.where(lo, sq, 0.0), axis=-1, keepdims=True),
                        jnp.sum(jnp.where(lo, 0.0, sq), axis=-1, keepdims=True))
        kkp = kkp / jnp.maximum(jnp.sqrt(nrm), 1e-12)
        at = -kkp * p_excl[:, sl]
        bt = kkp * a[:, sl] * p_inv[:, sl]
        kt = kd[:, sl] * p_inv[:, sl]
        rt = r[:, sl] * p_incl[:, sl]
        chains.append(dict(
            z=z, key=(b, z), idx=(b, z, p), p_end=p_end[:, sl],
            ar=_b(jnp.concatenate([at, rt], axis=0)),
            bk=_b(jnp.concatenate([_stack_heads(bt), _stack_heads(kt)], axis=0)),
            vs=_b(_stack_heads(v[:, sl]))))
    return chains


def _row_scan_mask(inclusive, reverse):
    r = lax.broadcasted_iota(jnp.int32, (CH, 2 * CH), 0)
    c = lax.broadcasted_iota(jnp.int32, (CH, 2 * CH), 1) & (CH - 1)
    if reverse:
        return (r <= c) if inclusive else (r < c)
    return (r >= c) if inclusive else (r > c)


def _rwkv_solve(chains, s_ref):
    n2 = 2 * CH
    masks = {z: (_row_scan_mask(True, z == 1), _row_scan_mask(False, z == 1)) for z in (0, 1)}
    stack = lambda x: _b(_stack_heads(x))
    s0 = [s_ref[c["idx"]] for c in chains]
    g = [_dot_nt(c["ar"], c["bk"]) for c in chains]
    s0t = [_dot_nt(c["ar"], _b(s)) for c, s in zip(chains, s0)]
    yield
    lp = [jnp.where(masks[c["z"]][1], gi[:CH, :n2], 0.0) for c, gi in zip(chains, g)]
    u = [st[:CH] + _dot(_b(jnp.where(masks[c["z"]][1], gi[:CH, n2:], 0.0)), c["vs"])
         for c, gi, st in zip(chains, g, s0t)]
    for j in range(6):
        lb = [_b(li) for li in lp]
        u = [ui + _dot(li, stack(ui)) for ui, li in zip(u, lb)]
        if j < 5:
            lp = [_dot(li, stack(lf)) for li, lf in zip(lb, lp)]
        yield
    ys = {}
    for c, gi, st, ui, s in zip(chains, g, s0t, u, s0):
        m_incl = masks[c["z"]][0]
        uv = jnp.concatenate([stack(ui), c["vs"]], axis=0)
        rbk = _b(jnp.concatenate([jnp.where(m_incl, gi[CH:, :n2], 0.0), jnp.where(m_incl, gi[CH:, n2:], 0.0)], axis=1))
        ys.setdefault(c["key"], []).append(st[CH:] + _dot(rbk, uv))
        s_ref[c["idx"]] = (s + _dot_tn(uv, c["bk"])) * c["p_end"]
    return {key: jnp.concatenate(parts, axis=1) for key, parts in ys.items()}


def _seq_block_specs(n_b, tb, nblk, cols):
    fwd = [pl.BlockSpec((n_b, tb, w), functools.partial(lambda j, c: (0, j, c), c=c)) for w, c in cols]
    bwd = [pl.BlockSpec((n_b, tb, w), functools.partial(lambda j, c: (0, nblk - 1 - j, c), c=c)) for w, c in cols]
    return fwd, bwd


def _run_stages(gens):
    out = [None] * len(gens)
    live = list(range(len(gens)))
    while live:
        for i in list(live):
            try:
                next(gens[i])
            except StopIteration as done:
                out[i] = done.value
                live.remove(i)
    return out


def _lru_kernel(xf, xb, tail, head, conv_w, conv_b, wa, ba, wx, bx, lam, h0_ref,
                hf_ref, hb_ref, h_ref, ext, a_s, b_s, *, n_b, t, wpb, nscan, wmajor):
    j = pl.program_id(0)
    nstep = nscan // wpb

    def rows(ref, b, w, sl):
        return ref[b, w, sl, :] if wmajor else ref[b, sl, w, :]

    def put(ref, b, w, sl, val):
        if wmajor:
            ref[b, w, sl, :] = val
        else:
            ref[b, sl, w, :] = val

    @pl.when(j == 0)
    def _():
        h_ref[...] = h0_ref[...]

    for wi in range(wpb):
        local = (wi, wpb - 1 - wi)
        glob = (j * wpb + wi, (nstep - 1 - j) * wpb + wpb - 1 - wi)
        for z, xr in enumerate((xf, xb)):
            w, g = local[z], glob[z]
            for b in range(n_b):
                if w > 0:
                    prev2 = rows(xr, b, w - 1, slice(t - 2, t))
                else:
                    prev2 = jnp.where(g > 0, tail[b, jnp.maximum(g - 1, 0)], 0.0)
                if w < wpb - 1:
                    next1 = rows(xr, b, w + 1, slice(0, 1))
                else:
                    next1 = jnp.where(g < nscan - 1, head[b, pl.ds(jnp.minimum(g + 1, nscan - 1), 1), :], 0.0)
                ext[b, z, 6:8, :] = prev2
                ext[b, z, 8:8 + t, :] = rows(xr, b, w, slice(None))
                ext[b, z, 8 + t:9 + t, :] = next1
                xc = conv_b[...] + ext[b, z, 6:6 + t, :] * conv_w[0:1, :]
                for tap in range(1, 4):
                    xc = xc + ext[b, z, 6 + tap:6 + tap + t, :] * conv_w[tap:tap + 1, :]
                xcb = _b(xc)
                rg = _sigmoid(_dot(xcb, wa[z]) + ba[z])
                ig = _sigmoid(_dot(xcb, wx[z]) + bx[z])
                log_a = -LRU_C * rg * _softplus(-lam[z])
                a = jnp.exp(log_a)
                a_s[b, z] = a
                b_s[b, z] = jnp.sqrt(-jnp.tanh(log_a) * (a * a + 1.0)) * ig * xc

        def step(i, carry):
            rf = pl.ds(i, 1)
            rb = pl.ds(t - 1 - i, 1)
            out = []
            for b in range(n_b):
                h_f = a_s[b, 0, rf, :] * carry[2 * b] + b_s[b, 0, rf, :]
                h_b = a_s[b, 1, rb, :] * carry[2 * b + 1] + b_s[b, 1, rb, :]
                put(hf_ref, b, local[0], rf, h_f)
                put(hb_ref, b, local[1], rb, h_b)
                out += [h_f, h_b]
            return tuple(out)

        init = tuple(h_ref[b, z:z + 1, :] for b in range(n_b) for z in (0, 1))
        fin = lax.fori_loop(0, t, step, init, unroll=8)
        for b in range(n_b):
            h_ref[b, 0:1, :] = fin[2 * b]
            h_ref[b, 1:2, :] = fin[2 * b + 1]


def _lru(x4, tail, head, lru_w, h0, wpb, wmajor):
    n_b = x4.shape[0]
    nscan, t = (x4.shape[1], x4.shape[2]) if wmajor else (x4.shape[2], x4.shape[1])
    nstep = nscan // wpb
    shp = (n_b, wpb, t, W4) if wmajor else (n_b, t, wpb, W4)
    at = (lambda j: (0, j, 0, 0)) if wmajor else (lambda j: (0, 0, j, 0))
    at_rev = (lambda j: (0, nstep - 1 - j, 0, 0)) if wmajor else (lambda j: (0, 0, nstep - 1 - j, 0))
    full = lambda s: pl.BlockSpec(s, lambda j: (0,) * len(s))
    out_sds = jax.ShapeDtypeStruct(x4.shape[:3] + (W4,), F32)
    return pl.pallas_call(
        functools.partial(_lru_kernel, n_b=n_b, t=t, wpb=wpb, nscan=nscan, wmajor=wmajor),
        grid=(nstep,),
        in_specs=[pl.BlockSpec(shp, at), pl.BlockSpec(shp, at_rev), full(tail.shape), full(head.shape),
                  full((4, W4)), full((1, W4)), full((2, W4, W4)), full((2, 1, W4)),
                  full((2, W4, W4)), full((2, 1, W4)), full((2, 1, W4)), full(h0.shape)],
        out_specs=[pl.BlockSpec(shp, at), pl.BlockSpec(shp, at_rev), full(h0.shape)],
        out_shape=[out_sds, out_sds, jax.ShapeDtypeStruct(h0.shape, F32)],
        scratch_shapes=[pltpu.VMEM((n_b, 2, t + 16, W4), F32), pltpu.VMEM((n_b, 2, t, W4), F32),
                        pltpu.VMEM((n_b, 2, t, W4), F32)],
        compiler_params=_cparams("arbitrary"),
        name="rglru",
    )(x4, x4, tail, head, *lru_w, h0)


def _gla_chunk(b, z, q, k, v, wd, w_up, w0):
    rev = z == 1
    last = 0 if rev else CH - 1
    pm_incl = _pair_scan_mask(True, rev)
    la = _log_sigmoid(_dot(wd[:, GLA_RANK * z:GLA_RANK * (z + 1)], w_up, HI) + w0) * (1.0 / GLA_TAU)
    bcum = _cumdot(_scan_mask(CH, True, rev), la)
    b_end = bcum[last:last + 1]
    qe = q * (GLA_DK ** -0.5) * jnp.exp(bcum)
    ke = k * jnp.exp(-bcum)
    kd = k * jnp.exp(b_end - bcum)
    e_end = jnp.exp(b_end)
    chains = []
    for p in range(GLA_HEADS // 2):
        sl = slice(128 * p, 128 * (p + 1))
        chains.append(dict(
            key=(b, z), idx=(b, z, p), mask=pm_incl, e_end=e_end[:, sl],
            qs=_b(_stack_heads(qe[:, sl])),
            ks=_b(_stack_heads(ke[:, sl])),
            kds=_b(_stack_heads(kd[:, sl])),
            vs=_b(jnp.concatenate([v[:, 2 * GLA_DV * p:2 * GLA_DV * p + GLA_DV],
                                   v[:, 2 * GLA_DV * p + GLA_DV:2 * GLA_DV * (p + 1)]], axis=0))))
    return chains


def _gla_solve(chains, s_ref):
    st = [s_ref[c["idx"]] for c in chains]
    att = [_b(jnp.where(c["mask"], _dot_nt(c["qs"], c["ks"]), 0.0)) for c in chains]
    inter = [_dot_nt(c["qs"], _b(s)) for c, s in zip(chains, st)]
    yield
    outs = {}
    for c, a, i, s in zip(chains, att, inter, st):
        o = _dot(a, c["vs"]) + i
        outs.setdefault(c["key"], []).extend([o[:CH], o[CH:]])
        s_ref[c["idx"]] = s * c["e_end"] + _dot_tn(c["vs"], c["kds"])
    return {key: jnp.concatenate(parts, axis=1) for key, parts in outs.items()}


def _mlstm_chunk(b, z, q, k, v, gates, bias):
    rev = z == 1
    m_incl = _scan_mask(CH, True, rev)
    last = 0 if rev else CH - 1
    pre = gates + bias
    lf = _log_sigmoid(pre)
    fcum = _cumdot(m_incl, lf)
    pre_t = pre.T
    fcum_t = fcum.T
    k = k * (ML_DH ** -0.5)
    chains = []
    for h in range(ML_HEADS):
        ci = ML_HEADS * z + h
        cf = 2 * ML_HEADS + ML_HEADS * z + h
        sl = slice(ML_DH * h, ML_DH * (h + 1))
        fc = fcum[:, cf:cf + 1]
        ic = pre[:, ci:ci + 1]
        fr = fcum_t[cf:cf + 1, :]
        ir = pre_t[ci:ci + 1, :]
        chains.append(dict(key=(b, z), idx=(b, z, h), q=q[:, sl], k=k[:, sl], v=v[:, sl],
                           qb=_b(q[:, sl]), kb=_b(k[:, sl]),
                           fc=fc, ic=ic, f_end=fc[last:last + 1],
                           dlog=jnp.where(m_incl, fc + (ir - fr), NEG)))
    return chains


def _mlstm_solve(chains, cm_ref, n_ref, m_ref):
    cm = [cm_ref[c["idx"]] for c in chains]
    n = [n_ref[c["idx"]] for c in chains]
    m = [m_ref[c["idx"]] for c in chains]
    qk = [_dot_nt(c["qb"], c["kb"]) for c in chains]
    qc = [_dot_nt(c["qb"], _b(s)) for c, s in zip(chains, cm)]
    yield
    inter = [c["fc"] + mi for c, mi in zip(chains, m)]
    mt = [jnp.maximum(i, jnp.max(c["dlog"], axis=-1, keepdims=True)) for c, i in zip(chains, inter)]
    w = [jnp.exp(c["dlog"] - t) * s for c, t, s in zip(chains, mt, qk)]
    ei = [jnp.exp(i - t) for i, t in zip(inter, mt)]
    yield
    wv = [_dot(_b(wi), _b(c["v"])) for c, wi in zip(chains, w)]
    yield
    outs = {}
    for c, wi, e, t, a, qci, cmi, ni, mi in zip(chains, w, ei, mt, wv, qc, cm, n, m):
        num = a + e * qci
        den = jnp.sum(wi, axis=-1, keepdims=True) + e * jnp.sum(c["q"] * ni, axis=-1, keepdims=True)
        outs.setdefault(c["key"], []).append(num / jnp.maximum(jnp.abs(den), jnp.exp(-t)))
        g = c["f_end"] - c["fc"] + c["ic"]
        m_new = jnp.maximum(c["f_end"] + mi, jnp.max(g, axis=0, keepdims=True))
        dec = jnp.exp(c["f_end"] + mi - m_new)
        wg = jnp.exp(g - m_new)
        cm_ref[c["idx"]] = dec * cmi + _dot_tn(_b(c["v"] * wg), c["kb"])
        n_ref[c["idx"]] = dec * ni + jnp.sum(c["k"] * wg, axis=0, keepdims=True)
        m_ref[c["idx"]] = m_new
    return {key: jnp.concatenate(parts, axis=1) for key, parts in outs.items()}


_RW_SEQ = [(0, W4, 0), (0, W4, 1), (0, W4, 2), (1, 128, 0), (1, 128, 1)]
_GLA_SEQ = [(0, 256, 0), (0, 256, 1), (0, 512, 1), (1, 128, 0)]
_ML_SEQ = [(0, W4, 0), (0, W4, 1), (0, W4, 2), (1, 128, 0)]


def _mix_kernel(*refs, n_b, nchunk):
    it = iter(refs)
    take = lambda n: [next(it) for _ in range(n)]
    rw_seq = (take(len(_RW_SEQ)), take(len(_RW_SEQ)))
    gla_seq = (take(len(_GLA_SEQ)), take(len(_GLA_SEQ)))
    ml_seq = (take(len(_ML_SEQ)), take(len(_ML_SEQ)))
    rw_par, gla_par, (ml_bias,) = take(6), take(2), take(1)
    init = take(5)
    outs = (take(2), take(2), take(2))
    state = take(5)
    rw_s, gla_s, cm_ref, n_ref, m_ref = state

    @pl.when(pl.program_id(0) == 0)
    def _():
        for dst, src in zip(state, init):
            dst[...] = src[...]

    def body(rows_f, rows_b):
        load = lambda ref, b, rows: ref[b, rows, :].astype(F32)
        rw, gla, ml = [], [], []
        for b in range(n_b):
            for z, rows in enumerate((rows_f, rows_b)):
                w_up, w0, a_up, a0, k_k, k_a = rw_par
                rw += _rwkv_chunk(b, z, *(load(ref, b, rows) for ref in rw_seq[z]),
                                  w_up[z], w0[z], a_up[z], a0[z], k_k[...], k_a[...])
                gla += _gla_chunk(b, z, *(load(ref, b, rows) for ref in gla_seq[z]), gla_par[0][z], gla_par[1][z])
                ml += _mlstm_chunk(b, z, *(load(ref, b, rows) for ref in ml_seq[z]), ml_bias[...])
        results = _run_stages([_rwkv_solve(rw, rw_s), _mlstm_solve(ml, cm_ref, n_ref, m_ref), _gla_solve(gla, gla_s)])
        for (o_f, o_b), res in zip((outs[0], outs[2], outs[1]), results):
            for b in range(n_b):
                o_f[b, rows_f, :] = res[(b, 0)].astype(o_f.dtype)
                o_b[b, rows_b, :] = res[(b, 1)].astype(o_b.dtype)

    _for_chunks(nchunk, body)


def _mix(u_rw, u_gla, u_ml, rw_par, gla_par, ml_bias, states, tb=128):
    n_b, length, _ = u_rw[0].shape
    nblk = length // tb
    full = lambda a: pl.BlockSpec(a.shape, lambda j: (0,) * a.ndim)
    seq_specs, seq_args = [], []
    for u, cols in ((u_rw, _RW_SEQ), (u_gla, _GLA_SEQ), (u_ml, _ML_SEQ)):
        fwd, bwd = _seq_block_specs(n_b, tb, nblk, [(w, c) for _, w, c in cols])
        seq_specs += fwd + bwd
        seq_args += [u[a] for a, _, _ in cols] * 2
    params = list(rw_par) + list(gla_par) + [ml_bias]
    out_f, out_b = _seq_block_specs(n_b, tb, nblk, [(W4, 0)])
    out_sds = jax.ShapeDtypeStruct((n_b, length, W4), BF16)
    res = pl.pallas_call(
        functools.partial(_mix_kernel, n_b=n_b, nchunk=tb // CH),
        grid=(nblk,),
        in_specs=seq_specs + [full(a) for a in params + list(states)],
        out_specs=[out_f[0], out_b[0]] * 3 + [full(a) for a in states],
        out_shape=[out_sds] * 6 + [jax.ShapeDtypeStruct(a.shape, F32) for a in states],
        compiler_params=_cparams("arbitrary"),
        name="mix3",
    )(*seq_args, *params, *states)
    return res[0:2], res[2:4], res[4:6], res[6:]


def _branch_kernel(yaf, yab, rr, rk, rv, rg, hbf, hbb, lg, ocf, ocb, gr, odf, odb, mo,
                   rw_ln, rw_rk, g_up, gla_ln, ml_ln, e64, e128, o_ref):
    f = lambda ref: ref[...].astype(F32)
    y = f(yaf) + f(yab)
    y = y * lax.rsqrt(_headsum(y * y, e64[...]) * (1.0 / RW_HEAD) + EPS) * rw_ln[...]
    bonus = _headsum(f(rr) * f(rk) * rw_rk[...], e64[...]) * f(rv)
    g = _dot(_sigmoid(rg[...]).astype(BF16), g_up[...])
    o_ref[:, 0:W4] = ((y + bonus) * g).astype(o_ref.dtype)
    o_ref[:, W4:2 * W4] = ((hbf[...] + hbb[...]) * jax.nn.gelu(lg[...])).astype(o_ref.dtype)
    y = f(ocf) + f(ocb)
    y = y * lax.rsqrt(_headsum(y * y, e128[...]) * (1.0 / GLA_DV) + EPS) * gla_ln[...]
    r = f(gr)
    o_ref[:, 2 * W4:3 * W4] = (y * (r * _sigmoid(r))).astype(o_ref.dtype)
    y = f(odf) + f(odb)
    y = y * lax.rsqrt(_headsum(y * y, e128[...]) * (1.0 / ML_DH) + EPS) * ml_ln[...]
    o_ref[:, 3 * W4:4 * W4] = (y * _sigmoid(f(mo))).astype(o_ref.dtype)


def _branches(u_rw, u_lru, u_gla, u_ml, ya, hb, oc, od, rw_ln, rw_rk, g_up, gla_ln, ml_ln, e64, e128, tm=256):
    r = u_lru.shape[0]
    row = lambda w, c: pl.BlockSpec((tm, w), lambda i: (i, c))
    full = lambda shp: pl.BlockSpec(shp, lambda i: (0,) * len(shp))
    return pl.pallas_call(
        _branch_kernel,
        grid=(r // tm,),
        in_specs=[row(W4, 0), row(W4, 0), row(W4, 0), row(W4, 1), row(W4, 2), row(128, 2),
                  row(W4, 0), row(W4, 0), row(W4, 1),
                  row(W4, 0), row(W4, 0), row(W4, 2),
                  row(W4, 0), row(W4, 0), row(W4, 3),
                  full((1, W4)), full((1, W4)), full((128, W4)), full((1, W4)), full((1, W4)),
                  full((W4, W4)), full((W4, W4))],
        out_specs=pl.BlockSpec((tm, D), lambda i: (i, 0)),
        out_shape=jax.ShapeDtypeStruct((r, D), BF16),
        compiler_params=_cparams("parallel"),
        name="branches",
    )(ya[0], ya[1], u_rw[0], u_rw[0], u_rw[0], u_rw[1], hb[0], hb[1], u_lru, oc[0], oc[1], u_gla[0],
      od[0], od[1], u_ml[0], rw_ln, rw_rk, g_up, gla_ln, ml_ln, e64, e128)


def _block_ones(width, head):
    i = np.arange(width) // head
    return jnp.asarray((i[:, None] == i[None, :]).astype(np.float32)).astype(BF16)


def _block_diag(w):
    z, nb, c, _ = w.shape
    eye = jnp.eye(nb, dtype=w.dtype)
    return jnp.einsum("znij,nm->znimj", w, eye).reshape(z, nb * c, nb * c)


N_MAIN = (3 * W4, 0, 256 + 256 + 512 + 512, 4 * W4)


def _regroup_w_in(w):
    sizes = [512, 512, 512, 128, 128, 128, 512, 512, 256, 256, 512, 512, 32, 512, 512, 512, 512, 8, 8]
    offs = np.concatenate([[0], np.cumsum(sizes)])
    pad = lambda n: jnp.zeros((w.shape[0], n), w.dtype)
    rw = w[:, offs[0]:offs[6]]
    lru = w[:, offs[6]:offs[8]]
    gla = jnp.concatenate([w[:, offs[8]:offs[13]], pad(96)], axis=1)
    ml = jnp.concatenate([w[:, offs[13]:offs[19]], pad(112)], axis=1)
    return [rw, lru, gla, ml]


def _mixers(hx, n_b, lw, states, latent):
    r = hx.shape[0]
    length = r // n_b
    u_rw, (u_lru,), u_gla, u_ml = (_mm(hx, w, n_main) for w, n_main in zip(lw["w_in"], N_MAIN))
    seq = lambda us: [u.reshape(n_b, length, u.shape[-1]) for u in us]
    flat = lambda u: u.reshape(r, u.shape[-1])
    s_rw, s_lru, s_gla, s_ml = states

    (ya_f, ya_b), (oc_f, oc_b), (od_f, od_b), (s_rw, s_gla, *s_ml) = _mix(
        seq(u_rw), seq(u_gla), seq(u_ml), lw["rw"], lw["gla"], lw["ml_bias"], [s_rw, s_gla, *s_ml])
    if latent:
        rows = length // GRID_W
        x4 = u_lru.reshape(n_b, rows, GRID_W, LRU_COLS)
        tail = jnp.swapaxes(x4[:, rows - 2:rows, :, :W4], 1, 2)
        head = x4[:, 0, :, :W4]
        hb_f, hb_b, s_lru = _lru(x4, tail, head, lw["lru"], s_lru, wpb=8, wmajor=False)
    else:
        x4 = u_lru.reshape(n_b, 1, length, LRU_COLS)
        hb_f, hb_b, s_lru = _lru(x4, jnp.zeros((n_b, 1, 2, W4), F32), jnp.zeros((n_b, 1, W4), F32), lw["lru"], s_lru,
                                 wpb=1, wmajor=True)
    ys = _branches(u_rw, u_lru, u_gla, u_ml, (flat(ya_f), flat(ya_b)), (flat(hb_f), flat(hb_b)),
                   (flat(oc_f), flat(oc_b)), (flat(od_f), flat(od_b)), *lw["branch"])
    return ys, (s_rw, s_lru, s_gla, tuple(s_ml))


def _zero_states(n_b):
    return (jnp.zeros((n_b, 2, RW_HEADS // 2, 128, 128), F32),
            jnp.zeros((n_b, 2, W4), F32),
            jnp.zeros((n_b, 2, GLA_HEADS // 2, GLA_DV, 2 * GLA_DK), F32),
            (jnp.zeros((n_b, 2, ML_HEADS, ML_DH, ML_DH), F32),
             jnp.zeros((n_b, 2, ML_HEADS, 1, ML_DH), F32),
             jnp.full((n_b, 2, ML_HEADS, 1, 1), NEG, F32)))


def kernel(x, c, ctx, c_ctx, ada_w, ada_b, norm_mix_w, w_in, rw_w_up, rw_w0, rw_a_up, rw_a0, rw_g_up, rw_k_k, rw_k_a, rw_r_k, rw_ln_w, lru_conv_w, lru_conv_b, lru_w_a, lru_b_a, lru_w_x, lru_b_x, lru_lambda, gla_w_up, gla_w0, gla_ln_w, ml_i_b, ml_f_b, ml_ln_w, br_w, gate_w, gate_b, out_w, norm_ffn_w, ffn_w1, ffn_w2, final_norm_w):
    n_b, seq, _ = x.shape
    n_c, ctx_len, _ = ctx.shape
    xs = x.reshape(n_b * seq, D)
    cs = ctx.reshape(n_c * ctx_len, D)
    cc = jnp.zeros((8, D), F32).at[0:n_b].set(c).at[n_b].set(c_ctx)
    e64 = _block_ones(W4, RW_HEAD)
    e128 = _block_ones(W4, GLA_DV)
    row = lambda v: v.reshape(1, -1)

    mods = jnp.pad(_ada(cc, ada_w, ada_b).reshape(DEPTH, 8, 6, D), ((0, 0), (0, 0), (0, 2), (0, 0)))
    mod_x = [mods[l, 0:n_b] for l in range(DEPTH)]
    mod_c = [mods[l, n_b:n_b + 1] for l in range(DEPTH)]
    hx = _norm(xs, norm_mix_w[0], mod_x[0], (0, 1), BF16)
    hc = _norm(cs, norm_mix_w[0], mod_c[0], (0, 1), BF16)
    w_in_b = w_in.astype(BF16)

    for l in range(DEPTH):
        last = l == DEPTH - 1
        lw = {
            "w_in": _regroup_w_in(w_in_b[l]),
            "rw": (rw_w_up[l], rw_w0[l][:, None, :], rw_a_up[l], rw_a0[l][:, None, :],
                   row(rw_k_k[l]), row(rw_k_a[l])),
            "lru": (lru_conv_w[l], row(lru_conv_b[l]), _block_diag(lru_w_a[l]).astype(BF16), lru_b_a[l][:, None, :],
                    _block_diag(lru_w_x[l]).astype(BF16), lru_b_x[l][:, None, :], lru_lambda[l][:, None, :]),
            "gla": (gla_w_up[l], gla_w0[l][:, None, :]),
            "ml_bias": jnp.zeros((1, 128), F32).at[0, 0:8].set(ml_i_b[l].reshape(-1)).at[0, 8:16].set(ml_f_b[l].reshape(-1)),
            "branch": (row(rw_ln_w[l]), row(rw_r_k[l]), rw_g_up[l].astype(BF16), row(gla_ln_w[l]), row(ml_ln_w[l]),
                       e64, e128),
        }
        gw = gate_w[l].astype(BF16)
        gb = gate_b[l][:, None, :]
        bw = br_w[l].astype(BF16)
        ow = out_w[l].astype(BF16)
        w1 = ffn_w1[l].astype(BF16)
        w2 = ffn_w2[l].astype(BF16)

        ys_c, st_c = _mixers(hc, n_c, lw, _zero_states(n_c), latent=False)
        ys_x, _ = _mixers(hx, n_b, lw, st_c, latent=True)
        xs, hn = _proj_residual(_gated_sum(hx, ys_x, gw, gb, bw), ow, xs, mod_x[l], norm_ffn_w[l])
        if last:
            out, = _ffn(hn, w1, w2, xs, mod_x[l], final_norm_w, mod_x[l], final=True)
        else:
            xs, hx = _ffn(hn, w1, w2, xs, mod_x[l], norm_mix_w[l + 1], mod_x[l + 1], final=False)
            cs, hn = _proj_residual(_gated_sum(hc, ys_c, gw, gb, bw), ow, cs, mod_c[l], norm_ffn_w[l])
            cs, hc = _ffn(hn, w1, w2, cs, mod_c[l], norm_mix_w[l + 1], mod_c[l + 1], final=False)
    return out.reshape(n_b, seq, D)
```

```python
import functools

import numpy as np
import jax
import jax.numpy as jnp
from jax import lax
from jax.experimental import pallas as pl
from jax.experimental.pallas import tpu as pltpu

F32 = jnp.float32
BF16 = jnp.bfloat16
HI = lax.Precision.HIGHEST

D = 2048
DEPTH = 2
GRID_W = 64
EPS = 1e-6
NEG = -1e30
CH = 64
D_FF = 4 * D
W4 = D // 4

RW_HEAD = 64
RW_HEADS = W4 // RW_HEAD
RW_DECAY_SCALE = 0.6065306597
LRU_BLOCKS = 8
LRU_C = 8.0
GLA_HEADS = 4
GLA_DK = 64
GLA_DV = 128
GLA_RANK = 16
GLA_TAU = 16.0
ML_HEADS = 4
ML_DH = 128

RW_COLS = 3 * W4 + 3 * 128
LRU_COLS = 2 * W4
GLA_COLS = 256 + 256 + 512 + 512 + 128
ML_COLS = 4 * W4 + 128

VMEM_LIMIT = 56 * 1024 * 1024


def _cparams(*sem):
    return pltpu.CompilerParams(dimension_semantics=sem, vmem_limit_bytes=VMEM_LIMIT)


def _dot(a, b, prec=None):
    return jnp.dot(a, b, precision=prec, preferred_element_type=F32)


def _dot_nt(a, b, prec=None):
    return lax.dot_general(a, b, (((1,), (1,)), ((), ())), precision=prec, preferred_element_type=F32)


def _dot_tn(a, b, prec=None):
    return lax.dot_general(a, b, (((0,), (0,)), ((), ())), precision=prec, preferred_element_type=F32)


def _b(x):
    return x.astype(BF16)


def _bdot(a, b):
    return jnp.dot(_b(a), _b(b), preferred_element_type=F32)


def _bdot_nt(a, b):
    return _dot_nt(_b(a), _b(b))


def _bdot_tn(a, b):
    return _dot_tn(_b(a), _b(b))


def _split_bf16(x, parts):
    out = []
    for i in range(parts):
        hi = _b(x)
        out.append(hi)
        if i < parts - 1:
            x = x - hi.astype(F32)
    return out


def _cumdot(mask, x):
    m = jnp.where(mask, 1.0, 0.0).astype(BF16)
    parts = _split_bf16(x, 3)
    acc = _dot(m, parts[0])
    for p in parts[1:]:
        acc = acc + _dot(m, p)
    return acc


def _headsum(x, e):
    hi, lo = _split_bf16(x, 2)
    return _dot(hi, e) + _dot(lo, e)


def _stack_heads(x):
    lo = lax.broadcasted_iota(jnp.int32, x.shape, 1) < 64
    return jnp.concatenate([jnp.where(lo, x, 0.0), jnp.where(lo, 0.0, x)], axis=0)


def _pair_scan_mask(inclusive, reverse):
    r = lax.broadcasted_iota(jnp.int32, (2 * CH, 2 * CH), 0) & (CH - 1)
    c = lax.broadcasted_iota(jnp.int32, (2 * CH, 2 * CH), 1) & (CH - 1)
    if reverse:
        return (r <= c) if inclusive else (r < c)
    return (r >= c) if inclusive else (r > c)


def _for_chunks(nchunk, body):
    for ci in range(nchunk):
        body(slice(ci * CH, (ci + 1) * CH), slice((nchunk - 1 - ci) * CH, (nchunk - ci) * CH))


def _sigmoid(x):
    return 1.0 / (1.0 + jnp.exp(-x))


def _log_sigmoid(x):
    return jnp.minimum(x, 0.0) - jnp.log1p(jnp.exp(-jnp.abs(x)))


def _softplus(x):
    return jnp.maximum(x, 0.0) + jnp.log1p(jnp.exp(-jnp.abs(x)))


def _scan_mask(n, inclusive, reverse):
    r = lax.broadcasted_iota(jnp.int32, (n, n), 0)
    c = lax.broadcasted_iota(jnp.int32, (n, n), 1)
    if reverse:
        return (r <= c) if inclusive else (r < c)
    return (r >= c) if inclusive else (r > c)


def _ada_kernel(s_ref, w_ref, b_ref, o_ref):
    s = s_ref[...]
    s = s * _sigmoid(s)
    o_ref[...] = _dot(s, w_ref[...], HI) + b_ref[...]


def _ada(cc, w, b):
    tn = 1024
    n_l = w.shape[0]
    return pl.pallas_call(
        _ada_kernel,
        grid=(n_l, 6 * D // tn),
        in_specs=[pl.BlockSpec((8, D), lambda l, j: (0, 0)),
                  pl.BlockSpec((None, D, tn), lambda l, j: (l, 0, j)),
                  pl.BlockSpec((None, 1, tn), lambda l, j: (l, 0, j))],
        out_specs=pl.BlockSpec((None, 8, tn), lambda l, j: (l, 0, j)),
        out_shape=jax.ShapeDtypeStruct((n_l, 8, 6 * D), F32),
        compiler_params=_cparams("parallel", "parallel"),
        name="ada",
    )(cc, w, b.reshape(n_l, 1, 6 * D))


def _rms_mod(x, w, m, rows):
    y = x * lax.rsqrt(jnp.mean(x * x, axis=-1, keepdims=True) + EPS) * w
    if rows is not None:
        shift, scale = rows
        y = y * (1.0 + m[scale:scale + 1]) + m[shift:shift + 1]
    return y


def _norm_kernel(x_ref, w_ref, mod_ref, o_ref, *, rows):
    o_ref[...] = _rms_mod(x_ref[...], w_ref[...], mod_ref[...], rows).astype(o_ref.dtype)


def _norm(x, w, mods, rows, out_dtype, tm=256):
    r = x.shape[0]
    g = mods.shape[0]
    per = r // g // tm
    return pl.pallas_call(
        functools.partial(_norm_kernel, rows=rows),
        grid=(r // tm,),
        in_specs=[pl.BlockSpec((tm, D), lambda i: (i, 0)),
                  pl.BlockSpec((1, D), lambda i: (0, 0)),
                  pl.BlockSpec((None, 8, D), lambda i: (i // per, 0, 0))],
        out_specs=pl.BlockSpec((tm, D), lambda i: (i, 0)),
        out_shape=jax.ShapeDtypeStruct((r, D), out_dtype),
        compiler_params=_cparams("parallel"),
        name="norm",
    )(x, w.reshape(1, D), mods)


def _mm_kernel(a_ref, w_ref, *o_refs, n_main):
    y = _dot(a_ref[...], w_ref[...])
    if n_main:
        o_refs[0][...] = y[:, :n_main].astype(BF16)
    if n_main < y.shape[1]:
        o_refs[-1][...] = y[:, n_main:]


def _mm(a, w, l, n_main, tm=512):
    r, k = a.shape
    n = w.shape[2]
    tm = min(tm, r)
    widths = [(n_main, BF16)] * bool(n_main) + [(n - n_main, F32)] * bool(n - n_main)
    return pl.pallas_call(
        functools.partial(_mm_kernel, n_main=n_main),
        grid=(r // tm,),
        in_specs=[pl.BlockSpec((tm, k), lambda i: (i, 0)),
                  pl.BlockSpec((None, k, n), lambda i: (l, 0, 0), pipeline_mode=pl.Buffered(1))],
        out_specs=[pl.BlockSpec((tm, wd), lambda i: (i, 0)) for wd, _ in widths],
        out_shape=[jax.ShapeDtypeStruct((r, wd), dt) for wd, dt in widths],
        compiler_params=_cparams("parallel"),
        name="in_proj",
    )(a, w)


def _gate_kernel(hx_ref, ys_ref, gw_ref, gb_ref, bw_ref, o_ref):
    hx = hx_ref[...]
    acc = None
    for i in range(4):
        g = _dot(hx, gw_ref[i]) + gb_ref[i]
        p = _dot(ys_ref[:, i * W4:(i + 1) * W4], bw_ref[i])
        t = _sigmoid(g) * p
        acc = t if acc is None else acc + t
    o_ref[...] = acc.astype(o_ref.dtype)


def _gated_sum(hx, ys, gw, gb, bw, l, tm=512, tn=512):
    r = hx.shape[0]
    tm = min(tm, r)
    return pl.pallas_call(
        _gate_kernel,
        grid=(D // tn, r // tm),
        in_specs=[pl.BlockSpec((tm, D), lambda j, i: (i, 0)),
                  pl.BlockSpec((tm, D), lambda j, i: (i, 0)),
                  pl.BlockSpec((None, 4, D, tn), lambda j, i: (l, 0, 0, j)),
                  pl.BlockSpec((None, 4, 1, tn), lambda j, i: (l, 0, 0, j)),
                  pl.BlockSpec((None, 4, W4, tn), lambda j, i: (l, 0, 0, j))],
        out_specs=pl.BlockSpec((tm, tn), lambda j, i: (i, j)),
        out_shape=jax.ShapeDtypeStruct((r, D), BF16),
        compiler_params=_cparams("parallel", "parallel"),
        name="gated_sum",
    )(hx, ys, gw, gb, bw)


def _proj_res_kernel(a_ref, w_ref, x_ref, mod_ref, nw_ref, o_ref, h_ref):
    m = mod_ref[...]
    x = x_ref[...] + m[2:3] * _dot(a_ref[...], w_ref[...])
    o_ref[...] = x
    h_ref[...] = _rms_mod(x, nw_ref[...], m, (3, 4)).astype(h_ref.dtype)


def _proj_residual(a, w, l, x, mods, norm_w, tm=512):
    r = a.shape[0]
    tm = min(tm, r)
    per = r // mods.shape[0] // tm
    row = pl.BlockSpec((tm, D), lambda i: (i, 0))
    return pl.pallas_call(
        _proj_res_kernel,
        grid=(r // tm,),
        in_specs=[row, pl.BlockSpec((None, D, D), lambda i: (l, 0, 0), pipeline_mode=pl.Buffered(1)), row,
                  pl.BlockSpec((None, 8, D), lambda i: (i // per, 0, 0)),
                  pl.BlockSpec((1, D), lambda i: (0, 0))],
        out_specs=[row, row],
        out_shape=[jax.ShapeDtypeStruct((r, D), F32), jax.ShapeDtypeStruct((r, D), BF16)],
        compiler_params=_cparams("parallel"),
        name="out_proj",
    )(a, w, x, mods, norm_w.reshape(1, D))


def _ffn_kernel(h_ref, w1_ref, w2_ref, x_ref, mod_ref, nw_ref, nmod_ref, *refs, final):
    o_refs, acc_ref = refs[:-1], refs[-1]
    f = pl.program_id(1)

    @pl.when(f == 0)
    def _():
        acc_ref[...] = jnp.zeros_like(acc_ref)

    a = jnp.maximum(_dot(h_ref[...], w1_ref[...]), 0.0)
    acc_ref[...] += _dot((a * a).astype(BF16), w2_ref[...])

    @pl.when(f == pl.num_programs(1) - 1)
    def _():
        x = x_ref[...] + mod_ref[...][5:6] * acc_ref[...]
        if final:
            o_refs[0][...] = _rms_mod(x, nw_ref[...], None, None)
        else:
            o_refs[0][...] = x
            o_refs[1][...] = _rms_mod(x, nw_ref[...], nmod_ref[...], (0, 1)).astype(BF16)


def _ffn(hn, w1, w2, l, x, mods, next_norm_w, next_mods, final, tm=512, tf=1024):
    r = hn.shape[0]
    tm = min(tm, r)
    per = r // mods.shape[0] // tm
    row = pl.BlockSpec((tm, D), lambda i, f: (i, 0))
    mod_spec = pl.BlockSpec((None, 8, D), lambda i, f: (i // per, 0, 0))
    out_specs = [row] if final else [row, row]
    out_shape = [jax.ShapeDtypeStruct((r, D), F32)] + ([] if final else [jax.ShapeDtypeStruct((r, D), BF16)])
    return pl.pallas_call(
        functools.partial(_ffn_kernel, final=final),
        grid=(r // tm, D_FF // tf),
        in_specs=[row, pl.BlockSpec((None, D, tf), lambda i, f: (l, 0, f)),
                  pl.BlockSpec((None, tf, D), lambda i, f: (l, f, 0)),
                  row, mod_spec, pl.BlockSpec((1, D), lambda i, f: (0, 0)), mod_spec],
        out_specs=out_specs,
        out_shape=out_shape,
        scratch_shapes=[pltpu.VMEM((tm, D), F32)],
        compiler_params=_cparams("parallel", "arbitrary"),
        name="ffn",
    )(hn, w1, w2, x, mods, next_norm_w.reshape(1, D), next_mods)


def _rwkv_chunk(b, z, r, k, v, wd, ad, w_up, w0, a_up, a0, k_k, k_a):
    rev = z == 1
    last = 0 if rev else CH - 1
    lo = lax.broadcasted_iota(jnp.int32, (CH, 128), 1) < 64

    wdz = wd[:, 64 * z:64 * z + 64]
    adz = ad[:, 64 * z:64 * z + 64]
    lw = -RW_DECAY_SCALE * _sigmoid(_dot(jnp.tanh(wdz), w_up, HI) + w0)
    a = _sigmoid(_dot(adz, a_up, HI) + a0)
    kk = k * k_k
    kd = k * (1.0 + (a - 1.0) * k_a)

    lc_incl = _cumdot(_scan_mask(CH, True, rev), lw)
    p_incl = jnp.exp(lc_incl)
    p_inv = jnp.exp(-lc_incl)
    p_excl = jnp.exp(lc_incl - lw)
    p_end = p_incl[last:last + 1]

    chains = []
    for p in range(RW_HEADS // 2):
        sl = slice(128 * p, 128 * (p + 1))
        kkp = kk[:, sl]
        sq = kkp * kkp
        nrm = jnp.where(lo, jnp.sum(jnp.where(lo, sq, 0.0), axis=-1, keepdims=True),
                        jnp.sum(jnp.where(lo, 0.0, sq), axis=-1, keepdims=True))
        kkp = kkp / jnp.maximum(jnp.sqrt(nrm), 1e-12)
        at = -kkp * p_excl[:, sl]
        bt = kkp * a[:, sl] * p_inv[:, sl]
        kt = kd[:, sl] * p_inv[:, sl]
        rt = r[:, sl] * p_incl[:, sl]
        chains.append(dict(
            z=z, key=(b, z), idx=(b, z, p), p_end=p_end[:, sl],
            ar=_b(jnp.concatenate([at, rt], axis=0)),
            bk=_b(jnp.concatenate([_stack_heads(bt), _stack_heads(kt)], axis=0)),
            vs=_b(_stack_heads(v[:, sl]))))
    return chains


def _row_scan_mask(inclusive, reverse):
    r = lax.broadcasted_iota(jnp.int32, (CH, 2 * CH), 0)
    c = lax.broadcasted_iota(jnp.int32, (CH, 2 * CH), 1) & (CH - 1)
    if reverse:
        return (r <= c) if inclusive else (r < c)
    return (r >= c) if inclusive else (r > c)


def _rwkv_solve(chains, s_ref):
    n2 = 2 * CH
    masks = {z: (_row_scan_mask(True, z == 1), _row_scan_mask(False, z == 1)) for z in (0, 1)}
    stack = lambda x: _b(_stack_heads(x))
    s0 = [s_ref[c["idx"]] for c in chains]
    g = [_dot_nt(c["ar"], c["bk"]) for c in chains]
    s0t = [_dot_nt(c["ar"], _b(s)) for c, s in zip(chains, s0)]
    yield
    lp = [jnp.where(masks[c["z"]][1], gi[:CH, :n2], 0.0) for c, gi in zip(chains, g)]
    u = [st[:CH] + _dot(_b(jnp.where(masks[c["z"]][1], gi[:CH, n2:], 0.0)), c["vs"])
         for c, gi, st in zip(chains, g, s0t)]
    for j in range(6):
        lb = [_b(li) for li in lp]
        u = [ui + _dot(li, stack(ui)) for ui, li in zip(u, lb)]
        if j < 5:
            lp = [_dot(li, stack(lf)) for li, lf in zip(lb, lp)]
        yield
    ys = {}
    for c, gi, st, ui, s in zip(chains, g, s0t, u, s0):
        m_incl = masks[c["z"]][0]
        uv = jnp.concatenate([stack(ui), c["vs"]], axis=0)
        rbk = _b(jnp.concatenate([jnp.where(m_incl, gi[CH:, :n2], 0.0), jnp.where(m_incl, gi[CH:, n2:], 0.0)], axis=1))
        ys.setdefault(c["key"], []).append(st[CH:] + _dot(rbk, uv))
        s_ref[c["idx"]] = (s + _dot_tn(uv, c["bk"])) * c["p_end"]
    return {key: jnp.concatenate(parts, axis=1) for key, parts in ys.items()}


def _seq_block_specs(n_b, tb, nblk, cols):
    fwd = [pl.BlockSpec((n_b, tb, w), functools.partial(lambda j, c: (0, j, c), c=c)) for w, c in cols]
    bwd = [pl.BlockSpec((n_b, tb, w), functools.partial(lambda j, c: (0, nblk - 1 - j, c), c=c)) for w, c in cols]
    return fwd, bwd


def _run_stages(gens):
    out = [None] * len(gens)
    live = list(range(len(gens)))
    while live:
        for i in list(live):
            try:
                next(gens[i])
            except StopIteration as done:
                out[i] = done.value
                live.remove(i)
    return out


def _lru_kernel(xf, xb, tail, head, conv_w, conv_b, wa, ba, wx, bx, lam, h0_ref,
                hf_ref, hb_ref, h_ref, ext, a_s, b_s, *, n_b, t, wpb, nscan, wmajor):
    j = pl.program_id(0)
    nstep = nscan // wpb

    def rows(ref, b, w, sl):
        return ref[b, w, sl, :] if wmajor else ref[b, sl, w, :]

    def put(ref, b, w, sl, val):
        if wmajor:
            ref[b, w, sl, :] = val
        else:
            ref[b, sl, w, :] = val

    @pl.when(j == 0)
    def _():
        h_ref[...] = h0_ref[...]

    for wi in range(wpb):
        local = (wi, wpb - 1 - wi)
        glob = (j * wpb + wi, (nstep - 1 - j) * wpb + wpb - 1 - wi)
        for z, xr in enumerate((xf, xb)):
            w, g = local[z], glob[z]
            for b in range(n_b):
                if w > 0:
                    prev2 = rows(xr, b, w - 1, slice(t - 2, t))
                else:
                    prev2 = jnp.where(g > 0, tail[b, jnp.maximum(g - 1, 0)], 0.0)
                if w < wpb - 1:
                    next1 = rows(xr, b, w + 1, slice(0, 1))
                else:
                    next1 = jnp.where(g < nscan - 1, head[b, pl.ds(jnp.minimum(g + 1, nscan - 1), 1), :], 0.0)
                ext[b, z, 6:8, :] = prev2
                ext[b, z, 8:8 + t, :] = rows(xr, b, w, slice(None))
                ext[b, z, 8 + t:9 + t, :] = next1
                xc = conv_b[...] + ext[b, z, 6:6 + t, :] * conv_w[0:1, :]
                for tap in range(1, 4):
                    xc = xc + ext[b, z, 6 + tap:6 + tap + t, :] * conv_w[tap:tap + 1, :]
                xcb = _b(xc)
                rg = _sigmoid(_dot(xcb, wa[z]) + ba[z])
                ig = _sigmoid(_dot(xcb, wx[z]) + bx[z])
                log_a = -LRU_C * rg * _softplus(-lam[z])
                a = jnp.exp(log_a)
                a_s[b, z] = a
                b_s[b, z] = jnp.sqrt(-jnp.tanh(log_a) * (a * a + 1.0)) * ig * xc

        def step(i, carry):
            rf = pl.ds(i, 1)
            rb = pl.ds(t - 1 - i, 1)
            out = []
            for b in range(n_b):
                h_f = a_s[b, 0, rf, :] * carry[2 * b] + b_s[b, 0, rf, :]
                h_b = a_s[b, 1, rb, :] * carry[2 * b + 1] + b_s[b, 1, rb, :]
                put(hf_ref, b, local[0], rf, h_f)
                put(hb_ref, b, local[1], rb, h_b)
                out += [h_f, h_b]
            return tuple(out)

        init = tuple(h_ref[b, z:z + 1, :] for b in range(n_b) for z in (0, 1))
        fin = lax.fori_loop(0, t, step, init, unroll=8)
        for b in range(n_b):
            h_ref[b, 0:1, :] = fin[2 * b]
            h_ref[b, 1:2, :] = fin[2 * b + 1]


def _lru(x4, tail, head, lru_w, h0, wpb, wmajor):
    n_b = x4.shape[0]
    nscan, t = (x4.shape[1], x4.shape[2]) if wmajor else (x4.shape[2], x4.shape[1])
    nstep = nscan // wpb
    shp = (n_b, wpb, t, W4) if wmajor else (n_b, t, wpb, W4)
    at = (lambda j: (0, j, 0, 0)) if wmajor else (lambda j: (0, 0, j, 0))
    at_rev = (lambda j: (0, nstep - 1 - j, 0, 0)) if wmajor else (lambda j: (0, 0, nstep - 1 - j, 0))
    full = lambda s: pl.BlockSpec(s, lambda j: (0,) * len(s))
    out_sds = jax.ShapeDtypeStruct(x4.shape[:3] + (W4,), F32)
    return pl.pallas_call(
        functools.partial(_lru_kernel, n_b=n_b, t=t, wpb=wpb, nscan=nscan, wmajor=wmajor),
        grid=(nstep,),
        in_specs=[pl.BlockSpec(shp, at), pl.BlockSpec(shp, at_rev), full(tail.shape), full(head.shape),
                  full((4, W4)), full((1, W4)), full((2, W4, W4)), full((2, 1, W4)),
                  full((2, W4, W4)), full((2, 1, W4)), full((2, 1, W4)), full(h0.shape)],
        out_specs=[pl.BlockSpec(shp, at), pl.BlockSpec(shp, at_rev), full(h0.shape)],
        out_shape=[out_sds, out_sds, jax.ShapeDtypeStruct(h0.shape, F32)],
        scratch_shapes=[pltpu.VMEM((n_b, 2, t + 16, W4), F32), pltpu.VMEM((n_b, 2, t, W4), F32),
                        pltpu.VMEM((n_b, 2, t, W4), F32)],
        compiler_params=_cparams("arbitrary"),
        name="rglru",
    )(x4, x4, tail, head, *lru_w, h0)


def _gla_chunk(b, z, q, k, v, wd, w_up, w0):
    rev = z == 1
    last = 0 if rev else CH - 1
    pm_incl = _pair_scan_mask(True, rev)
    la = _log_sigmoid(_dot(wd[:, GLA_RANK * z:GLA_RANK * (z + 1)], w_up, HI) + w0) * (1.0 / GLA_TAU)
    bcum = _cumdot(_scan_mask(CH, True, rev), la)
    b_end = bcum[last:last + 1]
    qe = q * (GLA_DK ** -0.5) * jnp.exp(bcum)
    ke = k * jnp.exp(-bcum)
    kd = k * jnp.exp(b_end - bcum)
    e_end = jnp.exp(b_end)
    chains = []
    for p in range(GLA_HEADS // 2):
        sl = slice(128 * p, 128 * (p + 1))
        chains.append(dict(
            key=(b, z), idx=(b, z, p), mask=pm_incl, e_end=e_end[:, sl],
            qs=_b(_stack_heads(qe[:, sl])),
            ks=_b(_stack_heads(ke[:, sl])),
            kds=_b(_stack_heads(kd[:, sl])),
            vs=_b(jnp.concatenate([v[:, 2 * GLA_DV * p:2 * GLA_DV * p + GLA_DV],
                                   v[:, 2 * GLA_DV * p + GLA_DV:2 * GLA_DV * (p + 1)]], axis=0))))
    return chains


def _gla_solve(chains, s_ref):
    st = [s_ref[c["idx"]] for c in chains]
    att = [_b(jnp.where(c["mask"], _dot_nt(c["qs"], c["ks"]), 0.0)) for c in chains]
    inter = [_dot_nt(c["qs"], _b(s)) for c, s in zip(chains, st)]
    yield
    outs = {}
    for c, a, i, s in zip(chains, att, inter, st):
        o = _dot(a, c["vs"]) + i
        outs.setdefault(c["key"], []).extend([o[:CH], o[CH:]])
        s_ref[c["idx"]] = s * c["e_end"] + _dot_tn(c["vs"], c["kds"])
    return {key: jnp.concatenate(parts, axis=1) for key, parts in outs.items()}


def _mlstm_chunk(b, z, q, k, v, gates, bias):
    rev = z == 1
    m_incl = _scan_mask(CH, True, rev)
    last = 0 if rev else CH - 1
    pre = gates + bias
    lf = _log_sigmoid(pre)
    fcum = _cumdot(m_incl, lf)
    pre_t = pre.T
    fcum_t = fcum.T
    k = k * (ML_DH ** -0.5)
    chains = []
    for h in range(ML_HEADS):
        ci = ML_HEADS * z + h
        cf = 2 * ML_HEADS + ML_HEADS * z + h
        sl = slice(ML_DH * h, ML_DH * (h + 1))
        fc = fcum[:, cf:cf + 1]
        ic = pre[:, ci:ci + 1]
        fr = fcum_t[cf:cf + 1, :]
        ir = pre_t[ci:ci + 1, :]
        chains.append(dict(key=(b, z), idx=(b, z, h), q=q[:, sl], k=k[:, sl], v=v[:, sl],
                           qb=_b(q[:, sl]), kb=_b(k[:, sl]),
                           fc=fc, ic=ic, f_end=fc[last:last + 1],
                           dlog=jnp.where(m_incl, fc + (ir - fr), NEG)))
    return chains


def _mlstm_solve(chains, cm_ref, n_ref, m_ref):
    cm = [cm_ref[c["idx"]] for c in chains]
    n = [n_ref[c["idx"]] for c in chains]
    m = [m_ref[c["idx"]] for c in chains]
    qk = [_dot_nt(c["qb"], c["kb"]) for c in chains]
    qc = [_dot_nt(c["qb"], _b(s)) for c, s in zip(chains, cm)]
    yield
    inter = [c["fc"] + mi for c, mi in zip(chains, m)]
    mt = [jnp.maximum(i, jnp.max(c["dlog"], axis=-1, keepdims=True)) for c, i in zip(chains, inter)]
    w = [jnp.exp(c["dlog"] - t) * s for c, t, s in zip(chains, mt, qk)]
    ei = [jnp.exp(i - t) for i, t in zip(inter, mt)]
    yield
    wv = [_dot(_b(wi), _b(c["v"])) for c, wi in zip(chains, w)]
    yield
    outs = {}
    for c, wi, e, t, a, qci, cmi, ni, mi in zip(chains, w, ei, mt, wv, qc, cm, n, m):
        num = a + e * qci
        den = jnp.sum(wi, axis=-1, keepdims=True) + e * jnp.sum(c["q"] * ni, axis=-1, keepdims=True)
        outs.setdefault(c["key"], []).append(num / jnp.maximum(jnp.abs(den), jnp.exp(-t)))
        g = c["f_end"] - c["fc"] + c["ic"]
        m_new = jnp.maximum(c["f_end"] + mi, jnp.max(g, axis=0, keepdims=True))
        dec = jnp.exp(c["f_end"] + mi - m_new)
        wg = jnp.exp(g - m_new)
        cm_ref[c["idx"]] = dec * cmi + _dot_tn(_b(c["v"] * wg), c["kb"])
        n_ref[c["idx"]] = dec * ni + jnp.sum(c["k"] * wg, axis=0, keepdims=True)
        m_ref[c["idx"]] = m_new
    return {key: jnp.concatenate(parts, axis=1) for key, parts in outs.items()}


_RW_SEQ = [(0, W4, 0), (0, W4, 1), (0, W4, 2), (1, 128, 0), (1, 128, 1)]
_GLA_SEQ = [(0, 256, 0), (0, 256, 1), (0, 512, 1), (1, 128, 0)]
_ML_SEQ = [(0, W4, 0), (0, W4, 1), (0, W4, 2), (1, 128, 0)]


def _mix_kernel(*refs, n_b, nchunk):
    it = iter(refs)
    take = lambda n: [next(it) for _ in range(n)]
    rw_seq = (take(len(_RW_SEQ)), take(len(_RW_SEQ)))
    gla_seq = (take(len(_GLA_SEQ)), take(len(_GLA_SEQ)))
    ml_seq = (take(len(_ML_SEQ)), take(len(_ML_SEQ)))
    rw_par, gla_par, (ml_bias,) = take(6), take(2), take(1)
    init = take(5)
    outs = (take(2), take(2), take(2))
    state = take(5)
    rw_s, gla_s, cm_ref, n_ref, m_ref = state

    @pl.when(pl.program_id(0) == 0)
    def _():
        for dst, src in zip(state, init):
            dst[...] = src[...]

    def body(rows_f, rows_b):
        load = lambda ref, b, rows: ref[b, rows, :].astype(F32)
        rw, gla, ml = [], [], []
        for b in range(n_b):
            for z, rows in enumerate((rows_f, rows_b)):
                w_up, w0, a_up, a0, k_k, k_a = rw_par
                rw += _rwkv_chunk(b, z, *(load(ref, b, rows) for ref in rw_seq[z]),
                                  w_up[z], w0[z], a_up[z], a0[z], k_k[...], k_a[...])
                gla += _gla_chunk(b, z, *(load(ref, b, rows) for ref in gla_seq[z]), gla_par[0][z], gla_par[1][z])
                ml += _mlstm_chunk(b, z, *(load(ref, b, rows) for ref in ml_seq[z]), ml_bias[...])
        results = _run_stages([_rwkv_solve(rw, rw_s), _mlstm_solve(ml, cm_ref, n_ref, m_ref), _gla_solve(gla, gla_s)])
        for (o_f, o_b), res in zip((outs[0], outs[2], outs[1]), results):
            for b in range(n_b):
                o_f[b, rows_f, :] = res[(b, 0)].astype(o_f.dtype)
                o_b[b, rows_b, :] = res[(b, 1)].astype(o_b.dtype)

    _for_chunks(nchunk, body)


def _mix(u_rw, u_gla, u_ml, rw_par, gla_par, ml_bias, states, tb=128):
    n_b, length, _ = u_rw[0].shape
    nblk = length // tb
    full = lambda a: pl.BlockSpec(a.shape, lambda j: (0,) * a.ndim)
    seq_specs, seq_args = [], []
    for u, cols in ((u_rw, _RW_SEQ), (u_gla, _GLA_SEQ), (u_ml, _ML_SEQ)):
        fwd, bwd = _seq_block_specs(n_b, tb, nblk, [(w, c) for _, w, c in cols])
        seq_specs += fwd + bwd
        seq_args += [u[a] for a, _, _ in cols] * 2
    params = list(rw_par) + list(gla_par) + [ml_bias]
    out_f, out_b = _seq_block_specs(n_b, tb, nblk, [(W4, 0)])
    out_sds = jax.ShapeDtypeStruct((n_b, length, W4), BF16)
    res = pl.pallas_call(
        functools.partial(_mix_kernel, n_b=n_b, nchunk=tb // CH),
        grid=(nblk,),
        in_specs=seq_specs + [full(a) for a in params + list(states)],
        out_specs=[out_f[0], out_b[0]] * 3 + [full(a) for a in states],
        out_shape=[out_sds] * 6 + [jax.ShapeDtypeStruct(a.shape, F32) for a in states],
        compiler_params=_cparams("arbitrary"),
        name="mix3",
    )(*seq_args, *params, *states)
    return res[0:2], res[2:4], res[4:6], res[6:]


def _branch_kernel(yaf, yab, rr, rk, rv, rg, hbf, hbb, lg, ocf, ocb, gr, odf, odb, mo,
                   rw_ln, rw_rk, g_up, gla_ln, ml_ln, e64, e128, o_ref):
    f = lambda ref: ref[...].astype(F32)
    y = f(yaf) + f(yab)
    y = y * lax.rsqrt(_headsum(y * y, e64[...]) * (1.0 / RW_HEAD) + EPS) * rw_ln[...]
    bonus = _headsum(f(rr) * f(rk) * rw_rk[...], e64[...]) * f(rv)
    g = _dot(_sigmoid(rg[...]).astype(BF16), g_up[...])
    o_ref[:, 0:W4] = ((y + bonus) * g).astype(o_ref.dtype)
    o_ref[:, W4:2 * W4] = ((hbf[...] + hbb[...]) * jax.nn.gelu(lg[...])).astype(o_ref.dtype)
    y = f(ocf) + f(ocb)
    y = y * lax.rsqrt(_headsum(y * y, e128[...]) * (1.0 / GLA_DV) + EPS) * gla_ln[...]
    r = f(gr)
    o_ref[:, 2 * W4:3 * W4] = (y * (r * _sigmoid(r))).astype(o_ref.dtype)
    y = f(odf) + f(odb)
    y = y * lax.rsqrt(_headsum(y * y, e128[...]) * (1.0 / ML_DH) + EPS) * ml_ln[...]
    o_ref[:, 3 * W4:4 * W4] = (y * _sigmoid(f(mo))).astype(o_ref.dtype)


def _branches(u_rw, u_lru, u_gla, u_ml, ya, hb, oc, od, rw_ln, rw_rk, g_up, gla_ln, ml_ln, e64, e128, tm=256):
    r = u_lru.shape[0]
    row = lambda w, c: pl.BlockSpec((tm, w), lambda i: (i, c))
    full = lambda shp: pl.BlockSpec(shp, lambda i: (0,) * len(shp))
    return pl.pallas_call(
        _branch_kernel,
        grid=(r // tm,),
        in_specs=[row(W4, 0), row(W4, 0), row(W4, 0), row(W4, 1), row(W4, 2), row(128, 2),
                  row(W4, 0), row(W4, 0), row(W4, 1),
                  row(W4, 0), row(W4, 0), row(W4, 2),
                  row(W4, 0), row(W4, 0), row(W4, 3),
                  full((1, W4)), full((1, W4)), full((128, W4)), full((1, W4)), full((1, W4)),
                  full((W4, W4)), full((W4, W4))],
        out_specs=pl.BlockSpec((tm, D), lambda i: (i, 0)),
        out_shape=jax.ShapeDtypeStruct((r, D), BF16),
        compiler_params=_cparams("parallel"),
        name="branches",
    )(ya[0], ya[1], u_rw[0], u_rw[0], u_rw[0], u_rw[1], hb[0], hb[1], u_lru, oc[0], oc[1], u_gla[0],
      od[0], od[1], u_ml[0], rw_ln, rw_rk, g_up, gla_ln, ml_ln, e64, e128)


def _block_ones(width, head):
    i = np.arange(width) // head
    return jnp.asarray((i[:, None] == i[None, :]).astype(np.float32)).astype(BF16)


def _block_diag(w):
    z, nb, c, _ = w.shape
    eye = jnp.eye(nb, dtype=w.dtype)
    return jnp.einsum("znij,nm->znimj", w, eye).reshape(z, nb * c, nb * c)


N_MAIN = (3 * W4, 0, 256 + 256 + 512 + 512, 4 * W4)


def _regroup_w_in(w):
    sizes = [512, 512, 512, 128, 128, 128, 512, 512, 256, 256, 512, 512, 32, 512, 512, 512, 512, 8, 8]
    offs = np.concatenate([[0], np.cumsum(sizes)])
    pad = lambda n: jnp.zeros(w.shape[:2] + (n,), w.dtype)
    rw = w[:, :, offs[0]:offs[6]]
    lru = w[:, :, offs[6]:offs[8]]
    gla = jnp.concatenate([w[:, :, offs[8]:offs[13]], pad(96)], axis=2)
    ml = jnp.concatenate([w[:, :, offs[13]:offs[19]], pad(112)], axis=2)
    return [rw, lru, gla, ml]


def _mixers(hx, n_b, l, lw, states, latent):
    r = hx.shape[0]
    length = r // n_b
    u_rw, (u_lru,), u_gla, u_ml = (_mm(hx, w, l, n_main) for w, n_main in zip(lw["w_in"], N_MAIN))
    seq = lambda us: [u.reshape(n_b, length, u.shape[-1]) for u in us]
    flat = lambda u: u.reshape(r, u.shape[-1])
    s_rw, s_lru, s_gla, s_ml = states

    (ya_f, ya_b), (oc_f, oc_b), (od_f, od_b), (s_rw, s_gla, *s_ml) = _mix(
        seq(u_rw), seq(u_gla), seq(u_ml), lw["rw"], lw["gla"], lw["ml_bias"], [s_rw, s_gla, *s_ml])
    if latent:
        rows = length // GRID_W
        x4 = u_lru.reshape(n_b, rows, GRID_W, LRU_COLS)
        tail = jnp.swapaxes(x4[:, rows - 2:rows, :, :W4], 1, 2)
        head = x4[:, 0, :, :W4]
        hb_f, hb_b, s_lru = _lru(x4, tail, head, lw["lru"], s_lru, wpb=8, wmajor=False)
    else:
        x4 = u_lru.reshape(n_b, 1, length, LRU_COLS)
        hb_f, hb_b, s_lru = _lru(x4, jnp.zeros((n_b, 1, 2, W4), F32), jnp.zeros((n_b, 1, W4), F32), lw["lru"], s_lru,
                                 wpb=1, wmajor=True)
    ys = _branches(u_rw, u_lru, u_gla, u_ml, (flat(ya_f), flat(ya_b)), (flat(hb_f), flat(hb_b)),
                   (flat(oc_f), flat(oc_b)), (flat(od_f), flat(od_b)), *lw["branch"])
    return ys, (s_rw, s_lru, s_gla, tuple(s_ml))


def _zero_states(n_b):
    return (jnp.zeros((n_b, 2, RW_HEADS // 2, 128, 128), F32),
            jnp.zeros((n_b, 2, W4), F32),
            jnp.zeros((n_b, 2, GLA_HEADS // 2, GLA_DV, 2 * GLA_DK), F32),
            (jnp.zeros((n_b, 2, ML_HEADS, ML_DH, ML_DH), F32),
             jnp.zeros((n_b, 2, ML_HEADS, 1, ML_DH), F32),
             jnp.full((n_b, 2, ML_HEADS, 1, 1), NEG, F32)))


def kernel(x, c, ctx, c_ctx, ada_w, ada_b, norm_mix_w, w_in, rw_w_up, rw_w0, rw_a_up, rw_a0, rw_g_up, rw_k_k, rw_k_a, rw_r_k, rw_ln_w, lru_conv_w, lru_conv_b, lru_w_a, lru_b_a, lru_w_x, lru_b_x, lru_lambda, gla_w_up, gla_w0, gla_ln_w, ml_i_b, ml_f_b, ml_ln_w, br_w, gate_w, gate_b, out_w, norm_ffn_w, ffn_w1, ffn_w2, final_norm_w):
    n_b, seq, _ = x.shape
    n_c, ctx_len, _ = ctx.shape
    xs = x.reshape(n_b * seq, D)
    cs = ctx.reshape(n_c * ctx_len, D)
    cc = jnp.zeros((8, D), F32).at[0:n_b].set(c).at[n_b].set(c_ctx)
    e64 = _block_ones(W4, RW_HEAD)
    e128 = _block_ones(W4, GLA_DV)
    row = lambda v: v.reshape(1, -1)

    mods = jnp.pad(_ada(cc, ada_w, ada_b).reshape(DEPTH, 8, 6, D), ((0, 0), (0, 0), (0, 2), (0, 0)))
    mod_x = [mods[l, 0:n_b] for l in range(DEPTH)]
    mod_c = [mods[l, n_b:n_b + 1] for l in range(DEPTH)]
    hx = _norm(xs, norm_mix_w[0], mod_x[0], (0, 1), BF16)
    hc = _norm(cs, norm_mix_w[0], mod_c[0], (0, 1), BF16)
    w_in_groups = _regroup_w_in(w_in.astype(BF16))
    gw, bw, ow = gate_w.astype(BF16), br_w.astype(BF16), out_w.astype(BF16)
    gb = gate_b[:, :, None, :]
    w1, w2 = ffn_w1.astype(BF16), ffn_w2.astype(BF16)

    for l in range(DEPTH):
        last = l == DEPTH - 1
        lw = {
            "w_in": w_in_groups,
            "rw": (rw_w_up[l], rw_w0[l][:, None, :], rw_a_up[l], rw_a0[l][:, None, :],
                   row(rw_k_k[l]), row(rw_k_a[l])),
            "lru": (lru_conv_w[l], row(lru_conv_b[l]), _block_diag(lru_w_a[l]).astype(BF16), lru_b_a[l][:, None, :],
                    _block_diag(lru_w_x[l]).astype(BF16), lru_b_x[l][:, None, :], lru_lambda[l][:, None, :]),
            "gla": (gla_w_up[l], gla_w0[l][:, None, :]),
            "ml_bias": jnp.zeros((1, 128), F32).at[0, 0:8].set(ml_i_b[l].reshape(-1)).at[0, 8:16].set(ml_f_b[l].reshape(-1)),
            "branch": (row(rw_ln_w[l]), row(rw_r_k[l]), rw_g_up[l].astype(BF16), row(gla_ln_w[l]), row(ml_ln_w[l]),
                       e64, e128),
        }
        ys_c, st_c = _mixers(hc, n_c, l, lw, _zero_states(n_c), latent=False)
        ys_x, _ = _mixers(hx, n_b, l, lw, st_c, latent=True)
        xs, hn = _proj_residual(_gated_sum(hx, ys_x, gw, gb, bw, l), ow, l, xs, mod_x[l], norm_ffn_w[l])
        if last:
            out, = _ffn(hn, w1, w2, l, xs, mod_x[l], final_norm_w, mod_x[l], final=True)
        else:
            xs, hx = _ffn(hn, w1, w2, l, xs, mod_x[l], norm_mix_w[l + 1], mod_x[l + 1], final=False)
            cs, hn = _proj_residual(_gated_sum(hc, ys_c, gw, gb, bw, l), ow, l, cs, mod_c[l], norm_ffn_w[l])
            cs, hc = _ffn(hn, w1, w2, l, cs, mod_c[l], norm_mix_w[l + 1], mod_c[l + 1], final=False)
    return out.reshape(n_b, seq, D)
```

```python
import functools

import numpy as np
import jax
import jax.numpy as jnp
from jax import lax
from jax.experimental import pallas as pl
from jax.experimental.pallas import tpu as pltpu

F32 = jnp.float32
BF16 = jnp.bfloat16
HI = lax.Precision.HIGHEST

D = 2048
DEPTH = 2
GRID_W = 64
EPS = 1e-6
NEG = -1e30
CH = 64
D_FF = 4 * D
W4 = D // 4

RW_HEAD = 64
RW_HEADS = W4 // RW_HEAD
RW_DECAY_SCALE = 0.6065306597
LRU_BLOCKS = 8
LRU_C = 8.0
GLA_HEADS = 4
GLA_DK = 64
GLA_DV = 128
GLA_RANK = 16
GLA_TAU = 16.0
ML_HEADS = 4
ML_DH = 128

RW_COLS = 3 * W4 + 3 * 128
LRU_COLS = 2 * W4
GLA_COLS = 256 + 256 + 512 + 512 + 128
ML_COLS = 4 * W4 + 128

VMEM_LIMIT = 56 * 1024 * 1024


def _cparams(*sem):
    return pltpu.CompilerParams(dimension_semantics=sem, vmem_limit_bytes=VMEM_LIMIT)


def _dot(a, b, prec=None):
    return jnp.dot(a, b, precision=prec, preferred_element_type=F32)


def _dot_nt(a, b, prec=None):
    return lax.dot_general(a, b, (((1,), (1,)), ((), ())), precision=prec, preferred_element_type=F32)


def _dot_tn(a, b, prec=None):
    return lax.dot_general(a, b, (((0,), (0,)), ((), ())), precision=prec, preferred_element_type=F32)


def _b(x):
    return x.astype(BF16)


def _bdot(a, b):
    return jnp.dot(_b(a), _b(b), preferred_element_type=F32)


def _bdot_nt(a, b):
    return _dot_nt(_b(a), _b(b))


def _bdot_tn(a, b):
    return _dot_tn(_b(a), _b(b))


def _split_bf16(x, parts):
    out = []
    for i in range(parts):
        hi = _b(x)
        out.append(hi)
        if i < parts - 1:
            x = x - hi.astype(F32)
    return out


def _cumdot(reverse, x):
    r = lax.broadcasted_iota(jnp.int32, (CH, 4 * CH), 0)
    c = lax.broadcasted_iota(jnp.int32, (CH, 4 * CH), 1)
    s = c & (CH - 1)
    m = jnp.where(c < 3 * CH, jnp.where((r <= s) if reverse else (r >= s), 1.0, 0.0), 0.0).astype(BF16)
    parts = _split_bf16(x, 3)
    return _dot(m, jnp.concatenate(parts + [jnp.zeros_like(parts[0])], axis=0))


def _dot_hilo(x, w_hh, w_lo):
    hi, lo = _split_bf16(x, 2)
    return _dot(jnp.concatenate([hi, lo], axis=1), w_hh) + _dot(hi, w_lo)


def _headsum(x, e):
    hi, lo = _split_bf16(x, 2)
    return _dot(hi, e) + _dot(lo, e)


def _stack_heads(x):
    lo = lax.broadcasted_iota(jnp.int32, x.shape, 1) < 64
    return jnp.concatenate([jnp.where(lo, x, 0.0), jnp.where(lo, 0.0, x)], axis=0)


def _pair_scan_mask(inclusive, reverse):
    r = lax.broadcasted_iota(jnp.int32, (2 * CH, 2 * CH), 0) & (CH - 1)
    c = lax.broadcasted_iota(jnp.int32, (2 * CH, 2 * CH), 1) & (CH - 1)
    if reverse:
        return (r <= c) if inclusive else (r < c)
    return (r >= c) if inclusive else (r > c)


def _for_chunks(nchunk, body):
    for ci in range(nchunk):
        body(slice(ci * CH, (ci + 1) * CH), slice((nchunk - 1 - ci) * CH, (nchunk - ci) * CH))


def _sigmoid(x):
    return 1.0 / (1.0 + jnp.exp(-x))


def _log_sigmoid(x):
    return jnp.minimum(x, 0.0) - jnp.log1p(jnp.exp(-jnp.abs(x)))


def _softplus(x):
    return jnp.maximum(x, 0.0) + jnp.log1p(jnp.exp(-jnp.abs(x)))


def _scan_mask(n, inclusive, reverse):
    r = lax.broadcasted_iota(jnp.int32, (n, n), 0)
    c = lax.broadcasted_iota(jnp.int32, (n, n), 1)
    if reverse:
        return (r <= c) if inclusive else (r < c)
    return (r >= c) if inclusive else (r > c)


def _ada_kernel(s_ref, w_ref, b_ref, o_ref):
    s = s_ref[...]
    s = s * _sigmoid(s)
    o_ref[...] = _dot(s, w_ref[...], HI) + b_ref[...]


def _ada(cc, w, b):
    tn = 1024
    n_l = w.shape[0]
    return pl.pallas_call(
        _ada_kernel,
        grid=(n_l, 6 * D // tn),
        in_specs=[pl.BlockSpec((8, D), lambda l, j: (0, 0)),
                  pl.BlockSpec((None, D, tn), lambda l, j: (l, 0, j)),
                  pl.BlockSpec((None, 1, tn), lambda l, j: (l, 0, j))],
        out_specs=pl.BlockSpec((None, 8, tn), lambda l, j: (l, 0, j)),
        out_shape=jax.ShapeDtypeStruct((n_l, 8, 6 * D), F32),
        compiler_params=_cparams("parallel", "parallel"),
        name="ada",
    )(cc, w, b.reshape(n_l, 1, 6 * D))


def _rms_mod(x, w, m, rows):
    y = x * lax.rsqrt(jnp.mean(x * x, axis=-1, keepdims=True) + EPS) * w
    if rows is not None:
        shift, scale = rows
        y = y * (1.0 + m[scale:scale + 1]) + m[shift:shift + 1]
    return y


def _norm_kernel(x_ref, w_ref, mod_ref, o_ref, *, rows):
    o_ref[...] = _rms_mod(x_ref[...], w_ref[...], mod_ref[...], rows).astype(o_ref.dtype)


def _norm(x, w, mods, rows, out_dtype, tm=256):
    r = x.shape[0]
    g = mods.shape[0]
    per = r // g // tm
    return pl.pallas_call(
        functools.partial(_norm_kernel, rows=rows),
        grid=(r // tm,),
        in_specs=[pl.BlockSpec((tm, D), lambda i: (i, 0)),
                  pl.BlockSpec((1, D), lambda i: (0, 0)),
                  pl.BlockSpec((None, 8, D), lambda i: (i // per, 0, 0))],
        out_specs=pl.BlockSpec((tm, D), lambda i: (i, 0)),
        out_shape=jax.ShapeDtypeStruct((r, D), out_dtype),
        compiler_params=_cparams("parallel"),
        name="norm",
    )(x, w.reshape(1, D), mods)


def _mm_kernel(a_ref, w_ref, *o_refs, n_main):
    y = _dot(a_ref[...], w_ref[...])
    if n_main:
        o_refs[0][...] = y[:, :n_main].astype(BF16)
    if n_main < y.shape[1]:
        o_refs[-1][...] = y[:, n_main:]


def _mm(a, w, l, n_main, tm=512):
    r, k = a.shape
    n = w.shape[2]
    tm = min(tm, r)
    widths = [(n_main, BF16)] * bool(n_main) + [(n - n_main, F32)] * bool(n - n_main)
    return pl.pallas_call(
        functools.partial(_mm_kernel, n_main=n_main),
        grid=(r // tm,),
        in_specs=[pl.BlockSpec((tm, k), lambda i: (i, 0)),
                  pl.BlockSpec((None, k, n), lambda i: (l, 0, 0), pipeline_mode=pl.Buffered(1))],
        out_specs=[pl.BlockSpec((tm, wd), lambda i: (i, 0)) for wd, _ in widths],
        out_shape=[jax.ShapeDtypeStruct((r, wd), dt) for wd, dt in widths],
        compiler_params=_cparams("parallel"),
        name="in_proj",
    )(a, w)


def _gate_kernel(hx_ref, ys_ref, gw_ref, gb_ref, bw_ref, o_ref):
    hx = hx_ref[...]
    acc = None
    for i in range(4):
        g = _dot(hx, gw_ref[i]) + gb_ref[i]
        p = _dot(ys_ref[:, i * W4:(i + 1) * W4], bw_ref[i])
        t = _sigmoid(g) * p
        acc = t if acc is None else acc + t
    o_ref[...] = acc.astype(o_ref.dtype)


def _gated_sum(hx, ys, gw, gb, bw, l, tm=512, tn=512):
    r = hx.shape[0]
    tm = min(tm, r)
    return pl.pallas_call(
        _gate_kernel,
        grid=(D // tn, r // tm),
        in_specs=[pl.BlockSpec((tm, D), lambda j, i: (i, 0)),
                  pl.BlockSpec((tm, D), lambda j, i: (i, 0)),
                  pl.BlockSpec((None, 4, D, tn), lambda j, i: (l, 0, 0, j)),
                  pl.BlockSpec((None, 4, 1, tn), lambda j, i: (l, 0, 0, j)),
                  pl.BlockSpec((None, 4, W4, tn), lambda j, i: (l, 0, 0, j))],
        out_specs=pl.BlockSpec((tm, tn), lambda j, i: (i, j)),
        out_shape=jax.ShapeDtypeStruct((r, D), BF16),
        compiler_params=_cparams("parallel", "parallel"),
        name="gated_sum",
    )(hx, ys, gw, gb, bw)


def _proj_res_kernel(a_ref, w_ref, x_ref, mod_ref, nw_ref, o_ref, h_ref):
    m = mod_ref[...]
    x = x_ref[...] + m[2:3] * _dot(a_ref[...], w_ref[...])
    o_ref[...] = x
    h_ref[...] = _rms_mod(x, nw_ref[...], m, (3, 4)).astype(h_ref.dtype)


def _proj_residual(a, w, l, x, mods, norm_w, tm=512):
    r = a.shape[0]
    tm = min(tm, r)
    per = r // mods.shape[0] // tm
    row = pl.BlockSpec((tm, D), lambda i: (i, 0))
    return pl.pallas_call(
        _proj_res_kernel,
        grid=(r // tm,),
        in_specs=[row, pl.BlockSpec((None, D, D), lambda i: (l, 0, 0), pipeline_mode=pl.Buffered(1)), row,
                  pl.BlockSpec((None, 8, D), lambda i: (i // per, 0, 0)),
                  pl.BlockSpec((1, D), lambda i: (0, 0))],
        out_specs=[row, row],
        out_shape=[jax.ShapeDtypeStruct((r, D), F32), jax.ShapeDtypeStruct((r, D), BF16)],
        compiler_params=_cparams("parallel"),
        name="out_proj",
    )(a, w, x, mods, norm_w.reshape(1, D))


def _ffn_kernel(h_ref, w1_ref, w2_ref, x_ref, mod_ref, nw_ref, nmod_ref, *refs, final):
    o_refs, acc_ref = refs[:-1], refs[-1]
    f = pl.program_id(1)

    @pl.when(f == 0)
    def _():
        acc_ref[...] = jnp.zeros_like(acc_ref)

    a = jnp.maximum(_dot(h_ref[...], w1_ref[...]), 0.0)
    acc_ref[...] += _dot((a * a).astype(BF16), w2_ref[...])

    @pl.when(f == pl.num_programs(1) - 1)
    def _():
        x = x_ref[...] + mod_ref[...][5:6] * acc_ref[...]
        if final:
            o_refs[0][...] = _rms_mod(x, nw_ref[...], None, None)
        else:
            o_refs[0][...] = x
            o_refs[1][...] = _rms_mod(x, nw_ref[...], nmod_ref[...], (0, 1)).astype(BF16)


def _ffn(hn, w1, w2, l, x, mods, next_norm_w, next_mods, final, tm=512, tf=1024):
    r = hn.shape[0]
    tm = min(tm, r)
    per = r // mods.shape[0] // tm
    row = pl.BlockSpec((tm, D), lambda i, f: (i, 0))
    mod_spec = pl.BlockSpec((None, 8, D), lambda i, f: (i // per, 0, 0))
    out_specs = [row] if final else [row, row]
    out_shape = [jax.ShapeDtypeStruct((r, D), F32)] + ([] if final else [jax.ShapeDtypeStruct((r, D), BF16)])
    return pl.pallas_call(
        functools.partial(_ffn_kernel, final=final),
        grid=(r // tm, D_FF // tf),
        in_specs=[row, pl.BlockSpec((None, D, tf), lambda i, f: (l, 0, f)),
                  pl.BlockSpec((None, tf, D), lambda i, f: (l, f, 0)),
                  row, mod_spec, pl.BlockSpec((1, D), lambda i, f: (0, 0)), mod_spec],
        out_specs=out_specs,
        out_shape=out_shape,
        scratch_shapes=[pltpu.VMEM((tm, D), F32)],
        compiler_params=_cparams("parallel", "arbitrary"),
        name="ffn",
    )(hn, w1, w2, x, mods, next_norm_w.reshape(1, D), next_mods)


def _rwkv_chunk(b, z, r, k, v, wd, ad, w_up, w0, a_up, a0, k_k, k_a):
    rev = z == 1
    last = 0 if rev else CH - 1
    lo = lax.broadcasted_iota(jnp.int32, (CH, 128), 1) < 64

    lw = -RW_DECAY_SCALE * _sigmoid(_dot_hilo(jnp.tanh(wd), *w_up) + w0)
    a = _sigmoid(_dot_hilo(ad, *a_up) + a0)
    kk = k * k_k
    kd = k * (1.0 + (a - 1.0) * k_a)

    lc_incl = _cumdot(rev, lw)
    p_incl = jnp.exp(lc_incl)
    p_inv = jnp.exp(-lc_incl)
    p_excl = jnp.exp(lc_incl - lw)
    p_end = p_incl[last:last + 1]

    chains = []
    for p in range(RW_HEADS // 2):
        sl = slice(128 * p, 128 * (p + 1))
        kkp = kk[:, sl]
        sq = kkp * kkp
        nrm = jnp.where(lo, jnp.sum(jnp.where(lo, sq, 0.0), axis=-1, keepdims=True),
                        jnp.sum(jnp.where(lo, 0.0, sq), axis=-1, keepdims=True))
        kkp = kkp / jnp.maximum(jnp.sqrt(nrm), 1e-12)
        at = -kkp * p_excl[:, sl]
        bt = kkp * a[:, sl] * p_inv[:, sl]
        kt = kd[:, sl] * p_inv[:, sl]
        rt = r[:, sl] * p_incl[:, sl]
        chains.append(dict(
            z=z, key=(b, z), idx=(b, z, p), p_end=p_end[:, sl],
            ar=_b(jnp.concatenate([at, rt], axis=0)),
            bk=_b(jnp.concatenate([_stack_heads(bt), _stack_heads(kt)], axis=0)),
            vs=_b(_stack_heads(v[:, sl]))))
    return chains


def _row_scan_mask(inclusive, reverse, blocks=2):
    r = lax.broadcasted_iota(jnp.int32, (CH, blocks * CH), 0)
    c = lax.broadcasted_iota(jnp.int32, (CH, blocks * CH), 1) & (CH - 1)
    if reverse:
        return (r <= c) if inclusive else (r < c)
    return (r >= c) if inclusive else (r > c)


def _rwkv_solve(chains, s_ref):
    n2 = 2 * CH
    masks = {z: (_row_scan_mask(True, z == 1, 4), _row_scan_mask(False, z == 1)) for z in (0, 1)}
    stack = lambda x: _b(_stack_heads(x))
    s0 = [s_ref[c["idx"]] for c in chains]
    g = [_dot_nt(c["ar"], c["bk"]) for c in chains]
    s0t = [_dot_nt(c["ar"], _b(s)) for c, s in zip(chains, s0)]
    yield
    lp = [jnp.where(masks[c["z"]][1], gi[:CH, :n2], 0.0) for c, gi in zip(chains, g)]
    u = [st[:CH] + _dot(_b(jnp.where(masks[c["z"]][1], gi[:CH, n2:], 0.0)), c["vs"])
         for c, gi, st in zip(chains, g, s0t)]
    for j in range(6):
        lb = [_b(li) for li in lp]
        u = [ui + _dot(li, stack(ui)) for ui, li in zip(u, lb)]
        if j < 5:
            lp = [_dot(li, stack(lf)) for li, lf in zip(lb, lp)]
        yield
    ys = {}
    for c, gi, st, ui, s in zip(chains, g, s0t, u, s0):
        uv = jnp.concatenate([stack(ui), c["vs"]], axis=0)
        rbk = _b(jnp.where(masks[c["z"]][0], gi[CH:], 0.0))
        ys.setdefault(c["key"], []).append(st[CH:] + _dot(rbk, uv))
        s_ref[c["idx"]] = (s + _dot_tn(uv, c["bk"])) * c["p_end"]
    return {key: jnp.concatenate(parts, axis=1) for key, parts in ys.items()}


def _seq_block_specs(n_b, tb, nblk, cols):
    fwd = [pl.BlockSpec((n_b, tb, w), functools.partial(lambda j, c: (0, j, c), c=c)) for w, c in cols]
    bwd = [pl.BlockSpec((n_b, tb, w), functools.partial(lambda j, c: (0, nblk - 1 - j, c), c=c)) for w, c in cols]
    return fwd, bwd


def _run_stages(gens):
    out = [None] * len(gens)
    live = list(range(len(gens)))
    while live:
        for i in list(live):
            try:
                next(gens[i])
            except StopIteration as done:
                out[i] = done.value
                live.remove(i)
    return out


def _lru_kernel(xf, xb, tail, head, conv_w, conv_b, wa, ba, wx, bx, lam, h0_ref,
                hf_ref, hb_ref, h_ref, ext, a_s, b_s, *, n_b, t, wpb, nscan, wmajor):
    j = pl.program_id(0)
    nstep = nscan // wpb

    def rows(ref, b, w, sl):
        return ref[b, w, sl, :] if wmajor else ref[b, sl, w, :]

    def put(ref, b, w, sl, val):
        if wmajor:
            ref[b, w, sl, :] = val
        else:
            ref[b, sl, w, :] = val

    @pl.when(j == 0)
    def _():
        h_ref[...] = h0_ref[...]

    for wi in range(wpb):
        local = (wi, wpb - 1 - wi)
        glob = (j * wpb + wi, (nstep - 1 - j) * wpb + wpb - 1 - wi)
        for z, xr in enumerate((xf, xb)):
            w, g = local[z], glob[z]
            for b in range(n_b):
                if w > 0:
                    prev2 = rows(xr, b, w - 1, slice(t - 2, t))
                else:
                    prev2 = jnp.where(g > 0, tail[b, jnp.maximum(g - 1, 0)], 0.0)
                if w < wpb - 1:
                    next1 = rows(xr, b, w + 1, slice(0, 1))
                else:
                    next1 = jnp.where(g < nscan - 1, head[b, pl.ds(jnp.minimum(g + 1, nscan - 1), 1), :], 0.0)
                ext[b, z, 6:8, :] = prev2
                ext[b, z, 8:8 + t, :] = rows(xr, b, w, slice(None))
                ext[b, z, 8 + t:9 + t, :] = next1
                xc = conv_b[...] + ext[b, z, 6:6 + t, :] * conv_w[0:1, :]
                for tap in range(1, 4):
                    xc = xc + ext[b, z, 6 + tap:6 + tap + t, :] * conv_w[tap:tap + 1, :]
                xcb = _b(xc)
                rg = _sigmoid(_dot(xcb, wa[z]) + ba[z])
                ig = _sigmoid(_dot(xcb, wx[z]) + bx[z])
                log_a = -LRU_C * rg * _softplus(-lam[z])
                a = jnp.exp(log_a)
                a_s[b, z] = a
                b_s[b, z] = jnp.sqrt(-jnp.tanh(log_a) * (a * a + 1.0)) * ig * xc

        def step(i, carry):
            rf = pl.ds(i, 1)
            rb = pl.ds(t - 1 - i, 1)
            out = []
            for b in range(n_b):
                h_f = a_s[b, 0, rf, :] * carry[2 * b] + b_s[b, 0, rf, :]
                h_b = a_s[b, 1, rb, :] * carry[2 * b + 1] + b_s[b, 1, rb, :]
                put(hf_ref, b, local[0], rf, h_f)
                put(hb_ref, b, local[1], rb, h_b)
                out += [h_f, h_b]
            return tuple(out)

        init = tuple(h_ref[b, z:z + 1, :] for b in range(n_b) for z in (0, 1))
        fin = lax.fori_loop(0, t, step, init, unroll=8)
        for b in range(n_b):
            h_ref[b, 0:1, :] = fin[2 * b]
            h_ref[b, 1:2, :] = fin[2 * b + 1]


def _lru(x4, tail, head, lru_w, h0, wpb, wmajor):
    n_b = x4.shape[0]
    nscan, t = (x4.shape[1], x4.shape[2]) if wmajor else (x4.shape[2], x4.shape[1])
    nstep = nscan // wpb
    shp = (n_b, wpb, t, W4) if wmajor else (n_b, t, wpb, W4)
    at = (lambda j: (0, j, 0, 0)) if wmajor else (lambda j: (0, 0, j, 0))
    at_rev = (lambda j: (0, nstep - 1 - j, 0, 0)) if wmajor else (lambda j: (0, 0, nstep - 1 - j, 0))
    full = lambda s: pl.BlockSpec(s, lambda j: (0,) * len(s))
    out_sds = jax.ShapeDtypeStruct(x4.shape[:3] + (W4,), F32)
    return pl.pallas_call(
        functools.partial(_lru_kernel, n_b=n_b, t=t, wpb=wpb, nscan=nscan, wmajor=wmajor),
        grid=(nstep,),
        in_specs=[pl.BlockSpec(shp, at), pl.BlockSpec(shp, at_rev), full(tail.shape), full(head.shape),
                  full((4, W4)), full((1, W4)), full((2, W4, W4)), full((2, 1, W4)),
                  full((2, W4, W4)), full((2, 1, W4)), full((2, 1, W4)), full(h0.shape)],
        out_specs=[pl.BlockSpec(shp, at), pl.BlockSpec(shp, at_rev), full(h0.shape)],
        out_shape=[out_sds, out_sds, jax.ShapeDtypeStruct(h0.shape, F32)],
        scratch_shapes=[pltpu.VMEM((n_b, 2, t + 16, W4), F32), pltpu.VMEM((n_b, 2, t, W4), F32),
                        pltpu.VMEM((n_b, 2, t, W4), F32)],
        compiler_params=_cparams("arbitrary"),
        name="rglru",
    )(x4, x4, tail, head, *lru_w, h0)


def _gla_chunk(b, z, q, k, v, wd, w_up, w0):
    rev = z == 1
    last = 0 if rev else CH - 1
    pm_incl = _pair_scan_mask(True, rev)
    la = _log_sigmoid(_dot_hilo(wd, *w_up) + w0) * (1.0 / GLA_TAU)
    bcum = _cumdot(rev, la)
    b_end = bcum[last:last + 1]
    qe = q * (GLA_DK ** -0.5) * jnp.exp(bcum)
    ke = k * jnp.exp(-bcum)
    kd = k * jnp.exp(b_end - bcum)
    e_end = jnp.exp(b_end)
    chains = []
    for p in range(GLA_HEADS // 2):
        sl = slice(128 * p, 128 * (p + 1))
        chains.append(dict(
            key=(b, z), idx=(b, z, p), mask=pm_incl, e_end=e_end[:, sl],
            qs=_b(_stack_heads(qe[:, sl])),
            ks=_b(_stack_heads(ke[:, sl])),
            kds=_b(_stack_heads(kd[:, sl])),
            vs=_b(jnp.concatenate([v[:, 2 * GLA_DV * p:2 * GLA_DV * p + GLA_DV],
                                   v[:, 2 * GLA_DV * p + GLA_DV:2 * GLA_DV * (p + 1)]], axis=0))))
    return chains


def _gla_solve(chains, s_ref):
    st = [s_ref[c["idx"]] for c in chains]
    att = [_b(jnp.where(c["mask"], _dot_nt(c["qs"], c["ks"]), 0.0)) for c in chains]
    inter = [_dot_nt(c["qs"], _b(s)) for c, s in zip(chains, st)]
    yield
    outs = {}
    for c, a, i, s in zip(chains, att, inter, st):
        o = _dot(a, c["vs"]) + i
        outs.setdefault(c["key"], []).extend([o[:CH], o[CH:]])
        s_ref[c["idx"]] = s * c["e_end"] + _dot_tn(c["vs"], c["kds"])
    return {key: jnp.concatenate(parts, axis=1) for key, parts in outs.items()}


def _mlstm_chunk(b, z, q, k, v, gates, bias):
    rev = z == 1
    m_incl = _scan_mask(CH, True, rev)
    last = 0 if rev else CH - 1
    pre = gates + bias
    lf = _log_sigmoid(pre)
    fcum = _cumdot(rev, lf)
    pre_t = pre.T
    fcum_t = fcum.T
    k = k * (ML_DH ** -0.5)
    chains = []
    for h in range(ML_HEADS):
        ci = ML_HEADS * z + h
        cf = 2 * ML_HEADS + ML_HEADS * z + h
        sl = slice(ML_DH * h, ML_DH * (h + 1))
        fc = fcum[:, cf:cf + 1]
        ic = pre[:, ci:ci + 1]
        fr = fcum_t[cf:cf + 1, :]
        ir = pre_t[ci:ci + 1, :]
        chains.append(dict(key=(b, z), idx=(b, z, h), q=q[:, sl], k=k[:, sl], v=v[:, sl],
                           qb=_b(q[:, sl]), kb=_b(k[:, sl]),
                           fc=fc, ic=ic, f_end=fc[last:last + 1],
                           dlog=jnp.where(m_incl, fc + (ir - fr), NEG)))
    return chains


def _mlstm_solve(chains, cm_ref, n_ref, m_ref):
    cm = [cm_ref[c["idx"]] for c in chains]
    n = [n_ref[c["idx"]] for c in chains]
    m = [m_ref[c["idx"]] for c in chains]
    qk = [_dot_nt(c["qb"], c["kb"]) for c in chains]
    qc = [_dot_nt(c["qb"], _b(s)) for c, s in zip(chains, cm)]
    yield
    inter = [c["fc"] + mi for c, mi in zip(chains, m)]
    mt = [jnp.maximum(i, jnp.max(c["dlog"], axis=-1, keepdims=True)) for c, i in zip(chains, inter)]
    w = [jnp.exp(c["dlog"] - t) * s for c, t, s in zip(chains, mt, qk)]
    ei = [jnp.exp(i - t) for i, t in zip(inter, mt)]
    yield
    wv = [_dot(_b(wi), _b(c["v"])) for c, wi in zip(chains, w)]
    yield
    outs = {}
    for c, wi, e, t, a, qci, cmi, ni, mi in zip(chains, w, ei, mt, wv, qc, cm, n, m):
        num = a + e * qci
        den = jnp.sum(wi, axis=-1, keepdims=True) + e * jnp.sum(c["q"] * ni, axis=-1, keepdims=True)
        outs.setdefault(c["key"], []).append(num / jnp.maximum(jnp.abs(den), jnp.exp(-t)))
        g = c["f_end"] - c["fc"] + c["ic"]
        m_new = jnp.maximum(c["f_end"] + mi, jnp.max(g, axis=0, keepdims=True))
        dec = jnp.exp(c["f_end"] + mi - m_new)
        wg = jnp.exp(g - m_new)
        cm_ref[c["idx"]] = dec * cmi + _dot_tn(_b(c["v"] * wg), c["kb"])
        n_ref[c["idx"]] = dec * ni + jnp.sum(c["k"] * wg, axis=0, keepdims=True)
        m_ref[c["idx"]] = m_new
    return {key: jnp.concatenate(parts, axis=1) for key, parts in outs.items()}


_RW_SEQ = [(0, W4, 0), (0, W4, 1), (0, W4, 2), (1, 128, 0), (1, 128, 1)]
_GLA_SEQ = [(0, 256, 0), (0, 256, 1), (0, 512, 1), (1, 128, 0)]
_ML_SEQ = [(0, W4, 0), (0, W4, 1), (0, W4, 2), (1, 128, 0)]


def _mix_kernel(*refs, n_b, nchunk):
    it = iter(refs)
    take = lambda n: [next(it) for _ in range(n)]
    rw_seq = (take(len(_RW_SEQ)), take(len(_RW_SEQ)))
    gla_seq = (take(len(_GLA_SEQ)), take(len(_GLA_SEQ)))
    ml_seq = (take(len(_ML_SEQ)), take(len(_ML_SEQ)))
    rw_par, gla_par, (ml_bias,) = take(8), take(3), take(1)
    init = take(5)
    outs = (take(2), take(2), take(2))
    state = take(5)
    rw_s, gla_s, cm_ref, n_ref, m_ref = state

    @pl.when(pl.program_id(0) == 0)
    def _():
        for dst, src in zip(state, init):
            dst[...] = src[...]

    def body(rows_f, rows_b):
        load = lambda ref, b, rows: ref[b, rows, :].astype(F32)
        rw, gla, ml = [], [], []
        for b in range(n_b):
            for z, rows in enumerate((rows_f, rows_b)):
                w_hh, w_lo, w0, a_hh, a_lo, a0, k_k, k_a = rw_par
                rw += _rwkv_chunk(b, z, *(load(ref, b, rows) for ref in rw_seq[z]),
                                  (w_hh[z], w_lo[z]), w0[z], (a_hh[z], a_lo[z]), a0[z], k_k[...], k_a[...])
                g_hh, g_lo, g0 = gla_par
                gla += _gla_chunk(b, z, *(load(ref, b, rows) for ref in gla_seq[z]), (g_hh[z], g_lo[z]), g0[z])
                ml += _mlstm_chunk(b, z, *(load(ref, b, rows) for ref in ml_seq[z]), ml_bias[...])
        results = _run_stages([_rwkv_solve(rw, rw_s), _mlstm_solve(ml, cm_ref, n_ref, m_ref), _gla_solve(gla, gla_s)])
        for (o_f, o_b), res in zip((outs[0], outs[2], outs[1]), results):
            for b in range(n_b):
                o_f[b, rows_f, :] = res[(b, 0)].astype(o_f.dtype)
                o_b[b, rows_b, :] = res[(b, 1)].astype(o_b.dtype)

    _for_chunks(nchunk, body)


def _mix(u_rw, u_gla, u_ml, rw_par, gla_par, ml_bias, states, tb=128):
    n_b, length, _ = u_rw[0].shape
    nblk = length // tb
    full = lambda a: pl.BlockSpec(a.shape, lambda j: (0,) * a.ndim)
    seq_specs, seq_args = [], []
    for u, cols in ((u_rw, _RW_SEQ), (u_gla, _GLA_SEQ), (u_ml, _ML_SEQ)):
        fwd, bwd = _seq_block_specs(n_b, tb, nblk, [(w, c) for _, w, c in cols])
        seq_specs += fwd + bwd
        seq_args += [u[a] for a, _, _ in cols] * 2
    params = list(rw_par) + list(gla_par) + [ml_bias]
    out_f, out_b = _seq_block_specs(n_b, tb, nblk, [(W4, 0)])
    out_sds = jax.ShapeDtypeStruct((n_b, length, W4), BF16)
    res = pl.pallas_call(
        functools.partial(_mix_kernel, n_b=n_b, nchunk=tb // CH),
        grid=(nblk,),
        in_specs=seq_specs + [full(a) for a in params + list(states)],
        out_specs=[out_f[0], out_b[0]] * 3 + [full(a) for a in states],
        out_shape=[out_sds] * 6 + [jax.ShapeDtypeStruct(a.shape, F32) for a in states],
        compiler_params=_cparams("arbitrary"),
        name="mix3",
    )(*seq_args, *params, *states)
    return res[0:2], res[2:4], res[4:6], res[6:]


def _branch_kernel(yaf, yab, rr, rk, rv, rg, hbf, hbb, lg, ocf, ocb, gr, odf, odb, mo,
                   rw_ln, rw_rk, g_up, gla_ln, ml_ln, e64, e128, o_ref):
    f = lambda ref: ref[...].astype(F32)
    y = f(yaf) + f(yab)
    y = y * lax.rsqrt(_headsum(y * y, e64[...]) * (1.0 / RW_HEAD) + EPS) * rw_ln[...]
    bonus = _headsum(f(rr) * f(rk) * rw_rk[...], e64[...]) * f(rv)
    g = _dot(_sigmoid(rg[...]).astype(BF16), g_up[...])
    o_ref[:, 0:W4] = ((y + bonus) * g).astype(o_ref.dtype)
    o_ref[:, W4:2 * W4] = ((hbf[...] + hbb[...]) * jax.nn.gelu(lg[...])).astype(o_ref.dtype)
    y = f(ocf) + f(ocb)
    y = y * lax.rsqrt(_headsum(y * y, e128[...]) * (1.0 / GLA_DV) + EPS) * gla_ln[...]
    r = f(gr)
    o_ref[:, 2 * W4:3 * W4] = (y * (r * _sigmoid(r))).astype(o_ref.dtype)
    y = f(odf) + f(odb)
    y = y * lax.rsqrt(_headsum(y * y, e128[...]) * (1.0 / ML_DH) + EPS) * ml_ln[...]
    o_ref[:, 3 * W4:4 * W4] = (y * _sigmoid(f(mo))).astype(o_ref.dtype)


def _branches(u_rw, u_lru, u_gla, u_ml, ya, hb, oc, od, rw_ln, rw_rk, g_up, gla_ln, ml_ln, e64, e128, tm=256):
    r = u_lru.shape[0]
    row = lambda w, c: pl.BlockSpec((tm, w), lambda i: (i, c))
    full = lambda shp: pl.BlockSpec(shp, lambda i: (0,) * len(shp))
    return pl.pallas_call(
        _branch_kernel,
        grid=(r // tm,),
        in_specs=[row(W4, 0), row(W4, 0), row(W4, 0), row(W4, 1), row(W4, 2), row(128, 2),
                  row(W4, 0), row(W4, 0), row(W4, 1),
                  row(W4, 0), row(W4, 0), row(W4, 2),
                  row(W4, 0), row(W4, 0), row(W4, 3),
                  full((1, W4)), full((1, W4)), full((128, W4)), full((1, W4)), full((1, W4)),
                  full((W4, W4)), full((W4, W4))],
        out_specs=pl.BlockSpec((tm, D), lambda i: (i, 0)),
        out_shape=jax.ShapeDtypeStruct((r, D), BF16),
        compiler_params=_cparams("parallel"),
        name="branches",
    )(ya[0], ya[1], u_rw[0], u_rw[0], u_rw[0], u_rw[1], hb[0], hb[1], u_lru, oc[0], oc[1], u_gla[0],
      od[0], od[1], u_ml[0], rw_ln, rw_rk, g_up, gla_ln, ml_ln, e64, e128)


def _block_ones(width, head):
    i = np.arange(width) // head
    return jnp.asarray((i[:, None] == i[None, :]).astype(np.float32)).astype(BF16)


def _hilo_weights(w):
    _, rank, n = w.shape
    padded = jnp.zeros((2, 128, n), F32)
    for z in range(2):
        padded = padded.at[z, z * rank:(z + 1) * rank].set(w[z])
    hi = padded.astype(BF16)
    lo = (padded - hi.astype(F32)).astype(BF16)
    return jnp.concatenate([hi, hi], axis=1), lo


def _block_diag(w):
    z, nb, c, _ = w.shape
    eye = jnp.eye(nb, dtype=w.dtype)
    return jnp.einsum("znij,nm->znimj", w, eye).reshape(z, nb * c, nb * c)


N_MAIN = (3 * W4, 0, 256 + 256 + 512 + 512, 4 * W4)


def _regroup_w_in(w):
    sizes = [512, 512, 512, 128, 128, 128, 512, 512, 256, 256, 512, 512, 32, 512, 512, 512, 512, 8, 8]
    offs = np.concatenate([[0], np.cumsum(sizes)])
    pad = lambda n: jnp.zeros(w.shape[:2] + (n,), w.dtype)
    rw = w[:, :, offs[0]:offs[6]]
    lru = w[:, :, offs[6]:offs[8]]
    gla = jnp.concatenate([w[:, :, offs[8]:offs[13]], pad(96)], axis=2)
    ml = jnp.concatenate([w[:, :, offs[13]:offs[19]], pad(112)], axis=2)
    return [rw, lru, gla, ml]


def _mixers(hx, n_b, l, lw, states, latent):
    r = hx.shape[0]
    length = r // n_b
    u_rw, (u_lru,), u_gla, u_ml = (_mm(hx, w, l, n_main) for w, n_main in zip(lw["w_in"], N_MAIN))
    seq = lambda us: [u.reshape(n_b, length, u.shape[-1]) for u in us]
    flat = lambda u: u.reshape(r, u.shape[-1])
    s_rw, s_lru, s_gla, s_ml = states

    (ya_f, ya_b), (oc_f, oc_b), (od_f, od_b), (s_rw, s_gla, *s_ml) = _mix(
        seq(u_rw), seq(u_gla), seq(u_ml), lw["rw"], lw["gla"], lw["ml_bias"], [s_rw, s_gla, *s_ml])
    if latent:
        rows = length // GRID_W
        x4 = u_lru.reshape(n_b, rows, GRID_W, LRU_COLS)
        tail = jnp.swapaxes(x4[:, rows - 2:rows, :, :W4], 1, 2)
        head = x4[:, 0, :, :W4]
        hb_f, hb_b, s_lru = _lru(x4, tail, head, lw["lru"], s_lru, wpb=8, wmajor=False)
    else:
        x4 = u_lru.reshape(n_b, 1, length, LRU_COLS)
        hb_f, hb_b, s_lru = _lru(x4, jnp.zeros((n_b, 1, 2, W4), F32), jnp.zeros((n_b, 1, W4), F32), lw["lru"], s_lru,
                                 wpb=1, wmajor=True)
    ys = _branches(u_rw, u_lru, u_gla, u_ml, (flat(ya_f), flat(ya_b)), (flat(hb_f), flat(hb_b)),
                   (flat(oc_f), flat(oc_b)), (flat(od_f), flat(od_b)), *lw["branch"])
    return ys, (s_rw, s_lru, s_gla, tuple(s_ml))


def _zero_states(n_b):
    return (jnp.zeros((n_b, 2, RW_HEADS // 2, 128, 128), F32),
            jnp.zeros((n_b, 2, W4), F32),
            jnp.zeros((n_b, 2, GLA_HEADS // 2, GLA_DV, 2 * GLA_DK), F32),
            (jnp.zeros((n_b, 2, ML_HEADS, ML_DH, ML_DH), F32),
             jnp.zeros((n_b, 2, ML_HEADS, 1, ML_DH), F32),
             jnp.full((n_b, 2, ML_HEADS, 1, 1), NEG, F32)))


def kernel(x, c, ctx, c_ctx, ada_w, ada_b, norm_mix_w, w_in, rw_w_up, rw_w0, rw_a_up, rw_a0, rw_g_up, rw_k_k, rw_k_a, rw_r_k, rw_ln_w, lru_conv_w, lru_conv_b, lru_w_a, lru_b_a, lru_w_x, lru_b_x, lru_lambda, gla_w_up, gla_w0, gla_ln_w, ml_i_b, ml_f_b, ml_ln_w, br_w, gate_w, gate_b, out_w, norm_ffn_w, ffn_w1, ffn_w2, final_norm_w):
    n_b, seq, _ = x.shape
    n_c, ctx_len, _ = ctx.shape
    xs = x.reshape(n_b * seq, D)
    cs = ctx.reshape(n_c * ctx_len, D)
    cc = jnp.zeros((8, D), F32).at[0:n_b].set(c).at[n_b].set(c_ctx)
    e64 = _block_ones(W4, RW_HEAD)
    e128 = _block_ones(W4, GLA_DV)
    row = lambda v: v.reshape(1, -1)

    mods = jnp.pad(_ada(cc, ada_w, ada_b).reshape(DEPTH, 8, 6, D), ((0, 0), (0, 0), (0, 2), (0, 0)))
    mod_x = [mods[l, 0:n_b] for l in range(DEPTH)]
    mod_c = [mods[l, n_b:n_b + 1] for l in range(DEPTH)]
    hx = _norm(xs, norm_mix_w[0], mod_x[0], (0, 1), BF16)
    hc = _norm(cs, norm_mix_w[0], mod_c[0], (0, 1), BF16)
    w_in_groups = _regroup_w_in(w_in.astype(BF16))
    gw, bw, ow = gate_w.astype(BF16), br_w.astype(BF16), out_w.astype(BF16)
    gb = gate_b[:, :, None, :]
    w1, w2 = ffn_w1.astype(BF16), ffn_w2.astype(BF16)

    for l in range(DEPTH):
        last = l == DEPTH - 1
        lw = {
            "w_in": w_in_groups,
            "rw": (*_hilo_weights(rw_w_up[l]), rw_w0[l][:, None, :], *_hilo_weights(rw_a_up[l]), rw_a0[l][:, None, :],
                   row(rw_k_k[l]), row(rw_k_a[l])),
            "lru": (lru_conv_w[l], row(lru_conv_b[l]), _block_diag(lru_w_a[l]).astype(BF16), lru_b_a[l][:, None, :],
                    _block_diag(lru_w_x[l]).astype(BF16), lru_b_x[l][:, None, :], lru_lambda[l][:, None, :]),
            "gla": (*_hilo_weights(gla_w_up[l]), gla_w0[l][:, None, :]),
            "ml_bias": jnp.zeros((1, 128), F32).at[0, 0:8].set(ml_i_b[l].reshape(-1)).at[0, 8:16].set(ml_f_b[l].reshape(-1)),
            "branch": (row(rw_ln_w[l]), row(rw_r_k[l]), rw_g_up[l].astype(BF16), row(gla_ln_w[l]), row(ml_ln_w[l]),
                       e64, e128),
        }
        ys_c, st_c = _mixers(hc, n_c, l, lw, _zero_states(n_c), latent=False)
        ys_x, _ = _mixers(hx, n_b, l, lw, st_c, latent=True)
        xs, hn = _proj_residual(_gated_sum(hx, ys_x, gw, gb, bw, l), ow, l, xs, mod_x[l], norm_ffn_w[l])
        if last:
            out, = _ffn(hn, w1, w2, l, xs, mod_x[l], final_norm_w, mod_x[l], final=True)
        else:
            xs, hx = _ffn(hn, w1, w2, l, xs, mod_x[l], norm_mix_w[l + 1], mod_x[l + 1], final=False)
            cs, hn = _proj_residual(_gated_sum(hc, ys_c, gw, gb, bw, l), ow, l, cs, mod_c[l], norm_ffn_w[l])
            cs, hc = _ffn(hn, w1, w2, l, cs, mod_c[l], norm_mix_w[l + 1], mod_c[l + 1], final=False)
    return out.reshape(n_b, seq, D)
```

```python
import functools

import numpy as np
import jax
import jax.numpy as jnp
from jax import lax
from jax.experimental import pallas as pl
from jax.experimental.pallas import tpu as pltpu

F32 = jnp.float32
BF16 = jnp.bfloat16
HI = lax.Precision.HIGHEST

D = 2048
DEPTH = 2
GRID_W = 64
EPS = 1e-6
NEG = -1e30
CH = 64
D_FF = 4 * D
W4 = D // 4

RW_HEAD = 64
RW_HEADS = W4 // RW_HEAD
RW_DECAY_SCALE = 0.6065306597
LRU_C = 8.0
GLA_HEADS = 4
GLA_DK = 64
GLA_DV = 128
GLA_TAU = 16.0
ML_HEADS = 4
ML_DH = 128

LANES = 128
LRU_COLS = 2 * W4
VMEM_LIMIT = 56 * 1024 * 1024


def _cparams(*sem):
    return pltpu.CompilerParams(dimension_semantics=sem, vmem_limit_bytes=VMEM_LIMIT)


def _dot(a, b, prec=None):
    return jnp.dot(a, b, precision=prec, preferred_element_type=F32)


def _dot_nt(a, b):
    return lax.dot_general(a, b, (((1,), (1,)), ((), ())), preferred_element_type=F32)


def _dot_tn(a, b):
    return lax.dot_general(a, b, (((0,), (0,)), ((), ())), preferred_element_type=F32)


def _b(x):
    return x.astype(BF16)


def _split_bf16(x, parts):
    out = []
    for i in range(parts):
        hi = _b(x)
        out.append(hi)
        if i < parts - 1:
            x = x - hi.astype(F32)
    return out


def _cumdot(reverse, x):
    r = lax.broadcasted_iota(jnp.int32, (CH, 4 * CH), 0)
    c = lax.broadcasted_iota(jnp.int32, (CH, 4 * CH), 1)
    s = c & (CH - 1)
    m = jnp.where(c < 3 * CH, jnp.where((r <= s) if reverse else (r >= s), 1.0, 0.0), 0.0).astype(BF16)
    parts = _split_bf16(x, 3)
    return _dot(m, jnp.concatenate(parts + [jnp.zeros_like(parts[0])], axis=0))


def _dot_hilo(x, w_hh, w_lo):
    hi, lo = _split_bf16(x, 2)
    return _dot(jnp.concatenate([hi, lo], axis=1), w_hh) + _dot(hi, w_lo)


def _headsum(x, e):
    hi, lo = _split_bf16(x, 2)
    return _dot(hi, e) + _dot(lo, e)


def _stack_heads(x):
    lo = lax.broadcasted_iota(jnp.int32, x.shape, 1) < LANES // 2
    return jnp.concatenate([jnp.where(lo, x, 0.0), jnp.where(lo, 0.0, x)], axis=0)


def _pair_scan_mask(inclusive, reverse):
    r = lax.broadcasted_iota(jnp.int32, (2 * CH, 2 * CH), 0) & (CH - 1)
    c = lax.broadcasted_iota(jnp.int32, (2 * CH, 2 * CH), 1) & (CH - 1)
    if reverse:
        return (r <= c) if inclusive else (r < c)
    return (r >= c) if inclusive else (r > c)


def _for_chunks(nchunk, body):
    for ci in range(nchunk):
        body(slice(ci * CH, (ci + 1) * CH), slice((nchunk - 1 - ci) * CH, (nchunk - ci) * CH))


def _sigmoid(x):
    return 1.0 / (1.0 + jnp.exp(-x))


def _log_sigmoid(x):
    return jnp.minimum(x, 0.0) - jnp.log1p(jnp.exp(-jnp.abs(x)))


def _softplus(x):
    return jnp.maximum(x, 0.0) + jnp.log1p(jnp.exp(-jnp.abs(x)))


def _scan_mask(n, inclusive, reverse):
    r = lax.broadcasted_iota(jnp.int32, (n, n), 0)
    c = lax.broadcasted_iota(jnp.int32, (n, n), 1)
    if reverse:
        return (r <= c) if inclusive else (r < c)
    return (r >= c) if inclusive else (r > c)


def _ada_kernel(s_ref, w_ref, b_ref, o_ref):
    s = s_ref[...]
    s = s * _sigmoid(s)
    o_ref[...] = _dot(s, w_ref[...], HI) + b_ref[...]


def _ada(cc, w, b):
    tn = 1024
    n_l = w.shape[0]
    return pl.pallas_call(
        _ada_kernel,
        grid=(n_l, 6 * D // tn),
        in_specs=[pl.BlockSpec((8, D), lambda l, j: (0, 0)),
                  pl.BlockSpec((None, D, tn), lambda l, j: (l, 0, j)),
                  pl.BlockSpec((None, 1, tn), lambda l, j: (l, 0, j))],
        out_specs=pl.BlockSpec((None, 8, tn), lambda l, j: (l, 0, j)),
        out_shape=jax.ShapeDtypeStruct((n_l, 8, 6 * D), F32),
        compiler_params=_cparams("parallel", "parallel"),
        name="ada",
    )(cc, w, b.reshape(n_l, 1, 6 * D))


def _rms_mod(x, w, m, rows):
    y = x * lax.rsqrt(jnp.mean(x * x, axis=-1, keepdims=True) + EPS) * w
    if rows is not None:
        shift, scale = rows
        y = y * (1.0 + m[scale:scale + 1]) + m[shift:shift + 1]
    return y


def _norm_kernel(x_ref, w_ref, mod_ref, o_ref, *, rows):
    o_ref[...] = _rms_mod(x_ref[...], w_ref[...], mod_ref[...], rows).astype(o_ref.dtype)


def _norm(x, w, mods, rows, out_dtype, tm=256):
    r = x.shape[0]
    g = mods.shape[0]
    per = r // g // tm
    return pl.pallas_call(
        functools.partial(_norm_kernel, rows=rows),
        grid=(r // tm,),
        in_specs=[pl.BlockSpec((tm, D), lambda i: (i, 0)),
                  pl.BlockSpec((1, D), lambda i: (0, 0)),
                  pl.BlockSpec((None, 8, D), lambda i: (i // per, 0, 0))],
        out_specs=pl.BlockSpec((tm, D), lambda i: (i, 0)),
        out_shape=jax.ShapeDtypeStruct((r, D), out_dtype),
        compiler_params=_cparams("parallel"),
        name="norm",
    )(x, w.reshape(1, D), mods)


def _mm_kernel(a_ref, w_ref, *o_refs, n_main):
    y = _dot(a_ref[...], w_ref[...])
    if n_main:
        o_refs[0][...] = y[:, :n_main].astype(BF16)
    if n_main < y.shape[1]:
        o_refs[-1][...] = y[:, n_main:]


def _mm(a, w, l, n_main, tm=512):
    r, k = a.shape
    n = w.shape[2]
    tm = min(tm, r)
    widths = [(n_main, BF16)] * bool(n_main) + [(n - n_main, F32)] * bool(n - n_main)
    return pl.pallas_call(
        functools.partial(_mm_kernel, n_main=n_main),
        grid=(r // tm,),
        in_specs=[pl.BlockSpec((tm, k), lambda i: (i, 0)),
                  pl.BlockSpec((None, k, n), lambda i: (l, 0, 0), pipeline_mode=pl.Buffered(1))],
        out_specs=[pl.BlockSpec((tm, wd), lambda i: (i, 0)) for wd, _ in widths],
        out_shape=[jax.ShapeDtypeStruct((r, wd), dt) for wd, dt in widths],
        compiler_params=_cparams("parallel"),
        name="in_proj",
    )(a, w)


def _gate_kernel(hx_ref, ys_ref, gw_ref, gb_ref, bw_ref, o_ref):
    hx = hx_ref[...]
    acc = None
    for i in range(4):
        g = _dot(hx, gw_ref[i]) + gb_ref[i]
        p = _dot(ys_ref[:, i * W4:(i + 1) * W4], bw_ref[i])
        t = _sigmoid(g) * p
        acc = t if acc is None else acc + t
    o_ref[...] = acc.astype(o_ref.dtype)


def _gated_sum(hx, ys, gw, gb, bw, l, tm=512, tn=512):
    r = hx.shape[0]
    tm = min(tm, r)
    return pl.pallas_call(
        _gate_kernel,
        grid=(D // tn, r // tm),
        in_specs=[pl.BlockSpec((tm, D), lambda j, i: (i, 0)),
                  pl.BlockSpec((tm, D), lambda j, i: (i, 0)),
                  pl.BlockSpec((None, 4, D, tn), lambda j, i: (l, 0, 0, j)),
                  pl.BlockSpec((None, 4, 1, tn), lambda j, i: (l, 0, 0, j)),
                  pl.BlockSpec((None, 4, W4, tn), lambda j, i: (l, 0, 0, j))],
        out_specs=pl.BlockSpec((tm, tn), lambda j, i: (i, j)),
        out_shape=jax.ShapeDtypeStruct((r, D), BF16),
        compiler_params=_cparams("parallel", "parallel"),
        name="gated_sum",
    )(hx, ys, gw, gb, bw)


def _proj_res_kernel(a_ref, w_ref, x_ref, mod_ref, nw_ref, o_ref, h_ref):
    m = mod_ref[...]
    x = x_ref[...] + m[2:3] * _dot(a_ref[...], w_ref[...])
    o_ref[...] = x
    h_ref[...] = _rms_mod(x, nw_ref[...], m, (3, 4)).astype(h_ref.dtype)


def _proj_residual(a, w, l, x, mods, norm_w, tm=512):
    r = a.shape[0]
    tm = min(tm, r)
    per = r // mods.shape[0] // tm
    row = pl.BlockSpec((tm, D), lambda i: (i, 0))
    return pl.pallas_call(
        _proj_res_kernel,
        grid=(r // tm,),
        in_specs=[row, pl.BlockSpec((None, D, D), lambda i: (l, 0, 0), pipeline_mode=pl.Buffered(1)), row,
                  pl.BlockSpec((None, 8, D), lambda i: (i // per, 0, 0)),
                  pl.BlockSpec((1, D), lambda i: (0, 0))],
        out_specs=[row, row],
        out_shape=[jax.ShapeDtypeStruct((r, D), F32), jax.ShapeDtypeStruct((r, D), BF16)],
        compiler_params=_cparams("parallel"),
        name="out_proj",
    )(a, w, x, mods, norm_w.reshape(1, D))


def _ffn_kernel(h_ref, w1_ref, w2_ref, x_ref, mod_ref, nw_ref, nmod_ref, *refs, final):
    o_refs, acc_ref = refs[:-1], refs[-1]
    f = pl.program_id(1)

    @pl.when(f == 0)
    def _():
        acc_ref[...] = jnp.zeros_like(acc_ref)

    a = jnp.maximum(_dot(h_ref[...], w1_ref[...]), 0.0)
    acc_ref[...] += _dot((a * a).astype(BF16), w2_ref[...])

    @pl.when(f == pl.num_programs(1) - 1)
    def _():
        x = x_ref[...] + mod_ref[...][5:6] * acc_ref[...]
        if final:
            o_refs[0][...] = _rms_mod(x, nw_ref[...], None, None)
        else:
            o_refs[0][...] = x
            o_refs[1][...] = _rms_mod(x, nw_ref[...], nmod_ref[...], (0, 1)).astype(BF16)


def _ffn(hn, w1, w2, l, x, mods, next_norm_w, next_mods, final, tm=512, tf=1024):
    r = hn.shape[0]
    tm = min(tm, r)
    per = r // mods.shape[0] // tm
    row = pl.BlockSpec((tm, D), lambda i, f: (i, 0))
    mod_spec = pl.BlockSpec((None, 8, D), lambda i, f: (i // per, 0, 0))
    out_specs = [row] if final else [row, row]
    out_shape = [jax.ShapeDtypeStruct((r, D), F32)] + ([] if final else [jax.ShapeDtypeStruct((r, D), BF16)])
    return pl.pallas_call(
        functools.partial(_ffn_kernel, final=final),
        grid=(r // tm, D_FF // tf),
        in_specs=[row, pl.BlockSpec((None, D, tf), lambda i, f: (l, 0, f)),
                  pl.BlockSpec((None, tf, D), lambda i, f: (l, f, 0)),
                  row, mod_spec, pl.BlockSpec((1, D), lambda i, f: (0, 0)), mod_spec],
        out_specs=out_specs,
        out_shape=out_shape,
        scratch_shapes=[pltpu.VMEM((tm, D), F32)],
        compiler_params=_cparams("parallel", "arbitrary"),
        name="ffn",
    )(hn, w1, w2, x, mods, next_norm_w.reshape(1, D), next_mods)


def _rwkv_chunk(b, z, r, k, v, wd, ad, w_up, w0, a_up, a0, k_k, k_a):
    rev = z == 1
    last = 0 if rev else CH - 1
    lo = lax.broadcasted_iota(jnp.int32, (CH, LANES), 1) < LANES // 2

    lw = -RW_DECAY_SCALE * _sigmoid(_dot_hilo(jnp.tanh(wd), *w_up) + w0)
    a = _sigmoid(_dot_hilo(ad, *a_up) + a0)
    kk = k * k_k
    kd = k * (1.0 + (a - 1.0) * k_a)

    lc_incl = _cumdot(rev, lw)
    p_incl = jnp.exp(lc_incl)
    p_inv = jnp.exp(-lc_incl)
    p_excl = jnp.exp(lc_incl - lw)
    p_end = p_incl[last:last + 1]

    chains = []
    for p in range(RW_HEADS // 2):
        sl = slice(LANES * p, LANES * (p + 1))
        kkp = kk[:, sl]
        sq = kkp * kkp
        nrm = jnp.where(lo, jnp.sum(jnp.where(lo, sq, 0.0), axis=-1, keepdims=True),
                        jnp.sum(jnp.where(lo, 0.0, sq), axis=-1, keepdims=True))
        kkp = kkp / jnp.maximum(jnp.sqrt(nrm), 1e-12)
        at = -kkp * p_excl[:, sl]
        bt = kkp * a[:, sl] * p_inv[:, sl]
        kt = kd[:, sl] * p_inv[:, sl]
        rt = r[:, sl] * p_incl[:, sl]
        chains.append(dict(
            z=z, key=(b, z), idx=(b, z, p), p_end=p_end[:, sl],
            ar=_b(jnp.concatenate([at, rt], axis=0)),
            bk=_b(jnp.concatenate([_stack_heads(bt), _stack_heads(kt)], axis=0)),
            vs=_b(_stack_heads(v[:, sl]))))
    return chains


def _row_scan_mask(inclusive, reverse, blocks=2):
    r = lax.broadcasted_iota(jnp.int32, (CH, blocks * CH), 0)
    c = lax.broadcasted_iota(jnp.int32, (CH, blocks * CH), 1) & (CH - 1)
    if reverse:
        return (r <= c) if inclusive else (r < c)
    return (r >= c) if inclusive else (r > c)


def _rwkv_solve(chains, s_ref):
    n2 = 2 * CH
    masks = {z: (_row_scan_mask(True, z == 1, 4), _row_scan_mask(False, z == 1)) for z in (0, 1)}
    stack = lambda x: _b(_stack_heads(x))
    s0 = [s_ref[c["idx"]] for c in chains]
    g = [_dot_nt(c["ar"], c["bk"]) for c in chains]
    s0t = [_dot_nt(c["ar"], _b(s)) for c, s in zip(chains, s0)]
    yield
    lp = [jnp.where(masks[c["z"]][1], gi[:CH, :n2], 0.0) for c, gi in zip(chains, g)]
    u = [st[:CH] + _dot(_b(jnp.where(masks[c["z"]][1], gi[:CH, n2:], 0.0)), c["vs"])
         for c, gi, st in zip(chains, g, s0t)]
    for j in range(6):
        lb = [_b(li) for li in lp]
        u = [ui + _dot(li, stack(ui)) for ui, li in zip(u, lb)]
        if j < 5:
            lp = [_dot(li, stack(lf)) for li, lf in zip(lb, lp)]
        yield
    ys = {}
    for c, gi, st, ui, s in zip(chains, g, s0t, u, s0):
        uv = jnp.concatenate([stack(ui), c["vs"]], axis=0)
        rbk = _b(jnp.where(masks[c["z"]][0], gi[CH:], 0.0))
        ys.setdefault(c["key"], []).append(st[CH:] + _dot(rbk, uv))
        s_ref[c["idx"]] = (s + _dot_tn(uv, c["bk"])) * c["p_end"]
    return {key: jnp.concatenate(parts, axis=1) for key, parts in ys.items()}


def _seq_block_specs(n_b, tb, nblk, cols):
    fwd = [pl.BlockSpec((n_b, tb, w), functools.partial(lambda j, c: (0, j, c), c=c)) for w, c in cols]
    bwd = [pl.BlockSpec((n_b, tb, w), functools.partial(lambda j, c: (0, nblk - 1 - j, c), c=c)) for w, c in cols]
    return fwd, bwd


def _run_stages(gens):
    out = [None] * len(gens)
    live = list(range(len(gens)))
    while live:
        for i in list(live):
            try:
                next(gens[i])
            except StopIteration as done:
                out[i] = done.value
                live.remove(i)
    return out


def _lru_kernel(xf, xb, tail, head, conv_w, conv_b, wa, ba, wx, bx, lam, h0_ref,
                hf_ref, hb_ref, h_ref, ext, a_s, b_s, *, n_b, t, wpb, nscan, wmajor):
    j = pl.program_id(0)
    nstep = nscan // wpb

    def rows(ref, b, w, sl):
        return ref[b, w, sl, :] if wmajor else ref[b, sl, w, :]

    def put(ref, b, w, sl, val):
        if wmajor:
            ref[b, w, sl, :] = val
        else:
            ref[b, sl, w, :] = val

    @pl.when(j == 0)
    def _():
        h_ref[...] = h0_ref[...]

    for wi in range(wpb):
        local = (wi, wpb - 1 - wi)
        glob = (j * wpb + wi, (nstep - 1 - j) * wpb + wpb - 1 - wi)
        for z, xr in enumerate((xf, xb)):
            w, g = local[z], glob[z]
            for b in range(n_b):
                if w > 0:
                    prev2 = rows(xr, b, w - 1, slice(t - 2, t))
                else:
                    prev2 = jnp.where(g > 0, tail[b, jnp.maximum(g - 1, 0)], 0.0)
                if w < wpb - 1:
                    next1 = rows(xr, b, w + 1, slice(0, 1))
                else:
                    next1 = jnp.where(g < nscan - 1, head[b, pl.ds(jnp.minimum(g + 1, nscan - 1), 1), :], 0.0)
                ext[b, z, 6:8, :] = prev2
                ext[b, z, 8:8 + t, :] = rows(xr, b, w, slice(None))
                ext[b, z, 8 + t:9 + t, :] = next1
                xc = conv_b[...] + ext[b, z, 6:6 + t, :] * conv_w[0:1, :]
                for tap in range(1, 4):
                    xc = xc + ext[b, z, 6 + tap:6 + tap + t, :] * conv_w[tap:tap + 1, :]
                xcb = _b(xc)
                rg = _sigmoid(_dot(xcb, wa[z]) + ba[z])
                ig = _sigmoid(_dot(xcb, wx[z]) + bx[z])
                log_a = -LRU_C * rg * _softplus(-lam[z])
                a = jnp.exp(log_a)
                a_s[b, z] = a
                b_s[b, z] = jnp.sqrt(-jnp.tanh(log_a) * (a * a + 1.0)) * ig * xc

        def step(i, carry):
            rf = pl.ds(i, 1)
            rb = pl.ds(t - 1 - i, 1)
            out = []
            for b in range(n_b):
                h_f = a_s[b, 0, rf, :] * carry[2 * b] + b_s[b, 0, rf, :]
                h_b = a_s[b, 1, rb, :] * carry[2 * b + 1] + b_s[b, 1, rb, :]
                put(hf_ref, b, local[0], rf, h_f)
                put(hb_ref, b, local[1], rb, h_b)
                out += [h_f, h_b]
            return tuple(out)

        init = tuple(h_ref[b, z:z + 1, :] for b in range(n_b) for z in (0, 1))
        fin = lax.fori_loop(0, t, step, init, unroll=8)
        for b in range(n_b):
            h_ref[b, 0:1, :] = fin[2 * b]
            h_ref[b, 1:2, :] = fin[2 * b + 1]


def _lru(x4, tail, head, lru_w, h0, wpb, wmajor):
    n_b = x4.shape[0]
    nscan, t = (x4.shape[1], x4.shape[2]) if wmajor else (x4.shape[2], x4.shape[1])
    nstep = nscan // wpb
    shp = (n_b, wpb, t, W4) if wmajor else (n_b, t, wpb, W4)
    at = (lambda j: (0, j, 0, 0)) if wmajor else (lambda j: (0, 0, j, 0))
    at_rev = (lambda j: (0, nstep - 1 - j, 0, 0)) if wmajor else (lambda j: (0, 0, nstep - 1 - j, 0))
    full = lambda s: pl.BlockSpec(s, lambda j: (0,) * len(s))
    out_sds = jax.ShapeDtypeStruct(x4.shape[:3] + (W4,), F32)
    return pl.pallas_call(
        functools.partial(_lru_kernel, n_b=n_b, t=t, wpb=wpb, nscan=nscan, wmajor=wmajor),
        grid=(nstep,),
        in_specs=[pl.BlockSpec(shp, at), pl.BlockSpec(shp, at_rev), full(tail.shape), full(head.shape),
                  full((4, W4)), full((1, W4)), full((2, W4, W4)), full((2, 1, W4)),
                  full((2, W4, W4)), full((2, 1, W4)), full((2, 1, W4)), full(h0.shape)],
        out_specs=[pl.BlockSpec(shp, at), pl.BlockSpec(shp, at_rev), full(h0.shape)],
        out_shape=[out_sds, out_sds, jax.ShapeDtypeStruct(h0.shape, F32)],
        scratch_shapes=[pltpu.VMEM((n_b, 2, t + 16, W4), F32), pltpu.VMEM((n_b, 2, t, W4), F32),
                        pltpu.VMEM((n_b, 2, t, W4), F32)],
        compiler_params=_cparams("arbitrary"),
        name="rglru",
    )(x4, x4, tail, head, *lru_w, h0)


def _gla_chunk(b, z, q, k, v, wd, w_up, w0):
    rev = z == 1
    last = 0 if rev else CH - 1
    pm_incl = _pair_scan_mask(True, rev)
    la = _log_sigmoid(_dot_hilo(wd, *w_up) + w0) * (1.0 / GLA_TAU)
    bcum = _cumdot(rev, la)
    b_end = bcum[last:last + 1]
    qe = q * (GLA_DK ** -0.5) * jnp.exp(bcum)
    ke = k * jnp.exp(-bcum)
    kd = k * jnp.exp(b_end - bcum)
    e_end = jnp.exp(b_end)
    chains = []
    for p in range(GLA_HEADS // 2):
        sl = slice(128 * p, 128 * (p + 1))
        chains.append(dict(
            key=(b, z), idx=(b, z, p), mask=pm_incl, e_end=e_end[:, sl],
            qs=_b(_stack_heads(qe[:, sl])),
            ks=_b(_stack_heads(ke[:, sl])),
            kds=_b(_stack_heads(kd[:, sl])),
            vs=_b(jnp.concatenate([v[:, 2 * GLA_DV * p:2 * GLA_DV * p + GLA_DV],
                                   v[:, 2 * GLA_DV * p + GLA_DV:2 * GLA_DV * (p + 1)]], axis=0))))
    return chains


def _gla_solve(chains, s_ref):
    st = [s_ref[c["idx"]] for c in chains]
    att = [_b(jnp.where(c["mask"], _dot_nt(c["qs"], c["ks"]), 0.0)) for c in chains]
    inter = [_dot_nt(c["qs"], _b(s)) for c, s in zip(chains, st)]
    yield
    outs = {}
    for c, a, i, s in zip(chains, att, inter, st):
        o = _dot(a, c["vs"]) + i
        outs.setdefault(c["key"], []).extend([o[:CH], o[CH:]])
        s_ref[c["idx"]] = s * c["e_end"] + _dot_tn(c["vs"], c["kds"])
    return {key: jnp.concatenate(parts, axis=1) for key, parts in outs.items()}


def _mlstm_chunk(b, z, q, k, v, gates, bias):
    rev = z == 1
    m_incl = _scan_mask(CH, True, rev)
    last = 0 if rev else CH - 1
    pre = gates + bias
    lf = _log_sigmoid(pre)
    fcum = _cumdot(rev, lf)
    pre_t = pre.T
    fcum_t = fcum.T
    k = k * (ML_DH ** -0.5)
    chains = []
    for h in range(ML_HEADS):
        ci = ML_HEADS * z + h
        cf = 2 * ML_HEADS + ML_HEADS * z + h
        sl = slice(ML_DH * h, ML_DH * (h + 1))
        fc = fcum[:, cf:cf + 1]
        ic = pre[:, ci:ci + 1]
        fr = fcum_t[cf:cf + 1, :]
        ir = pre_t[ci:ci + 1, :]
        chains.append(dict(key=(b, z), idx=(b, z, h), q=q[:, sl], k=k[:, sl], v=v[:, sl],
                           qb=_b(q[:, sl]), kb=_b(k[:, sl]),
                           fc=fc, ic=ic, f_end=fc[last:last + 1],
                           dlog=jnp.where(m_incl, fc + (ir - fr), NEG)))
    return chains


def _mlstm_solve(chains, cm_ref, n_ref, m_ref):
    cm = [cm_ref[c["idx"]] for c in chains]
    n = [n_ref[c["idx"]] for c in chains]
    m = [m_ref[c["idx"]] for c in chains]
    qk = [_dot_nt(c["qb"], c["kb"]) for c in chains]
    qc = [_dot_nt(c["qb"], _b(s)) for c, s in zip(chains, cm)]
    yield
    inter = [c["fc"] + mi for c, mi in zip(chains, m)]
    mt = [jnp.maximum(i, jnp.max(c["dlog"], axis=-1, keepdims=True)) for c, i in zip(chains, inter)]
    w = [jnp.exp(c["dlog"] - t) * s for c, t, s in zip(chains, mt, qk)]
    ei = [jnp.exp(i - t) for i, t in zip(inter, mt)]
    yield
    wv = [_dot(_b(wi), _b(c["v"])) for c, wi in zip(chains, w)]
    yield
    outs = {}
    for c, wi, e, t, a, qci, cmi, ni, mi in zip(chains, w, ei, mt, wv, qc, cm, n, m):
        num = a + e * qci
        den = jnp.sum(wi, axis=-1, keepdims=True) + e * jnp.sum(c["q"] * ni, axis=-1, keepdims=True)
        outs.setdefault(c["key"], []).append(num / jnp.maximum(jnp.abs(den), jnp.exp(-t)))
        g = c["f_end"] - c["fc"] + c["ic"]
        m_new = jnp.maximum(c["f_end"] + mi, jnp.max(g, axis=0, keepdims=True))
        dec = jnp.exp(c["f_end"] + mi - m_new)
        wg = jnp.exp(g - m_new)
        cm_ref[c["idx"]] = dec * cmi + _dot_tn(_b(c["v"] * wg), c["kb"])
        n_ref[c["idx"]] = dec * ni + jnp.sum(c["k"] * wg, axis=0, keepdims=True)
        m_ref[c["idx"]] = m_new
    return {key: jnp.concatenate(parts, axis=1) for key, parts in outs.items()}


_RW_SEQ = [(0, W4, 0), (0, W4, 1), (0, W4, 2), (1, 128, 0), (1, 128, 1)]
_GLA_SEQ = [(0, 256, 0), (0, 256, 1), (0, 512, 1), (1, 128, 0)]
_ML_SEQ = [(0, W4, 0), (0, W4, 1), (0, W4, 2), (1, 128, 0)]


def _mix_kernel(*refs, n_b, nchunk):
    it = iter(refs)
    take = lambda n: [next(it) for _ in range(n)]
    rw_seq = (take(len(_RW_SEQ)), take(len(_RW_SEQ)))
    gla_seq = (take(len(_GLA_SEQ)), take(len(_GLA_SEQ)))
    ml_seq = (take(len(_ML_SEQ)), take(len(_ML_SEQ)))
    rw_par, gla_par, (ml_bias,) = take(8), take(3), take(1)
    init = take(5)
    outs = (take(2), take(2), take(2))
    state = take(5)
    rw_s, gla_s, cm_ref, n_ref, m_ref = state

    @pl.when(pl.program_id(0) == 0)
    def _():
        for dst, src in zip(state, init):
            dst[...] = src[...]

    def prepare(rows_f, rows_b):
        load = lambda ref, b, rows: ref[b, rows, :].astype(F32)
        rw, gla, ml = [], [], []
        for b in range(n_b):
            for z, rows in enumerate((rows_f, rows_b)):
                w_hh, w_lo, w0, a_hh, a_lo, a0, k_k, k_a = rw_par
                rw += _rwkv_chunk(b, z, *(load(ref, b, rows) for ref in rw_seq[z]),
                                  (w_hh[z], w_lo[z]), w0[z], (a_hh[z], a_lo[z]), a0[z], k_k[...], k_a[...])
                g_hh, g_lo, g0 = gla_par
                gla += _gla_chunk(b, z, *(load(ref, b, rows) for ref in gla_seq[z]), (g_hh[z], g_lo[z]), g0[z])
                ml += _mlstm_chunk(b, z, *(load(ref, b, rows) for ref in ml_seq[z]), ml_bias[...])
        prepared.append((rows_f, rows_b, rw, gla, ml))

    prepared = []
    _for_chunks(nchunk, prepare)
    for rows_f, rows_b, rw, gla, ml in prepared:
        results = _run_stages([_rwkv_solve(rw, rw_s), _mlstm_solve(ml, cm_ref, n_ref, m_ref), _gla_solve(gla, gla_s)])
        for (o_f, o_b), res in zip((outs[0], outs[2], outs[1]), results):
            for b in range(n_b):
                o_f[b, rows_f, :] = res[(b, 0)].astype(o_f.dtype)
                o_b[b, rows_b, :] = res[(b, 1)].astype(o_b.dtype)


def _mix(u_rw, u_gla, u_ml, rw_par, gla_par, ml_bias, states, tb=128):
    n_b, length, _ = u_rw[0].shape
    nblk = length // tb
    full = lambda a: pl.BlockSpec(a.shape, lambda j: (0,) * a.ndim)
    seq_specs, seq_args = [], []
    for u, cols in ((u_rw, _RW_SEQ), (u_gla, _GLA_SEQ), (u_ml, _ML_SEQ)):
        fwd, bwd = _seq_block_specs(n_b, tb, nblk, [(w, c) for _, w, c in cols])
        seq_specs += fwd + bwd
        seq_args += [u[a] for a, _, _ in cols] * 2
    params = list(rw_par) + list(gla_par) + [ml_bias]
    out_f, out_b = _seq_block_specs(n_b, tb, nblk, [(W4, 0)])
    out_sds = jax.ShapeDtypeStruct((n_b, length, W4), BF16)
    res = pl.pallas_call(
        functools.partial(_mix_kernel, n_b=n_b, nchunk=tb // CH),
        grid=(nblk,),
        in_specs=seq_specs + [full(a) for a in params + list(states)],
        out_specs=[out_f[0], out_b[0]] * 3 + [full(a) for a in states],
        out_shape=[out_sds] * 6 + [jax.ShapeDtypeStruct(a.shape, F32) for a in states],
        compiler_params=_cparams("arbitrary"),
        name="mix3",
    )(*seq_args, *params, *states)
    return res[0:2], res[2:4], res[4:6], res[6:]


def _branch_kernel(yaf, yab, rr, rk, rv, rg, hbf, hbb, lg, ocf, ocb, gr, odf, odb, mo,
                   rw_ln, rw_rk, g_up, gla_ln, ml_ln, e64, e128, o_ref):
    f = lambda ref: ref[...].astype(F32)
    y = f(yaf) + f(yab)
    y = y * lax.rsqrt(_headsum(y * y, e64[...]) * (1.0 / RW_HEAD) + EPS) * rw_ln[...]
    bonus = _headsum(f(rr) * f(rk) * rw_rk[...], e64[...]) * f(rv)
    g = _dot(_sigmoid(rg[...]).astype(BF16), g_up[...])
    o_ref[:, 0:W4] = ((y + bonus) * g).astype(o_ref.dtype)
    o_ref[:, W4:2 * W4] = ((hbf[...] + hbb[...]) * jax.nn.gelu(lg[...])).astype(o_ref.dtype)
    y = f(ocf) + f(ocb)
    y = y * lax.rsqrt(_headsum(y * y, e128[...]) * (1.0 / GLA_DV) + EPS) * gla_ln[...]
    r = f(gr)
    o_ref[:, 2 * W4:3 * W4] = (y * (r * _sigmoid(r))).astype(o_ref.dtype)
    y = f(odf) + f(odb)
    y = y * lax.rsqrt(_headsum(y * y, e128[...]) * (1.0 / ML_DH) + EPS) * ml_ln[...]
    o_ref[:, 3 * W4:4 * W4] = (y * _sigmoid(f(mo))).astype(o_ref.dtype)


def _branches(u_rw, u_lru, u_gla, u_ml, ya, hb, oc, od, rw_ln, rw_rk, g_up, gla_ln, ml_ln, e64, e128, tm=256):
    r = u_lru.shape[0]
    row = lambda w, c: pl.BlockSpec((tm, w), lambda i: (i, c))
    full = lambda shp: pl.BlockSpec(shp, lambda i: (0,) * len(shp))
    return pl.pallas_call(
        _branch_kernel,
        grid=(r // tm,),
        in_specs=[row(W4, 0), row(W4, 0), row(W4, 0), row(W4, 1), row(W4, 2), row(128, 2),
                  row(W4, 0), row(W4, 0), row(W4, 1),
                  row(W4, 0), row(W4, 0), row(W4, 2),
                  row(W4, 0), row(W4, 0), row(W4, 3),
                  full((1, W4)), full((1, W4)), full((128, W4)), full((1, W4)), full((1, W4)),
                  full((W4, W4)), full((W4, W4))],
        out_specs=pl.BlockSpec((tm, D), lambda i: (i, 0)),
        out_shape=jax.ShapeDtypeStruct((r, D), BF16),
        compiler_params=_cparams("parallel"),
        name="branches",
    )(ya[0], ya[1], u_rw[0], u_rw[0], u_rw[0], u_rw[1], hb[0], hb[1], u_lru, oc[0], oc[1], u_gla[0],
      od[0], od[1], u_ml[0], rw_ln, rw_rk, g_up, gla_ln, ml_ln, e64, e128)


def _block_ones(width, head):
    i = np.arange(width) // head
    return jnp.asarray((i[:, None] == i[None, :]).astype(np.float32)).astype(BF16)


def _hilo_weights(w):
    _, rank, n = w.shape
    padded = jnp.zeros((2, 128, n), F32)
    for z in range(2):
        padded = padded.at[z, z * rank:(z + 1) * rank].set(w[z])
    hi = padded.astype(BF16)
    lo = (padded - hi.astype(F32)).astype(BF16)
    return jnp.concatenate([hi, hi], axis=1), lo


def _block_diag(w):
    z, nb, c, _ = w.shape
    eye = jnp.eye(nb, dtype=w.dtype)
    return jnp.einsum("znij,nm->znimj", w, eye).reshape(z, nb * c, nb * c)


N_MAIN = (3 * W4, 0, 256 + 256 + 512 + 512, 4 * W4)


def _regroup_w_in(w):
    sizes = [512, 512, 512, 128, 128, 128, 512, 512, 256, 256, 512, 512, 32, 512, 512, 512, 512, 8, 8]
    offs = np.concatenate([[0], np.cumsum(sizes)])
    pad = lambda n: jnp.zeros(w.shape[:2] + (n,), w.dtype)
    rw = w[:, :, offs[0]:offs[6]]
    lru = w[:, :, offs[6]:offs[8]]
    gla = jnp.concatenate([w[:, :, offs[8]:offs[13]], pad(96)], axis=2)
    ml = jnp.concatenate([w[:, :, offs[13]:offs[19]], pad(112)], axis=2)
    return [rw, lru, gla, ml]


def _mixers(hx, n_b, l, lw, states, latent):
    r = hx.shape[0]
    length = r // n_b
    u_rw, (u_lru,), u_gla, u_ml = (_mm(hx, w, l, n_main) for w, n_main in zip(lw["w_in"], N_MAIN))
    seq = lambda us: [u.reshape(n_b, length, u.shape[-1]) for u in us]
    flat = lambda u: u.reshape(r, u.shape[-1])
    s_rw, s_lru, s_gla, s_ml = states

    (ya_f, ya_b), (oc_f, oc_b), (od_f, od_b), (s_rw, s_gla, *s_ml) = _mix(
        seq(u_rw), seq(u_gla), seq(u_ml), lw["rw"], lw["gla"], lw["ml_bias"], [s_rw, s_gla, *s_ml])
    if latent:
        rows = length // GRID_W
        x4 = u_lru.reshape(n_b, rows, GRID_W, LRU_COLS)
        tail = jnp.swapaxes(x4[:, rows - 2:rows, :, :W4], 1, 2)
        head = x4[:, 0, :, :W4]
        hb_f, hb_b, s_lru = _lru(x4, tail, head, lw["lru"], s_lru, wpb=8, wmajor=False)
    else:
        x4 = u_lru.reshape(n_b, 1, length, LRU_COLS)
        hb_f, hb_b, s_lru = _lru(x4, jnp.zeros((n_b, 1, 2, W4), F32), jnp.zeros((n_b, 1, W4), F32), lw["lru"], s_lru,
                                 wpb=1, wmajor=True)
    ys = _branches(u_rw, u_lru, u_gla, u_ml, (flat(ya_f), flat(ya_b)), (flat(hb_f), flat(hb_b)),
                   (flat(oc_f), flat(oc_b)), (flat(od_f), flat(od_b)), *lw["branch"])
    return ys, (s_rw, s_lru, s_gla, tuple(s_ml))


def _zero_states(n_b):
    return (jnp.zeros((n_b, 2, RW_HEADS // 2, 128, 128), F32),
            jnp.zeros((n_b, 2, W4), F32),
            jnp.zeros((n_b, 2, GLA_HEADS // 2, GLA_DV, 2 * GLA_DK), F32),
            (jnp.zeros((n_b, 2, ML_HEADS, ML_DH, ML_DH), F32),
             jnp.zeros((n_b, 2, ML_HEADS, 1, ML_DH), F32),
             jnp.full((n_b, 2, ML_HEADS, 1, 1), NEG, F32)))


def kernel(x, c, ctx, c_ctx, ada_w, ada_b, norm_mix_w, w_in, rw_w_up, rw_w0, rw_a_up, rw_a0, rw_g_up, rw_k_k, rw_k_a, rw_r_k, rw_ln_w, lru_conv_w, lru_conv_b, lru_w_a, lru_b_a, lru_w_x, lru_b_x, lru_lambda, gla_w_up, gla_w0, gla_ln_w, ml_i_b, ml_f_b, ml_ln_w, br_w, gate_w, gate_b, out_w, norm_ffn_w, ffn_w1, ffn_w2, final_norm_w):
    n_b, seq, _ = x.shape
    n_c, ctx_len, _ = ctx.shape
    xs = x.reshape(n_b * seq, D)
    cs = ctx.reshape(n_c * ctx_len, D)
    cc = jnp.zeros((8, D), F32).at[0:n_b].set(c).at[n_b].set(c_ctx)
    e64 = _block_ones(W4, RW_HEAD)
    e128 = _block_ones(W4, GLA_DV)
    row = lambda v: v.reshape(1, -1)

    mods = jnp.pad(_ada(cc, ada_w, ada_b).reshape(DEPTH, 8, 6, D), ((0, 0), (0, 0), (0, 2), (0, 0)))
    mod_x = [mods[l, 0:n_b] for l in range(DEPTH)]
    mod_c = [mods[l, n_b:n_b + 1] for l in range(DEPTH)]
    hx = _norm(xs, norm_mix_w[0], mod_x[0], (0, 1), BF16)
    hc = _norm(cs, norm_mix_w[0], mod_c[0], (0, 1), BF16)
    w_in_groups = _regroup_w_in(w_in.astype(BF16))
    gw, bw, ow = gate_w.astype(BF16), br_w.astype(BF16), out_w.astype(BF16)
    gb = gate_b[:, :, None, :]
    w1, w2 = ffn_w1.astype(BF16), ffn_w2.astype(BF16)

    for l in range(DEPTH):
        last = l == DEPTH - 1
        lw = {
            "w_in": w_in_groups,
            "rw": (*_hilo_weights(rw_w_up[l]), rw_w0[l][:, None, :], *_hilo_weights(rw_a_up[l]), rw_a0[l][:, None, :],
                   row(rw_k_k[l]), row(rw_k_a[l])),
            "lru": (lru_conv_w[l], row(lru_conv_b[l]), _block_diag(lru_w_a[l]).astype(BF16), lru_b_a[l][:, None, :],
                    _block_diag(lru_w_x[l]).astype(BF16), lru_b_x[l][:, None, :], lru_lambda[l][:, None, :]),
            "gla": (*_hilo_weights(gla_w_up[l]), gla_w0[l][:, None, :]),
            "ml_bias": jnp.zeros((1, 128), F32).at[0, 0:8].set(ml_i_b[l].reshape(-1)).at[0, 8:16].set(ml_f_b[l].reshape(-1)),
            "branch": (row(rw_ln_w[l]), row(rw_r_k[l]), rw_g_up[l].astype(BF16), row(gla_ln_w[l]), row(ml_ln_w[l]),
                       e64, e128),
        }
        ys_c, st_c = _mixers(hc, n_c, l, lw, _zero_states(n_c), latent=False)
        ys_x, _ = _mixers(hx, n_b, l, lw, st_c, latent=True)
        xs, hn = _proj_residual(_gated_sum(hx, ys_x, gw, gb, bw, l), ow, l, xs, mod_x[l], norm_ffn_w[l])
        if last:
            out, = _ffn(hn, w1, w2, l, xs, mod_x[l], final_norm_w, mod_x[l], final=True)
        else:
            xs, hx = _ffn(hn, w1, w2, l, xs, mod_x[l], norm_mix_w[l + 1], mod_x[l + 1], final=False)
            cs, hn = _proj_residual(_gated_sum(hc, ys_c, gw, gb, bw, l), ow, l, cs, mod_c[l], norm_ffn_w[l])
            cs, hc = _ffn(hn, w1, w2, l, cs, mod_c[l], norm_mix_w[l + 1], mod_c[l + 1], final=False)
    return out.reshape(n_b, seq, D)
```

```python
import functools

import numpy as np
import jax
import jax.numpy as jnp
from jax import lax
from jax.experimental import pallas as pl
from jax.experimental.pallas import tpu as pltpu

F32 = jnp.float32
BF16 = jnp.bfloat16
HI = lax.Precision.HIGHEST

D = 2048
DEPTH = 2
GRID_W = 64
EPS = 1e-6
NEG = -1e30
CH = 64
D_FF = 4 * D
W4 = D // 4

RW_HEAD = 64
RW_HEADS = W4 // RW_HEAD
RW_DECAY_SCALE = 0.6065306597
LRU_C = 8.0
GLA_HEADS = 4
GLA_DK = 64
GLA_DV = 128
GLA_TAU = 16.0
ML_HEADS = 4
ML_DH = 128

LANES = 128
LRU_COLS = 2 * W4
VMEM_LIMIT = 56 * 1024 * 1024


def _cparams(*sem, vmem=VMEM_LIMIT):
    return pltpu.CompilerParams(dimension_semantics=sem, vmem_limit_bytes=vmem)


def _dot(a, b, prec=None):
    return jnp.dot(a, b, precision=prec, preferred_element_type=F32)


def _dot_nt(a, b):
    return lax.dot_general(a, b, (((1,), (1,)), ((), ())), preferred_element_type=F32)


def _dot_tn(a, b):
    return lax.dot_general(a, b, (((0,), (0,)), ((), ())), preferred_element_type=F32)


def _b(x):
    return x.astype(BF16)


def _split_bf16(x, parts):
    out = []
    for i in range(parts):
        hi = _b(x)
        out.append(hi)
        if i < parts - 1:
            x = x - hi.astype(F32)
    return out


def _cumdot(reverse, x):
    r = lax.broadcasted_iota(jnp.int32, (CH, 4 * CH), 0)
    c = lax.broadcasted_iota(jnp.int32, (CH, 4 * CH), 1)
    s = c & (CH - 1)
    m = jnp.where(c < 3 * CH, jnp.where((r <= s) if reverse else (r >= s), 1.0, 0.0), 0.0).astype(BF16)
    parts = _split_bf16(x, 3)
    return _dot(m, jnp.concatenate(parts + [jnp.zeros_like(parts[0])], axis=0))


def _dot_hilo(x, w_hh, w_lo):
    hi, lo = _split_bf16(x, 2)
    return _dot(jnp.concatenate([hi, lo], axis=1), w_hh) + _dot(hi, w_lo)


def _headsum(x, e):
    hi, lo = _split_bf16(x, 2)
    return _dot(hi, e) + _dot(lo, e)


def _stack_heads(x):
    lo = lax.broadcasted_iota(jnp.int32, x.shape, 1) < LANES // 2
    return jnp.concatenate([jnp.where(lo, x, 0.0), jnp.where(lo, 0.0, x)], axis=0)


def _pair_scan_mask(inclusive, reverse):
    r = lax.broadcasted_iota(jnp.int32, (2 * CH, 2 * CH), 0) & (CH - 1)
    c = lax.broadcasted_iota(jnp.int32, (2 * CH, 2 * CH), 1) & (CH - 1)
    if reverse:
        return (r <= c) if inclusive else (r < c)
    return (r >= c) if inclusive else (r > c)


def _for_chunks(nchunk, body):
    for ci in range(nchunk):
        body(slice(ci * CH, (ci + 1) * CH), slice((nchunk - 1 - ci) * CH, (nchunk - ci) * CH))


def _sigmoid(x):
    return 1.0 / (1.0 + jnp.exp(-x))


def _log_sigmoid(x):
    return jnp.minimum(x, 0.0) - jnp.log1p(jnp.exp(-jnp.abs(x)))


def _softplus(x):
    return jnp.maximum(x, 0.0) + jnp.log1p(jnp.exp(-jnp.abs(x)))


def _scan_mask(n, inclusive, reverse):
    r = lax.broadcasted_iota(jnp.int32, (n, n), 0)
    c = lax.broadcasted_iota(jnp.int32, (n, n), 1)
    if reverse:
        return (r <= c) if inclusive else (r < c)
    return (r >= c) if inclusive else (r > c)


def _ada_kernel(s_ref, w_ref, b_ref, o_ref):
    s = s_ref[...]
    s = s * _sigmoid(s)
    o_ref[...] = _dot(s, w_ref[...], HI) + b_ref[...]


def _ada(cc, w, b):
    tn = 1024
    n_l = w.shape[0]
    return pl.pallas_call(
        _ada_kernel,
        grid=(n_l, 6 * D // tn),
        in_specs=[pl.BlockSpec((8, D), lambda l, j: (0, 0)),
                  pl.BlockSpec((None, D, tn), lambda l, j: (l, 0, j)),
                  pl.BlockSpec((None, 1, tn), lambda l, j: (l, 0, j))],
        out_specs=pl.BlockSpec((None, 8, tn), lambda l, j: (l, 0, j)),
        out_shape=jax.ShapeDtypeStruct((n_l, 8, 6 * D), F32),
        compiler_params=_cparams("parallel", "parallel"),
        name="ada",
    )(cc, w, b.reshape(n_l, 1, 6 * D))


def _rms_mod(x, w, m, rows):
    y = x * lax.rsqrt(jnp.mean(x * x, axis=-1, keepdims=True) + EPS) * w
    if rows is not None:
        shift, scale = rows
        y = y * (1.0 + m[scale:scale + 1]) + m[shift:shift + 1]
    return y


def _norm_kernel(x_ref, w_ref, mod_ref, o_ref, *, rows):
    o_ref[...] = _rms_mod(x_ref[...], w_ref[...], mod_ref[...], rows).astype(o_ref.dtype)


def _norm(x, w, mods, rows, out_dtype, tm=256):
    r = x.shape[0]
    g = mods.shape[0]
    per = r // g // tm
    return pl.pallas_call(
        functools.partial(_norm_kernel, rows=rows),
        grid=(r // tm,),
        in_specs=[pl.BlockSpec((tm, D), lambda i: (i, 0)),
                  pl.BlockSpec((1, D), lambda i: (0, 0)),
                  pl.BlockSpec((None, 8, D), lambda i: (i // per, 0, 0))],
        out_specs=pl.BlockSpec((tm, D), lambda i: (i, 0)),
        out_shape=jax.ShapeDtypeStruct((r, D), out_dtype),
        compiler_params=_cparams("parallel"),
        name="norm",
    )(x, w.reshape(1, D), mods)


def _mm_kernel(a_ref, w_ref, *o_refs, n_main):
    y = _dot(a_ref[...], w_ref[...])
    if n_main:
        o_refs[0][...] = y[:, :n_main].astype(BF16)
    if n_main < y.shape[1]:
        o_refs[-1][...] = y[:, n_main:]


def _mm(a, w, l, n_main, tm=512):
    r, k = a.shape
    n = w.shape[2]
    tm = min(tm, r)
    widths = [(n_main, BF16)] * bool(n_main) + [(n - n_main, F32)] * bool(n - n_main)
    return pl.pallas_call(
        functools.partial(_mm_kernel, n_main=n_main),
        grid=(r // tm,),
        in_specs=[pl.BlockSpec((tm, k), lambda i: (i, 0)),
                  pl.BlockSpec((None, k, n), lambda i: (l, 0, 0), pipeline_mode=pl.Buffered(1))],
        out_specs=[pl.BlockSpec((tm, wd), lambda i: (i, 0)) for wd, _ in widths],
        out_shape=[jax.ShapeDtypeStruct((r, wd), dt) for wd, dt in widths],
        compiler_params=_cparams("parallel"),
        name="in_proj",
    )(a, w)


def _gate_kernel(hx_ref, ys_ref, gw_ref, gb_ref, bw_ref, o_ref):
    hx = hx_ref[...]
    acc = None
    for i in range(4):
        g = _dot(hx, gw_ref[i]) + gb_ref[i]
        p = _dot(ys_ref[:, i * W4:(i + 1) * W4], bw_ref[i])
        t = _sigmoid(g) * p
        acc = t if acc is None else acc + t
    o_ref[...] = acc.astype(o_ref.dtype)


def _gated_sum(hx, ys, gw, gb, bw, l, tm=512, tn=512):
    r = hx.shape[0]
    tm = min(tm, r)
    return pl.pallas_call(
        _gate_kernel,
        grid=(D // tn, r // tm),
        in_specs=[pl.BlockSpec((tm, D), lambda j, i: (i, 0)),
                  pl.BlockSpec((tm, D), lambda j, i: (i, 0)),
                  pl.BlockSpec((None, 4, D, tn), lambda j, i: (l, 0, 0, j)),
                  pl.BlockSpec((None, 4, 1, tn), lambda j, i: (l, 0, 0, j)),
                  pl.BlockSpec((None, 4, W4, tn), lambda j, i: (l, 0, 0, j))],
        out_specs=pl.BlockSpec((tm, tn), lambda j, i: (i, j)),
        out_shape=jax.ShapeDtypeStruct((r, D), BF16),
        compiler_params=_cparams("parallel", "parallel"),
        name="gated_sum",
    )(hx, ys, gw, gb, bw)


def _proj_res_kernel(a_ref, w_ref, x_ref, mod_ref, nw_ref, o_ref, h_ref):
    m = mod_ref[...]
    x = x_ref[...] + m[2:3] * _dot(a_ref[...], w_ref[...])
    o_ref[...] = x
    h_ref[...] = _rms_mod(x, nw_ref[...], m, (3, 4)).astype(h_ref.dtype)


def _proj_residual(a, w, l, x, mods, norm_w, tm=512):
    r = a.shape[0]
    tm = min(tm, r)
    per = r // mods.shape[0] // tm
    row = pl.BlockSpec((tm, D), lambda i: (i, 0))
    return pl.pallas_call(
        _proj_res_kernel,
        grid=(r // tm,),
        in_specs=[row, pl.BlockSpec((None, D, D), lambda i: (l, 0, 0), pipeline_mode=pl.Buffered(1)), row,
                  pl.BlockSpec((None, 8, D), lambda i: (i // per, 0, 0)),
                  pl.BlockSpec((1, D), lambda i: (0, 0))],
        out_specs=[row, row],
        out_shape=[jax.ShapeDtypeStruct((r, D), F32), jax.ShapeDtypeStruct((r, D), BF16)],
        compiler_params=_cparams("parallel"),
        name="out_proj",
    )(a, w, x, mods, norm_w.reshape(1, D))


FFN_SLOTS = 3
FFN_VMEM_LIMIT = 60 * 1024 * 1024


def _ffn_kernel(h_ref, w1_hbm, w2_hbm, x_ref, mod_ref, nw_ref, nmod_ref, *refs, final, l, tf):
    o_refs, (acc_ref, w1_buf, w2_buf, sem) = refs[:-4], refs[-4:]
    f = pl.program_id(1)
    nf = pl.num_programs(1)
    step = pl.program_id(0) * nf + f
    nstep = pl.num_programs(0) * nf

    def tile_copies(t):
        slot = t % FFN_SLOTS
        col = pl.multiple_of((t % nf) * tf, tf)
        return (pltpu.make_async_copy(w1_hbm.at[l, :, pl.ds(col, tf)], w1_buf.at[slot], sem.at[0, slot]),
                pltpu.make_async_copy(w2_hbm.at[l, pl.ds(col, tf), :], w2_buf.at[slot], sem.at[1, slot]))

    @pl.when(step == 0)
    def _():
        for t in range(FFN_SLOTS - 1):
            for cp in tile_copies(t):
                cp.start()

    @pl.when(step + FFN_SLOTS - 1 < nstep)
    def _():
        for cp in tile_copies(step + FFN_SLOTS - 1):
            cp.start()

    for cp in tile_copies(step):
        cp.wait()

    @pl.when(f == 0)
    def _():
        acc_ref[...] = jnp.zeros_like(acc_ref)

    slot = step % FFN_SLOTS
    a = jnp.maximum(_dot(h_ref[...], w1_buf[slot]), 0.0)
    acc_ref[...] += _dot((a * a).astype(BF16), w2_buf[slot])

    @pl.when(f == pl.num_programs(1) - 1)
    def _():
        x = x_ref[...] + mod_ref[...][5:6] * acc_ref[...]
        if final:
            o_refs[0][...] = _rms_mod(x, nw_ref[...], None, None)
        else:
            o_refs[0][...] = x
            o_refs[1][...] = _rms_mod(x, nw_ref[...], nmod_ref[...], (0, 1)).astype(BF16)


def _ffn(hn, w1, w2, l, x, mods, next_norm_w, next_mods, final, tm=512, tf=1024):
    r = hn.shape[0]
    tm = min(tm, r)
    per = r // mods.shape[0] // tm
    row = pl.BlockSpec((tm, D), lambda i, f: (i, 0))
    mod_spec = pl.BlockSpec((None, 8, D), lambda i, f: (i // per, 0, 0))
    out_specs = [row] if final else [row, row]
    out_shape = [jax.ShapeDtypeStruct((r, D), F32)] + ([] if final else [jax.ShapeDtypeStruct((r, D), BF16)])
    return pl.pallas_call(
        functools.partial(_ffn_kernel, final=final, l=l, tf=tf),
        grid=(r // tm, D_FF // tf),
        in_specs=[row, pl.BlockSpec(memory_space=pl.ANY), pl.BlockSpec(memory_space=pl.ANY),
                  row, mod_spec, pl.BlockSpec((1, D), lambda i, f: (0, 0)), mod_spec],
        out_specs=out_specs,
        out_shape=out_shape,
        scratch_shapes=[pltpu.VMEM((tm, D), F32), pltpu.VMEM((FFN_SLOTS, D, tf), BF16),
                        pltpu.VMEM((FFN_SLOTS, tf, D), BF16), pltpu.SemaphoreType.DMA((2, FFN_SLOTS))],
        compiler_params=_cparams("arbitrary", "arbitrary", vmem=FFN_VMEM_LIMIT),
        name="ffn",
    )(hn, w1, w2, x, mods, next_norm_w.reshape(1, D), next_mods)


def _rwkv_chunk(b, z, r, k, v, wd, ad, w_up, w0, a_up, a0, k_k, k_a):
    rev = z == 1
    last = 0 if rev else CH - 1
    lo = lax.broadcasted_iota(jnp.int32, (CH, LANES), 1) < LANES // 2

    lw = -RW_DECAY_SCALE * _sigmoid(_dot_hilo(jnp.tanh(wd), *w_up) + w0)
    a = _sigmoid(_dot_hilo(ad, *a_up) + a0)
    kk = k * k_k
    kd = k * (1.0 + (a - 1.0) * k_a)

    lc_incl = _cumdot(rev, lw)
    p_incl = jnp.exp(lc_incl)
    p_inv = jnp.exp(-lc_incl)
    p_excl = jnp.exp(lc_incl - lw)
    p_end = p_incl[last:last + 1]

    chains = []
    for p in range(RW_HEADS // 2):
        sl = slice(LANES * p, LANES * (p + 1))
        kkp = kk[:, sl]
        sq = kkp * kkp
        nrm = jnp.where(lo, jnp.sum(jnp.where(lo, sq, 0.0), axis=-1, keepdims=True),
                        jnp.sum(jnp.where(lo, 0.0, sq), axis=-1, keepdims=True))
        kkp = kkp / jnp.maximum(jnp.sqrt(nrm), 1e-12)
        at = -kkp * p_excl[:, sl]
        bt = kkp * a[:, sl] * p_inv[:, sl]
        kt = kd[:, sl] * p_inv[:, sl]
        rt = r[:, sl] * p_incl[:, sl]
        chains.append(dict(
            z=z, key=(b, z), idx=(b, z, p), p_end=p_end[:, sl],
            ar=_b(jnp.concatenate([at, rt], axis=0)),
            bk=_b(jnp.concatenate([_stack_heads(bt), _stack_heads(kt)], axis=0)),
            vs=_b(_stack_heads(v[:, sl]))))
    return chains


def _row_scan_mask(inclusive, reverse, blocks=2):
    r = lax.broadcasted_iota(jnp.int32, (CH, blocks * CH), 0)
    c = lax.broadcasted_iota(jnp.int32, (CH, blocks * CH), 1) & (CH - 1)
    if reverse:
        return (r <= c) if inclusive else (r < c)
    return (r >= c) if inclusive else (r > c)


def _rwkv_solve(chains, s_ref):
    n2 = 2 * CH
    masks = {z: (_row_scan_mask(True, z == 1, 4), _row_scan_mask(False, z == 1)) for z in (0, 1)}
    stack = lambda x: _b(_stack_heads(x))
    s0 = [s_ref[c["idx"]] for c in chains]
    g = [_dot_nt(c["ar"], c["bk"]) for c in chains]
    s0t = [_dot_nt(c["ar"], _b(s)) for c, s in zip(chains, s0)]
    yield
    lp = [jnp.where(masks[c["z"]][1], gi[:CH, :n2], 0.0) for c, gi in zip(chains, g)]
    u = [st[:CH] + _dot(_b(jnp.where(masks[c["z"]][1], gi[:CH, n2:], 0.0)), c["vs"])
         for c, gi, st in zip(chains, g, s0t)]
    for j in range(6):
        lb = [_b(li) for li in lp]
        u = [ui + _dot(li, stack(ui)) for ui, li in zip(u, lb)]
        if j < 5:
            lp = [_dot(li, stack(lf)) for li, lf in zip(lb, lp)]
        yield
    ys = {}
    for c, gi, st, ui, s in zip(chains, g, s0t, u, s0):
        uv = jnp.concatenate([stack(ui), c["vs"]], axis=0)
        rbk = _b(jnp.where(masks[c["z"]][0], gi[CH:], 0.0))
        ys.setdefault(c["key"], []).append(st[CH:] + _dot(rbk, uv))
        s_ref[c["idx"]] = (s + _dot_tn(uv, c["bk"])) * c["p_end"]
    return {key: jnp.concatenate(parts, axis=1) for key, parts in ys.items()}


def _seq_block_specs(n_b, tb, nblk, cols):
    fwd = [pl.BlockSpec((n_b, tb, w), functools.partial(lambda j, c: (0, j, c), c=c)) for w, c in cols]
    bwd = [pl.BlockSpec((n_b, tb, w), functools.partial(lambda j, c: (0, nblk - 1 - j, c), c=c)) for w, c in cols]
    return fwd, bwd


def _run_stages(gens):
    out = [None] * len(gens)
    live = list(range(len(gens)))
    while live:
        for i in list(live):
            try:
                next(gens[i])
            except StopIteration as done:
                out[i] = done.value
                live.remove(i)
    return out


def _lru_kernel(xf, xb, tail, head, conv_w, conv_b, wa, ba, wx, bx, lam, h0_ref,
                hf_ref, hb_ref, h_ref, ext, a_s, b_s, *, n_b, t, wpb, nscan, wmajor):
    j = pl.program_id(0)
    nstep = nscan // wpb

    def rows(ref, b, w, sl):
        return ref[b, w, sl, :] if wmajor else ref[b, sl, w, :]

    def put(ref, b, w, sl, val):
        if wmajor:
            ref[b, w, sl, :] = val
        else:
            ref[b, sl, w, :] = val

    @pl.when(j == 0)
    def _():
        h_ref[...] = h0_ref[...]

    for wi in range(wpb):
        local = (wi, wpb - 1 - wi)
        glob = (j * wpb + wi, (nstep - 1 - j) * wpb + wpb - 1 - wi)
        for z, xr in enumerate((xf, xb)):
            w, g = local[z], glob[z]
            for b in range(n_b):
                if w > 0:
                    prev2 = rows(xr, b, w - 1, slice(t - 2, t))
                else:
                    prev2 = jnp.where(g > 0, tail[b, jnp.maximum(g - 1, 0)], 0.0)
                if w < wpb - 1:
                    next1 = rows(xr, b, w + 1, slice(0, 1))
                else:
                    next1 = jnp.where(g < nscan - 1, head[b, pl.ds(jnp.minimum(g + 1, nscan - 1), 1), :], 0.0)
                ext[b, z, 6:8, :] = prev2
                ext[b, z, 8:8 + t, :] = rows(xr, b, w, slice(None))
                ext[b, z, 8 + t:9 + t, :] = next1
                xc = conv_b[...] + ext[b, z, 6:6 + t, :] * conv_w[0:1, :]
                for tap in range(1, 4):
                    xc = xc + ext[b, z, 6 + tap:6 + tap + t, :] * conv_w[tap:tap + 1, :]
                xcb = _b(xc)
                rg = _sigmoid(_dot(xcb, wa[z]) + ba[z])
                ig = _sigmoid(_dot(xcb, wx[z]) + bx[z])
                log_a = -LRU_C * rg * _softplus(-lam[z])
                a = jnp.exp(log_a)
                a_s[b, z] = a
                b_s[b, z] = jnp.sqrt(-jnp.tanh(log_a) * (a * a + 1.0)) * ig * xc

        def step(i, carry):
            rf = pl.ds(i, 1)
            rb = pl.ds(t - 1 - i, 1)
            out = []
            for b in range(n_b):
                h_f = a_s[b, 0, rf, :] * carry[2 * b] + b_s[b, 0, rf, :]
                h_b = a_s[b, 1, rb, :] * carry[2 * b + 1] + b_s[b, 1, rb, :]
                put(hf_ref, b, local[0], rf, h_f)
                put(hb_ref, b, local[1], rb, h_b)
                out += [h_f, h_b]
            return tuple(out)

        init = tuple(h_ref[b, z:z + 1, :] for b in range(n_b) for z in (0, 1))
        fin = lax.fori_loop(0, t, step, init, unroll=8)
        for b in range(n_b):
            h_ref[b, 0:1, :] = fin[2 * b]
            h_ref[b, 1:2, :] = fin[2 * b + 1]


def _lru(x4, tail, head, lru_w, h0, wpb, wmajor):
    n_b = x4.shape[0]
    nscan, t = (x4.shape[1], x4.shape[2]) if wmajor else (x4.shape[2], x4.shape[1])
    nstep = nscan // wpb
    shp = (n_b, wpb, t, W4) if wmajor else (n_b, t, wpb, W4)
    at = (lambda j: (0, j, 0, 0)) if wmajor else (lambda j: (0, 0, j, 0))
    at_rev = (lambda j: (0, nstep - 1 - j, 0, 0)) if wmajor else (lambda j: (0, 0, nstep - 1 - j, 0))
    full = lambda s: pl.BlockSpec(s, lambda j: (0,) * len(s))
    out_sds = jax.ShapeDtypeStruct(x4.shape[:3] + (W4,), F32)
    return pl.pallas_call(
        functools.partial(_lru_kernel, n_b=n_b, t=t, wpb=wpb, nscan=nscan, wmajor=wmajor),
        grid=(nstep,),
        in_specs=[pl.BlockSpec(shp, at), pl.BlockSpec(shp, at_rev), full(tail.shape), full(head.shape),
                  full((4, W4)), full((1, W4)), full((2, W4, W4)), full((2, 1, W4)),
                  full((2, W4, W4)), full((2, 1, W4)), full((2, 1, W4)), full(h0.shape)],
        out_specs=[pl.BlockSpec(shp, at), pl.BlockSpec(shp, at_rev), full(h0.shape)],
        out_shape=[out_sds, out_sds, jax.ShapeDtypeStruct(h0.shape, F32)],
        scratch_shapes=[pltpu.VMEM((n_b, 2, t + 16, W4), F32), pltpu.VMEM((n_b, 2, t, W4), F32),
                        pltpu.VMEM((n_b, 2, t, W4), F32)],
        compiler_params=_cparams("arbitrary"),
        name="rglru",
    )(x4, x4, tail, head, *lru_w, h0)


def _gla_chunk(b, z, q, k, v, wd, w_up, w0):
    rev = z == 1
    last = 0 if rev else CH - 1
    pm_incl = _pair_scan_mask(True, rev)
    la = _log_sigmoid(_dot_hilo(wd, *w_up) + w0) * (1.0 / GLA_TAU)
    bcum = _cumdot(rev, la)
    b_end = bcum[last:last + 1]
    qe = q * (GLA_DK ** -0.5) * jnp.exp(bcum)
    ke = k * jnp.exp(-bcum)
    kd = k * jnp.exp(b_end - bcum)
    e_end = jnp.exp(b_end)
    chains = []
    for p in range(GLA_HEADS // 2):
        sl = slice(128 * p, 128 * (p + 1))
        chains.append(dict(
            key=(b, z), idx=(b, z, p), mask=pm_incl, e_end=e_end[:, sl],
            qs=_b(_stack_heads(qe[:, sl])),
            ks=_b(_stack_heads(ke[:, sl])),
            kds=_b(_stack_heads(kd[:, sl])),
            vs=_b(jnp.concatenate([v[:, 2 * GLA_DV * p:2 * GLA_DV * p + GLA_DV],
                                   v[:, 2 * GLA_DV * p + GLA_DV:2 * GLA_DV * (p + 1)]], axis=0))))
    return chains


def _gla_solve(chains, s_ref):
    st = [s_ref[c["idx"]] for c in chains]
    att = [_b(jnp.where(c["mask"], _dot_nt(c["qs"], c["ks"]), 0.0)) for c in chains]
    inter = [_dot_nt(c["qs"], _b(s)) for c, s in zip(chains, st)]
    yield
    outs = {}
    for c, a, i, s in zip(chains, att, inter, st):
        o = _dot(a, c["vs"]) + i
        outs.setdefault(c["key"], []).extend([o[:CH], o[CH:]])
        s_ref[c["idx"]] = s * c["e_end"] + _dot_tn(c["vs"], c["kds"])
    return {key: jnp.concatenate(parts, axis=1) for key, parts in outs.items()}


def _mlstm_chunk(b, z, q, k, v, gates, bias):
    rev = z == 1
    m_incl = _scan_mask(CH, True, rev)
    last = 0 if rev else CH - 1
    pre = gates + bias
    lf = _log_sigmoid(pre)
    fcum = _cumdot(rev, lf)
    pre_t = pre.T
    fcum_t = fcum.T
    k = k * (ML_DH ** -0.5)
    chains = []
    for h in range(ML_HEADS):
        ci = ML_HEADS * z + h
        cf = 2 * ML_HEADS + ML_HEADS * z + h
        sl = slice(ML_DH * h, ML_DH * (h + 1))
        fc = fcum[:, cf:cf + 1]
        ic = pre[:, ci:ci + 1]
        fr = fcum_t[cf:cf + 1, :]
        ir = pre_t[ci:ci + 1, :]
        chains.append(dict(key=(b, z), idx=(b, z, h), q=q[:, sl], k=k[:, sl], v=v[:, sl],
                           qb=_b(q[:, sl]), kb=_b(k[:, sl]),
                           fc=fc, ic=ic, f_end=fc[last:last + 1],
                           dlog=jnp.where(m_incl, fc + (ir - fr), NEG)))
    return chains


def _mlstm_solve(chains, cm_ref, n_ref, m_ref):
    cm = [cm_ref[c["idx"]] for c in chains]
    n = [n_ref[c["idx"]] for c in chains]
    m = [m_ref[c["idx"]] for c in chains]
    qk = [_dot_nt(c["qb"], c["kb"]) for c in chains]
    qc = [_dot_nt(c["qb"], _b(s)) for c, s in zip(chains, cm)]
    yield
    inter = [c["fc"] + mi for c, mi in zip(chains, m)]
    mt = [jnp.maximum(i, jnp.max(c["dlog"], axis=-1, keepdims=True)) for c, i in zip(chains, inter)]
    w = [jnp.exp(c["dlog"] - t) * s for c, t, s in zip(chains, mt, qk)]
    ei = [jnp.exp(i - t) for i, t in zip(inter, mt)]
    yield
    wv = [_dot(_b(wi), _b(c["v"])) for c, wi in zip(chains, w)]
    yield
    outs = {}
    for c, wi, e, t, a, qci, cmi, ni, mi in zip(chains, w, ei, mt, wv, qc, cm, n, m):
        num = a + e * qci
        den = jnp.sum(wi, axis=-1, keepdims=True) + e * jnp.sum(c["q"] * ni, axis=-1, keepdims=True)
        outs.setdefault(c["key"], []).append(num / jnp.maximum(jnp.abs(den), jnp.exp(-t)))
        g = c["f_end"] - c["fc"] + c["ic"]
        m_new = jnp.maximum(c["f_end"] + mi, jnp.max(g, axis=0, keepdims=True))
        dec = jnp.exp(c["f_end"] + mi - m_new)
        wg = jnp.exp(g - m_new)
        cm_ref[c["idx"]] = dec * cmi + _dot_tn(_b(c["v"] * wg), c["kb"])
        n_ref[c["idx"]] = dec * ni + jnp.sum(c["k"] * wg, axis=0, keepdims=True)
        m_ref[c["idx"]] = m_new
    return {key: jnp.concatenate(parts, axis=1) for key, parts in outs.items()}


_RW_SEQ = [(0, W4, 0), (0, W4, 1), (0, W4, 2), (1, 128, 0), (1, 128, 1)]
_GLA_SEQ = [(0, 256, 0), (0, 256, 1), (0, 512, 1), (1, 128, 0)]
_ML_SEQ = [(0, W4, 0), (0, W4, 1), (0, W4, 2), (1, 128, 0)]


def _mix_kernel(*refs, n_b, nchunk):
    it = iter(refs)
    take = lambda n: [next(it) for _ in range(n)]
    rw_seq = (take(len(_RW_SEQ)), take(len(_RW_SEQ)))
    gla_seq = (take(len(_GLA_SEQ)), take(len(_GLA_SEQ)))
    ml_seq = (take(len(_ML_SEQ)), take(len(_ML_SEQ)))
    rw_par, gla_par, (ml_bias,) = take(8), take(3), take(1)
    init = take(5)
    outs = (take(2), take(2), take(2))
    state = take(5)
    rw_s, gla_s, cm_ref, n_ref, m_ref = state

    @pl.when(pl.program_id(0) == 0)
    def _():
        for dst, src in zip(state, init):
            dst[...] = src[...]

    def prepare(rows_f, rows_b):
        load = lambda ref, b, rows: ref[b, rows, :].astype(F32)
        rw, gla, ml = [], [], []
        for b in range(n_b):
            for z, rows in enumerate((rows_f, rows_b)):
                w_hh, w_lo, w0, a_hh, a_lo, a0, k_k, k_a = rw_par
                rw += _rwkv_chunk(b, z, *(load(ref, b, rows) for ref in rw_seq[z]),
                                  (w_hh[z], w_lo[z]), w0[z], (a_hh[z], a_lo[z]), a0[z], k_k[...], k_a[...])
                g_hh, g_lo, g0 = gla_par
                gla += _gla_chunk(b, z, *(load(ref, b, rows) for ref in gla_seq[z]), (g_hh[z], g_lo[z]), g0[z])
                ml += _mlstm_chunk(b, z, *(load(ref, b, rows) for ref in ml_seq[z]), ml_bias[...])
        prepared.append((rows_f, rows_b, rw, gla, ml))

    prepared = []
    _for_chunks(nchunk, prepare)
    for rows_f, rows_b, rw, gla, ml in prepared:
        results = _run_stages([_rwkv_solve(rw, rw_s), _mlstm_solve(ml, cm_ref, n_ref, m_ref), _gla_solve(gla, gla_s)])
        for (o_f, o_b), res in zip((outs[0], outs[2], outs[1]), results):
            for b in range(n_b):
                o_f[b, rows_f, :] = res[(b, 0)].astype(o_f.dtype)
                o_b[b, rows_b, :] = res[(b, 1)].astype(o_b.dtype)


def _mix(u_rw, u_gla, u_ml, rw_par, gla_par, ml_bias, states, tb=128):
    n_b, length, _ = u_rw[0].shape
    nblk = length // tb
    full = lambda a: pl.BlockSpec(a.shape, lambda j: (0,) * a.ndim)
    seq_specs, seq_args = [], []
    for u, cols in ((u_rw, _RW_SEQ), (u_gla, _GLA_SEQ), (u_ml, _ML_SEQ)):
        fwd, bwd = _seq_block_specs(n_b, tb, nblk, [(w, c) for _, w, c in cols])
        seq_specs += fwd + bwd
        seq_args += [u[a] for a, _, _ in cols] * 2
    params = list(rw_par) + list(gla_par) + [ml_bias]
    out_f, out_b = _seq_block_specs(n_b, tb, nblk, [(W4, 0)])
    out_sds = jax.ShapeDtypeStruct((n_b, length, W4), BF16)
    res = pl.pallas_call(
        functools.partial(_mix_kernel, n_b=n_b, nchunk=tb // CH),
        grid=(nblk,),
        in_specs=seq_specs + [full(a) for a in params + list(states)],
        out_specs=[out_f[0], out_b[0]] * 3 + [full(a) for a in states],
        out_shape=[out_sds] * 6 + [jax.ShapeDtypeStruct(a.shape, F32) for a in states],
        compiler_params=_cparams("arbitrary"),
        name="mix3",
    )(*seq_args, *params, *states)
    return res[0:2], res[2:4], res[4:6], res[6:]


def _branch_kernel(yaf, yab, rr, rk, rv, rg, hbf, hbb, lg, ocf, ocb, gr, odf, odb, mo,
                   rw_ln, rw_rk, g_up, gla_ln, ml_ln, e64, e128, o_ref):
    f = lambda ref: ref[...].astype(F32)
    y = f(yaf) + f(yab)
    y = y * lax.rsqrt(_headsum(y * y, e64[...]) * (1.0 / RW_HEAD) + EPS) * rw_ln[...]
    bonus = _headsum(f(rr) * f(rk) * rw_rk[...], e64[...]) * f(rv)
    g = _dot(_sigmoid(rg[...]).astype(BF16), g_up[...])
    o_ref[:, 0:W4] = ((y + bonus) * g).astype(o_ref.dtype)
    o_ref[:, W4:2 * W4] = ((hbf[...] + hbb[...]) * jax.nn.gelu(lg[...])).astype(o_ref.dtype)
    y = f(ocf) + f(ocb)
    y = y * lax.rsqrt(_headsum(y * y, e128[...]) * (1.0 / GLA_DV) + EPS) * gla_ln[...]
    r = f(gr)
    o_ref[:, 2 * W4:3 * W4] = (y * (r * _sigmoid(r))).astype(o_ref.dtype)
    y = f(odf) + f(odb)
    y = y * lax.rsqrt(_headsum(y * y, e128[...]) * (1.0 / ML_DH) + EPS) * ml_ln[...]
    o_ref[:, 3 * W4:4 * W4] = (y * _sigmoid(f(mo))).astype(o_ref.dtype)


def _branches(u_rw, u_lru, u_gla, u_ml, ya, hb, oc, od, rw_ln, rw_rk, g_up, gla_ln, ml_ln, e64, e128, tm=256):
    r = u_lru.shape[0]
    row = lambda w, c: pl.BlockSpec((tm, w), lambda i: (i, c))
    full = lambda shp: pl.BlockSpec(shp, lambda i: (0,) * len(shp))
    return pl.pallas_call(
        _branch_kernel,
        grid=(r // tm,),
        in_specs=[row(W4, 0), row(W4, 0), row(W4, 0), row(W4, 1), row(W4, 2), row(128, 2),
                  row(W4, 0), row(W4, 0), row(W4, 1),
                  row(W4, 0), row(W4, 0), row(W4, 2),
                  row(W4, 0), row(W4, 0), row(W4, 3),
                  full((1, W4)), full((1, W4)), full((128, W4)), full((1, W4)), full((1, W4)),
                  full((W4, W4)), full((W4, W4))],
        out_specs=pl.BlockSpec((tm, D), lambda i: (i, 0)),
        out_shape=jax.ShapeDtypeStruct((r, D), BF16),
        compiler_params=_cparams("parallel"),
        name="branches",
    )(ya[0], ya[1], u_rw[0], u_rw[0], u_rw[0], u_rw[1], hb[0], hb[1], u_lru, oc[0], oc[1], u_gla[0],
      od[0], od[1], u_ml[0], rw_ln, rw_rk, g_up, gla_ln, ml_ln, e64, e128)


def _block_ones(width, head):
    i = np.arange(width) // head
    return jnp.asarray((i[:, None] == i[None, :]).astype(np.float32)).astype(BF16)


def _hilo_weights(w):
    _, rank, n = w.shape
    padded = jnp.zeros((2, 128, n), F32)
    for z in range(2):
        padded = padded.at[z, z * rank:(z + 1) * rank].set(w[z])
    hi = padded.astype(BF16)
    lo = (padded - hi.astype(F32)).astype(BF16)
    return jnp.concatenate([hi, hi], axis=1), lo


def _block_diag(w):
    z, nb, c, _ = w.shape
    eye = jnp.eye(nb, dtype=w.dtype)
    return jnp.einsum("znij,nm->znimj", w, eye).reshape(z, nb * c, nb * c)


N_MAIN = (3 * W4, 0, 256 + 256 + 512 + 512, 4 * W4)


def _regroup_w_in(w):
    sizes = [512, 512, 512, 128, 128, 128, 512, 512, 256, 256, 512, 512, 32, 512, 512, 512, 512, 8, 8]
    offs = np.concatenate([[0], np.cumsum(sizes)])
    pad = lambda n: jnp.zeros(w.shape[:2] + (n,), w.dtype)
    rw = w[:, :, offs[0]:offs[6]]
    lru = w[:, :, offs[6]:offs[8]]
    gla = jnp.concatenate([w[:, :, offs[8]:offs[13]], pad(96)], axis=2)
    ml = jnp.concatenate([w[:, :, offs[13]:offs[19]], pad(112)], axis=2)
    return [rw, lru, gla, ml]


def _mixers(hx, n_b, l, lw, states, latent):
    r = hx.shape[0]
    length = r // n_b
    u_rw, (u_lru,), u_gla, u_ml = (_mm(hx, w, l, n_main) for w, n_main in zip(lw["w_in"], N_MAIN))
    seq = lambda us: [u.reshape(n_b, length, u.shape[-1]) for u in us]
    flat = lambda u: u.reshape(r, u.shape[-1])
    s_rw, s_lru, s_gla, s_ml = states

    (ya_f, ya_b), (oc_f, oc_b), (od_f, od_b), (s_rw, s_gla, *s_ml) = _mix(
        seq(u_rw), seq(u_gla), seq(u_ml), lw["rw"], lw["gla"], lw["ml_bias"], [s_rw, s_gla, *s_ml])
    if latent:
        rows = length // GRID_W
        x4 = u_lru.reshape(n_b, rows, GRID_W, LRU_COLS)
        tail = jnp.swapaxes(x4[:, rows - 2:rows, :, :W4], 1, 2)
        head = x4[:, 0, :, :W4]
        hb_f, hb_b, s_lru = _lru(x4, tail, head, lw["lru"], s_lru, wpb=8, wmajor=False)
    else:
        x4 = u_lru.reshape(n_b, 1, length, LRU_COLS)
        hb_f, hb_b, s_lru = _lru(x4, jnp.zeros((n_b, 1, 2, W4), F32), jnp.zeros((n_b, 1, W4), F32), lw["lru"], s_lru,
                                 wpb=1, wmajor=True)
    ys = _branches(u_rw, u_lru, u_gla, u_ml, (flat(ya_f), flat(ya_b)), (flat(hb_f), flat(hb_b)),
                   (flat(oc_f), flat(oc_b)), (flat(od_f), flat(od_b)), *lw["branch"])
    return ys, (s_rw, s_lru, s_gla, tuple(s_ml))


def _zero_states(n_b):
    return (jnp.zeros((n_b, 2, RW_HEADS // 2, 128, 128), F32),
            jnp.zeros((n_b, 2, W4), F32),
            jnp.zeros((n_b, 2, GLA_HEADS // 2, GLA_DV, 2 * GLA_DK), F32),
            (jnp.zeros((n_b, 2, ML_HEADS, ML_DH, ML_DH), F32),
             jnp.zeros((n_b, 2, ML_HEADS, 1, ML_DH), F32),
             jnp.full((n_b, 2, ML_HEADS, 1, 1), NEG, F32)))


def kernel(x, c, ctx, c_ctx, ada_w, ada_b, norm_mix_w, w_in, rw_w_up, rw_w0, rw_a_up, rw_a0, rw_g_up, rw_k_k, rw_k_a, rw_r_k, rw_ln_w, lru_conv_w, lru_conv_b, lru_w_a, lru_b_a, lru_w_x, lru_b_x, lru_lambda, gla_w_up, gla_w0, gla_ln_w, ml_i_b, ml_f_b, ml_ln_w, br_w, gate_w, gate_b, out_w, norm_ffn_w, ffn_w1, ffn_w2, final_norm_w):
    n_b, seq, _ = x.shape
    n_c, ctx_len, _ = ctx.shape
    xs = x.reshape(n_b * seq, D)
    cs = ctx.reshape(n_c * ctx_len, D)
    cc = jnp.zeros((8, D), F32).at[0:n_b].set(c).at[n_b].set(c_ctx)
    e64 = _block_ones(W4, RW_HEAD)
    e128 = _block_ones(W4, GLA_DV)
    row = lambda v: v.reshape(1, -1)

    mods = jnp.pad(_ada(cc, ada_w, ada_b).reshape(DEPTH, 8, 6, D), ((0, 0), (0, 0), (0, 2), (0, 0)))
    mod_x = [mods[l, 0:n_b] for l in range(DEPTH)]
    mod_c = [mods[l, n_b:n_b + 1] for l in range(DEPTH)]
    hx = _norm(xs, norm_mix_w[0], mod_x[0], (0, 1), BF16)
    hc = _norm(cs, norm_mix_w[0], mod_c[0], (0, 1), BF16)
    w_in_groups = _regroup_w_in(w_in.astype(BF16))
    gw, bw, ow = gate_w.astype(BF16), br_w.astype(BF16), out_w.astype(BF16)
    gb = gate_b[:, :, None, :]
    w1, w2 = ffn_w1.astype(BF16), ffn_w2.astype(BF16)

    for l in range(DEPTH):
        last = l == DEPTH - 1
        lw = {
            "w_in": w_in_groups,
            "rw": (*_hilo_weights(rw_w_up[l]), rw_w0[l][:, None, :], *_hilo_weights(rw_a_up[l]), rw_a0[l][:, None, :],
                   row(rw_k_k[l]), row(rw_k_a[l])),
            "lru": (lru_conv_w[l], row(lru_conv_b[l]), _block_diag(lru_w_a[l]).astype(BF16), lru_b_a[l][:, None, :],
                    _block_diag(lru_w_x[l]).astype(BF16), lru_b_x[l][:, None, :], lru_lambda[l][:, None, :]),
            "gla": (*_hilo_weights(gla_w_up[l]), gla_w0[l][:, None, :]),
            "ml_bias": jnp.zeros((1, 128), F32).at[0, 0:8].set(ml_i_b[l].reshape(-1)).at[0, 8:16].set(ml_f_b[l].reshape(-1)),
            "branch": (row(rw_ln_w[l]), row(rw_r_k[l]), rw_g_up[l].astype(BF16), row(gla_ln_w[l]), row(ml_ln_w[l]),
                       e64, e128),
        }
        ys_c, st_c = _mixers(hc, n_c, l, lw, _zero_states(n_c), latent=False)
        ys_x, _ = _mixers(hx, n_b, l, lw, st_c, latent=True)
        xs, hn = _proj_residual(_gated_sum(hx, ys_x, gw, gb, bw, l), ow, l, xs, mod_x[l], norm_ffn_w[l])
        if last:
            out, = _ffn(hn, w1, w2, l, xs, mod_x[l], final_norm_w, mod_x[l], final=True)
        else:
            xs, hx = _ffn(hn, w1, w2, l, xs, mod_x[l], norm_mix_w[l + 1], mod_x[l + 1], final=False)
            cs, hn = _proj_residual(_gated_sum(hc, ys_c, gw, gb, bw, l), ow, l, cs, mod_c[l], norm_ffn_w[l])
            cs, hc = _ffn(hn, w1, w2, l, cs, mod_c[l], norm_mix_w[l + 1], mod_c[l + 1], final=False)
    return out.reshape(n_b, seq, D)
```
